```python
import math
import jax
import jax.numpy as jnp
from jax import lax
import numpy as np

D_MODEL = 1024
BATCH = 8
SEQ = 4096
DEPTH = 2

GRID_W = 64
CTX_LEN = 256
N_EVEN = (DEPTH + 1) // 2
N_ODD = DEPTH // 2
EPS = 1e-6

NA_HEADS = 8
NA_DH = 64
NA_W = NA_HEADS * NA_DH
NA_WIN_R = 8
NA_WIN_C = 16

DN_HEADS = 4
DN_DK = 128
DN_DV = 128
DN_QK_W = DN_HEADS * DN_DK
DN_W = DN_HEADS * DN_DV
DN_CONV_W = 2 * DN_QK_W + DN_W
DN_CONV = 5
DN_CHUNK = 64

MIX0_W = NA_W + DN_W
OFF_DN = 3 * NA_W
OFF_AB = OFF_DN + DN_CONV_W
OFF_Z = OFF_AB + 4 * DN_HEADS
IN0_W = OFF_Z + MIX0_W

HY_W = D_MODEL
HY_ORDER = 2
HY_DIRS = 2
HY_CONV = 3
HY_EMB = 33
HY_BANDS = (HY_EMB - 1) // 2
HY_FFN = 64
HY_TARGET = 1e-2
HY_DECAY_MIN = math.log(HY_TARGET) / 1.5
HY_DECAY_MAX = math.log(HY_TARGET) / 0.3
IN1_W = (HY_ORDER + 1) * HY_W + HY_W

kernel_name = 'hybrid_na_gdn_hyena_prefix_dit'


def rmsnorm(s, g):
    s32 = s.astype(jnp.float32)
    return (s32 * lax.rsqrt(jnp.mean(s32 * s32, axis=-1, keepdims=True) + EPS)).astype(s.dtype) * g


def l2norm(t):
    return t * lax.rsqrt(jnp.sum(t * t, axis=-1, keepdims=True) + EPS)


def adaln_in(s, cvec, norm_g, mod_w, mod_b):
    m = jax.nn.silu(cvec) @ mod_w + mod_b
    shift, scale, gate = jnp.split(m, 3, axis=-1)
    return rmsnorm(s, norm_g) * (1.0 + scale) + shift, gate


def dwconv(u, w):
    k = w.shape[0]
    return lax.conv_general_dilated(u, w[:, None, :].astype(u.dtype), window_strides=(1,), padding=[(k // 2, k // 2)], dimension_numbers=('NWC', 'WIO', 'NWC'), feature_group_count=u.shape[-1])


def neighbourhood_attention(qx, kx, vx, kc, vc, rpb):
    b, l, h, dh = qx.shape
    rows = l // GRID_W
    wr = min(NA_WIN_R, rows)
    nwin = wr * NA_WIN_C
    qg = (qx * dh ** -0.5).reshape(b, rows, GRID_W, h, dh)
    kg = kx.reshape(b, rows, GRID_W, h, dh)
    vg = vx.reshape(b, rows, GRID_W, h, dh)
    cols = np.arange(GRID_W)
    c0 = np.clip(cols - NA_WIN_C // 2, 0, GRID_W - NA_WIN_C)
    col_idx = c0[:, None] + np.arange(NA_WIN_C)[None, :]
    dc_idx = col_idx - cols[:, None] + NA_WIN_C - 1

    def one_row(r):
        r0 = jnp.clip(r - wr // 2, 0, rows - wr)
        qr = lax.dynamic_index_in_dim(qg, r, axis=1, keepdims=False)
        kr = lax.dynamic_slice_in_dim(kg, r0, wr, axis=1)
        vr = lax.dynamic_slice_in_dim(vg, r0, wr, axis=1)
        kw = kr[:, :, col_idx]
        vw = vr[:, :, col_idx]
        s_win = jnp.einsum('bqhd,brqkhd->bhqrk', qr, kw)
        dr_idx = r0 + jnp.arange(wr) - r + NA_WIN_R - 1
        bias = rpb[:, dr_idx[:, None, None], dc_idx[None, :, :]]
        s_win = s_win + jnp.transpose(bias, (0, 2, 1, 3))[None]
        s_ctx = jnp.einsum('bqhd,bkhd->bhqk', qr, kc)
        s = jnp.concatenate([s_win.reshape(b, h, GRID_W, nwin), s_ctx], axis=-1).astype(jnp.float32)
        p = jax.nn.softmax(s, axis=-1).astype(vx.dtype)
        p_win = p[..., :nwin].reshape(b, h, GRID_W, wr, NA_WIN_C)
        return jnp.einsum('bhqrk,brqkhd->bqhd', p_win, vw) + jnp.einsum('bhqk,bkhd->bqhd', p[..., nwin:], vc)

    o = lax.map(one_row, jnp.arange(rows))
    return jnp.transpose(o, (1, 0, 2, 3, 4)).reshape(b, l, h * dh)


def context_attention(q, k, v):
    b, l, h, dh = q.shape
    s = jnp.einsum('bqhd,bkhd->bhqk', q * dh ** -0.5, k).astype(jnp.float32)
    p = jax.nn.softmax(s, axis=-1).astype(v.dtype)
    return jnp.einsum('bhqk,bkhd->bqhd', p, v).reshape(b, l, h * dh)


def gated_delta_chunked(q, k, v, g, beta, s0):
    b, l, h, dk = q.shape
    dv = v.shape[-1]
    n = l // DN_CHUNK

    def chunk(t):
        return jnp.moveaxis(t.reshape(b, n, DN_CHUNK, h, *t.shape[3:]), 3, 1)

    q, k, v, g, beta = chunk(q), chunk(k), chunk(v), chunk(g), chunk(beta)
    gc = jnp.cumsum(g, axis=-1)
    idx = jnp.arange(DN_CHUNK)
    incl = idx[:, None] >= idx[None, :]
    strict = idx[:, None] > idx[None, :]
    diff = gc[..., :, None] - gc[..., None, :]
    decay = jnp.where(incl, jnp.exp(jnp.where(incl, diff, 0.0)), 0.0)
    kb = k * beta[..., None]
    vb = v * beta[..., None]
    lmat = jnp.where(strict, jnp.einsum('bhncd,bhnsd->bhncs', kb, k) * decay, 0.0)
    rhs = jnp.concatenate([vb, kb * jnp.exp(gc)[..., None]], axis=-1)
    sol = lax.linalg.triangular_solve(lmat, rhs, left_side=True, lower=True, unit_diagonal=True)
    u, w = sol[..., :dv], sol[..., dv:]
    aqk = jnp.einsum('bhncd,bhnsd->bhncs', q, k) * decay

    def step(s, inp):
        qi, ki, ui, wi, gi, ai = inp
        v_new = ui - jnp.einsum('bhck,bhkv->bhcv', wi, s)
        o = jnp.einsum('bhck,bhkv->bhcv', qi * jnp.exp(gi)[..., None], s) + jnp.einsum('bhcs,bhsv->bhcv', ai, v_new)
        glast = gi[..., -1]
        s = s * jnp.exp(glast)[..., None, None] + jnp.einsum('bhck,bhcv->bhkv', ki * jnp.exp(glast[..., None] - gi)[..., None], v_new)
        return s, o

    xs = tuple(jnp.moveaxis(t, 2, 0) for t in (q, k, u, w, gc, aqk))
    s_fin, o = lax.scan(step, s0, xs)
    return jnp.transpose(o, (1, 0, 3, 2, 4)).reshape(b, l, h, dv), s_fin


def dn_prepare(qkv, ab, conv_w, a_log, dt_bias):
    b, l, _ = qkv.shape
    t = jax.nn.silu(dwconv(qkv, conv_w)).astype(jnp.float32)
    q = l2norm(t[..., :DN_QK_W].reshape(b, l, DN_HEADS, DN_DK)) * DN_DK ** -0.5
    k = l2norm(t[..., DN_QK_W:2 * DN_QK_W].reshape(b, l, DN_HEADS, DN_DK))
    v = t[..., 2 * DN_QK_W:].reshape(b, l, DN_HEADS, DN_DV)
    ab = ab.astype(jnp.float32).reshape(b, l, 2, 2, DN_HEADS)
    g = -jnp.exp(a_log.astype(jnp.float32)) * jax.nn.softplus(ab[:, :, 0] + dt_bias.astype(jnp.float32))
    beta = jax.nn.sigmoid(ab[:, :, 1])
    return q, k, v, g, beta


def orient(t, rev):
    return t[:, ::-1] if rev else t


def bidir_gated_deltanet(qkv_x, ab_x, qkv_c, ab_c, conv_w, a_log, dt_bias, norm_g):
    qx, kx, vx, gx, bx = dn_prepare(qkv_x, ab_x, conv_w, a_log, dt_bias)
    qc, kc, vc, gcx, bcx = dn_prepare(qkv_c, ab_c, conv_w, a_log, dt_bias)
    b = qkv_x.shape[0]
    s0 = jnp.zeros((b, DN_HEADS, DN_DK, DN_DV), jnp.float32)
    o_x = 0.0
    o_c = 0.0
    for d in range(2):
        rev = d == 1
        oc, s_ctx = gated_delta_chunked(orient(qc, rev), orient(kc, rev), orient(vc, rev), orient(gcx[:, :, d], rev), orient(bcx[:, :, d], rev), s0)
        ox, _ = gated_delta_chunked(orient(qx, rev), orient(kx, rev), orient(vx, rev), orient(gx[:, :, d], rev), orient(bx[:, :, d], rev), s_ctx)
        o_c = o_c + orient(oc, rev)
        o_x = o_x + orient(ox, rev)
    o_x = rmsnorm(o_x, norm_g).reshape(b, -1, DN_W).astype(qkv_x.dtype)
    o_c = rmsnorm(o_c, norm_g).reshape(b, -1, DN_W).astype(qkv_c.dtype)
    return o_x, o_c


def even_layer(x, ctx, c_lat, c_cx, norm_g, mod_w, mod_b, w_in, rpb, dn_conv, dn_a_log, dn_dt_bias, dn_norm_g, w_out, ctx_needed):
    hx, gate_x = adaln_in(x, c_lat, norm_g, mod_w, mod_b)
    hc, gate_c = adaln_in(ctx, c_cx, norm_g, mod_w, mod_b)
    px = hx @ w_in
    pc = hc @ w_in
    b, l, _ = px.shape
    lc = pc.shape[1]
    qx, kx, vx = (t.reshape(b, l, NA_HEADS, NA_DH) for t in jnp.split(px[..., :OFF_DN], 3, axis=-1))
    qc, kc, vc = (t.reshape(b, lc, NA_HEADS, NA_DH) for t in jnp.split(pc[..., :OFF_DN], 3, axis=-1))
    na_x = neighbourhood_attention(qx, kx, vx, kc, vc, rpb)
    dn_x, dn_c = bidir_gated_deltanet(px[..., OFF_DN:OFF_AB], px[..., OFF_AB:OFF_Z], pc[..., OFF_DN:OFF_AB], pc[..., OFF_AB:OFF_Z], dn_conv, dn_a_log, dn_dt_bias, dn_norm_g)
    x_new = x + gate_x * ((jnp.concatenate([na_x, dn_x], axis=-1) * jax.nn.silu(px[..., OFF_Z:])) @ w_out)
    if ctx_needed:
        na_c = context_attention(qc, kc, vc)
        ctx = ctx + gate_c * ((jnp.concatenate([na_c, dn_c], axis=-1) * jax.nn.silu(pc[..., OFF_Z:])) @ w_out)
    return x_new, ctx


def hyena_filters(length, w1, b1, f1, w2, b2, f2, w3):
    t = jnp.linspace(0.0, 1.0, length, dtype=jnp.float32)[:, None]
    w = 2.0 * math.pi * jnp.arange(length, dtype=jnp.float32)[:, None] / length
    f = jnp.linspace(1e-4, HY_BANDS - 1, HY_BANDS, dtype=jnp.float32)[None, :]
    feats = jnp.concatenate([t, jnp.cos(f * w), -jnp.sin(f * w)], axis=-1)
    hid = jnp.sin(f1 * (feats @ w1 + b1))
    hid = jnp.sin(f2 * (hid @ w2 + b2))
    h = (hid @ w3).astype(jnp.float32).reshape(length, HY_ORDER, HY_DIRS, HY_W)
    decay = jnp.abs(jnp.linspace(HY_DECAY_MIN, HY_DECAY_MAX, HY_W, dtype=jnp.float32))
    h = h * jnp.exp(-t * decay)[:, None, None, :]
    return h * lax.rsqrt(jnp.sum(h * h, axis=0, keepdims=True) + EPS)


def long_conv(u, h_fwd, h_bwd, skip):
    length = u.shape[1]
    n = 2 * length
    u32 = u.astype(jnp.float32)
    hf = jnp.fft.rfft(h_fwd, n=n, axis=0)
    hb = jnp.fft.rfft(h_bwd, n=n, axis=0)
    y_f = jnp.fft.irfft(jnp.fft.rfft(u32, n=n, axis=1) * hf, n=n, axis=1)[:, :length]
    y_b = jnp.fft.irfft(jnp.fft.rfft(u32[:, ::-1], n=n, axis=1) * hb, n=n, axis=1)[:, :length][:, ::-1]
    return (y_f + y_b + u32 * skip.astype(jnp.float32)).astype(u.dtype)


def hyena_layer(s, cvec, norm_g, mod_w, mod_b, w_in, conv_w, fw1, fb1, ff1, fw2, fb2, ff2, fw3, skip, w_out):
    h, gate = adaln_in(s, cvec, norm_g, mod_w, mod_b)
    p = h @ w_in
    u = dwconv(p[..., :3 * HY_W], conv_w)
    v, x1, x2 = jnp.split(u, 3, axis=-1)
    filt = hyena_filters(s.shape[1], fw1, fb1, ff1, fw2, fb2, ff2, fw3)
    z = x1 * long_conv(v, filt[:, 0, 0], filt[:, 0, 1], skip[0])
    y = x2 * long_conv(z, filt[:, 1, 0], filt[:, 1, 1], skip[1])
    return s + gate * ((y * jax.nn.silu(p[..., 3 * HY_W:])) @ w_out)


def setup_inputs(seed: int = 0) -> dict:
    key = jax.random.key(seed)
    ks = jax.random.split(key, 32)
    f32 = jnp.float32
    D = D_MODEL

    def nrm(k, shape, s):
        return jax.random.normal(k, shape, f32) * s

    dt = jnp.exp(jax.random.uniform(ks[11], (N_EVEN, 2, DN_HEADS), f32, math.log(1e-3), math.log(1e-1)))
    return {
        'x': nrm(ks[0], (BATCH, SEQ, D), 1.0),
        'c': nrm(ks[1], (BATCH, D), 1.0),
        'ctx': nrm(ks[2], (BATCH, CTX_LEN, D), 1.0),
        'c_ctx': nrm(ks[3], (D,), 1.0),
        'e_norm_g': 1.0 + nrm(ks[4], (N_EVEN, D), 0.05),
        'e_mod_w': nrm(ks[5], (N_EVEN, D, 3 * D), D ** -0.5),
        'e_mod_b': nrm(ks[6], (N_EVEN, 3 * D), 0.02),
        'e_w_in': nrm(ks[7], (N_EVEN, D, IN0_W), D ** -0.5),
        'e_na_rpb': nrm(ks[8], (N_EVEN, NA_HEADS, 2 * NA_WIN_R - 1, 2 * NA_WIN_C - 1), 0.1),
        'e_dn_conv': nrm(ks[9], (N_EVEN, DN_CONV, DN_CONV_W), DN_CONV ** -0.5),
        'e_dn_a_log': jnp.log(jax.random.uniform(ks[10], (N_EVEN, 2, DN_HEADS), f32, 1.0, 16.0)),
        'e_dn_dt_bias': dt + jnp.log(-jnp.expm1(-dt)),
        'e_dn_norm_g': 1.0 + nrm(ks[12], (N_EVEN, DN_DV), 0.05),
        'e_w_out': nrm(ks[13], (N_EVEN, MIX0_W, D), MIX0_W ** -0.5),
        'o_norm_g': 1.0 + nrm(ks[14], (N_ODD, D), 0.05),
        'o_mod_w': nrm(ks[15], (N_ODD, D, 3 * D), D ** -0.5),
        'o_mod_b': nrm(ks[16], (N_ODD, 3 * D), 0.02),
        'o_w_in': nrm(ks[17], (N_ODD, D, IN1_W), D ** -0.5),
        'o_hy_conv': nrm(ks[18], (N_ODD, HY_CONV, 3 * HY_W), HY_CONV ** -0.5),
        'o_ffn_w1': nrm(ks[19], (N_ODD, HY_EMB, HY_FFN), HY_EMB ** -0.5),
        'o_ffn_b1': nrm(ks[20], (N_ODD, HY_FFN), 0.02),
        'o_ffn_f1': 1.0 + nrm(ks[21], (N_ODD, HY_FFN), 0.05),
        'o_ffn_w2': nrm(ks[22], (N_ODD, HY_FFN, HY_FFN), HY_FFN ** -0.5),
        'o_ffn_b2': nrm(ks[23], (N_ODD, HY_FFN), 0.02),
        'o_ffn_f2': 1.0 + nrm(ks[24], (N_ODD, HY_FFN), 0.05),
        'o_ffn_w3': nrm(ks[25], (N_ODD, HY_FFN, HY_ORDER * HY_DIRS * HY_W), HY_FFN ** -0.5),
        'o_hy_skip': nrm(ks[26], (N_ODD, HY_ORDER, HY_W), 0.5),
        'o_w_out': nrm(ks[27], (N_ODD, HY_W, D), HY_W ** -0.5),
        'final_norm_g': 1.0 + nrm(ks[28], (D,), 0.05),
    }


def reference(x, c, ctx, c_ctx, e_norm_g, e_mod_w, e_mod_b, e_w_in, e_na_rpb, e_dn_conv, e_dn_a_log, e_dn_dt_bias, e_dn_norm_g, e_w_out, o_norm_g, o_mod_w, o_mod_b, o_w_in, o_hy_conv, o_ffn_w1, o_ffn_b1, o_ffn_f1, o_ffn_w2, o_ffn_b2, o_ffn_f2, o_ffn_w3, o_hy_skip, o_w_out, final_norm_g):
    c_lat = c[:, None, :]
    c_cx = c_ctx[None, None, :]
    for i in range(DEPTH):
        ctx_needed = any(j % 2 == 0 for j in range(i + 1, DEPTH))
        j = i // 2
        if i % 2 == 0:
            x, ctx = even_layer(x, ctx, c_lat, c_cx, e_norm_g[j], e_mod_w[j], e_mod_b[j], e_w_in[j], e_na_rpb[j], e_dn_conv[j], e_dn_a_log[j], e_dn_dt_bias[j], e_dn_norm_g[j], e_w_out[j], ctx_needed)
        else:
            x_new = hyena_layer(x, c_lat, o_norm_g[j], o_mod_w[j], o_mod_b[j], o_w_in[j], o_hy_conv[j], o_ffn_w1[j], o_ffn_b1[j], o_ffn_f1[j], o_ffn_w2[j], o_ffn_b2[j], o_ffn_f2[j], o_ffn_w3[j], o_hy_skip[j], o_w_out[j])
            if ctx_needed:
                ctx = hyena_layer(ctx, c_cx, o_norm_g[j], o_mod_w[j], o_mod_b[j], o_w_in[j], o_hy_conv[j], o_ffn_w1[j], o_ffn_b1[j], o_ffn_f1[j], o_ffn_w2[j], o_ffn_b2[j], o_ffn_f2[j], o_ffn_w3[j], o_hy_skip[j], o_w_out[j])
            x = x_new
    return rmsnorm(x, final_norm_g)
```

```python
import functools
import math

import numpy as np
import jax
import jax.numpy as jnp
from jax import lax
from jax.experimental import pallas as pl
from jax.experimental.pallas import tpu as pltpu

F32 = jnp.float32
BF16 = jnp.bfloat16
HI = lax.Precision.HIGHEST
EPS = 1e-6
NEG = -1e30

LANES = 128
VMEM_LIMIT_BYTES = 56 * 1024 * 1024

GRID_W = 64
NA_HEADS = 8
NA_DH = 64
NA_W = NA_HEADS * NA_DH
NA_WIN_R = 8
NA_WIN_C = 16
NA_QROWS = 4
NA_KROWS = NA_QROWS + NA_WIN_R - 1

DN_HEADS = 4
DN_DK = 128
DN_W = DN_HEADS * DN_DK
DN_CONV_W = 3 * DN_W
DN_CONV = 5
DN_CHUNK = 128

HY_W = 1024
HY_EMB = 33
HY_BANDS = (HY_EMB - 1) // 2
HY_DECAY_MIN = math.log(1e-2) / 1.5
HY_DECAY_MAX = math.log(1e-2) / 0.3


def _params(*sem):
    return pltpu.CompilerParams(dimension_semantics=sem, vmem_limit_bytes=VMEM_LIMIT_BYTES)


def _silu(v):
    return v * jax.nn.sigmoid(v)


def _dot(a, b):
    return jnp.dot(a.astype(BF16), b.astype(BF16), preferred_element_type=F32)


def _dot_nt(a, b):
    return lax.dot_general(a.astype(BF16), b.astype(BF16), (((1,), (1,)), ((), ())),
                           preferred_element_type=F32)


def _dot_tn(a, b):
    return lax.dot_general(a.astype(BF16), b.astype(BF16), (((0,), (0,)), ((), ())),
                           preferred_element_type=F32)


def _mod_kernel(c_ref, w_ref, b_ref, o_ref):
    o_ref[...] = jnp.dot(_silu(c_ref[...]), w_ref[...], precision=HI,
                         preferred_element_type=F32) + b_ref[...]


def _modulation(cvecs, w, b):
    r, d = cvecs.shape
    n = w.shape[1]
    tn = 512
    return pl.pallas_call(
        _mod_kernel,
        grid=(n // tn,),
        in_specs=[pl.BlockSpec((r, d), lambda j: (0, 0)),
                  pl.BlockSpec((d, tn), lambda j: (0, j)),
                  pl.BlockSpec((1, tn), lambda j: (0, j))],
        out_specs=pl.BlockSpec((r, tn), lambda j: (0, j)),
        out_shape=jax.ShapeDtypeStruct((r, n), F32),
        compiler_params=_params("parallel"),
        name="adaln_mod",
    )(cvecs, w, b.reshape(1, n))


def _proj_kernel(x_ref, g_ref, sh_ref, sc_ref, *refs, precs):
    n = len(precs)
    w_refs, o_refs = refs[:n], refs[n:]
    x = x_ref[0]
    ms = jnp.mean(x * x, axis=-1, keepdims=True)
    h = x * lax.rsqrt(ms + EPS) * g_ref[...] * (1.0 + sc_ref[0]) + sh_ref[0]
    hb = h.astype(BF16)
    for w_ref, o_ref, prec in zip(w_refs, o_refs, precs):
        if prec == "bf16":
            r = jnp.dot(hb, w_ref[...], preferred_element_type=F32)
        else:
            r = jnp.dot(h, w_ref[...], precision=HI, preferred_element_type=F32)
        o_ref[0] = r.astype(o_ref.dtype)


def _norm_proj(x, norm_g, shift, scale, sections, tm):
    b, l, d = x.shape
    ws, precs, out_shapes, out_specs, w_specs = [], [], [], [], []
    for w, prec, odt in sections:
        ws.append(w.astype(BF16) if prec == "bf16" else w)
        precs.append(prec)
        n = w.shape[1]
        w_specs.append(pl.BlockSpec((d, n), lambda bi, i: (0, 0)))
        out_specs.append(pl.BlockSpec((1, tm, n), lambda bi, i: (bi, i, 0)))
        out_shapes.append(jax.ShapeDtypeStruct((b, l, n), odt))
    vec = pl.BlockSpec((1, 1, d), lambda bi, i: (bi, 0, 0))
    return pl.pallas_call(
        functools.partial(_proj_kernel, precs=tuple(precs)),
        grid=(b, l // tm),
        in_specs=[pl.BlockSpec((1, tm, d), lambda bi, i: (bi, i, 0)),
                  pl.BlockSpec((1, d), lambda bi, i: (0, 0)), vec, vec] + w_specs,
        out_specs=out_specs,
        out_shape=out_shapes,
        compiler_params=_params("parallel", "parallel"),
        name="norm_proj",
    )(x, norm_g.reshape(1, d), shift, scale, *ws)


def _na_bias_table(rpb, rows):
    wr = min(NA_WIN_R, rows)
    tables = []
    i = np.arange(NA_QROWS)[:, None, None, None]
    qc = np.arange(GRID_W)[None, :, None, None]
    j = np.arange(NA_KROWS)[None, None, :, None]
    kc = np.arange(GRID_W)[None, None, None, :]
    for r_first, k_first in ((0, 0), (NA_QROWS, 0), (rows - NA_QROWS, rows - NA_KROWS)):
        qr = r_first + i
        kr = k_first + j
        r0 = np.clip(qr - wr // 2, 0, rows - wr)
        c0 = np.clip(qc - NA_WIN_C // 2, 0, GRID_W - NA_WIN_C)
        valid = (kr >= r0) & (kr < r0 + wr) & (kc >= c0) & (kc < c0 + NA_WIN_C)
        dr = np.clip(kr - qr + NA_WIN_R - 1, 0, 2 * NA_WIN_R - 2)
        dc = np.clip(kc - qc + NA_WIN_C - 1, 0, 2 * NA_WIN_C - 2)
        shape = (NA_QROWS, GRID_W, NA_KROWS, GRID_W)
        dr = np.broadcast_to(dr, shape)
        dc = np.broadcast_to(dc, shape)
        valid = np.broadcast_to(valid, shape)
        t = jnp.where(valid[None], rpb[:, dr, dc], NEG)
        tables.append(t.reshape(NA_HEADS, NA_QROWS * GRID_W, NA_KROWS * GRID_W))
    return jnp.stack(tables)


def _na_kernel(q_ref, k_ref, v_ref, kc_ref, vc_ref, bias_ref, o_ref, *, rows):
    blk = pl.program_id(1)
    nq = NA_QROWS * GRID_W
    nk = NA_KROWS * GRID_W
    k_first = jnp.clip(blk * NA_QROWS - NA_WIN_R // 2, 0, rows - NA_KROWS)
    start = pl.multiple_of(k_first * GRID_W, GRID_W)
    lane = lax.broadcasted_iota(jnp.int32, (nq, LANES), 1)
    low = lane < NA_DH
    for p in range(NA_W // LANES):
        cs = slice(p * LANES, (p + 1) * LANES)
        q2 = q_ref[0, :, cs]
        k2 = k_ref[0, pl.ds(start, nk), cs]
        v2 = v_ref[0, pl.ds(start, nk), cs]
        kc2 = kc_ref[0, :, cs]
        vc2 = vc_ref[0, :, cs]
        halves = []
        for hh in range(2):
            sel = low if hh == 0 else jnp.logical_not(low)
            qh = jnp.where(sel, q2, jnp.zeros_like(q2))
            s_win = _dot_nt(qh, k2) * (NA_DH ** -0.5) + bias_ref[0, 2 * p + hh]
            s_ctx = _dot_nt(qh, kc2) * (NA_DH ** -0.5)
            m = jnp.maximum(jnp.max(s_win, axis=-1, keepdims=True),
                            jnp.max(s_ctx, axis=-1, keepdims=True))
            p_win = jnp.exp(s_win - m)
            p_ctx = jnp.exp(s_ctx - m)
            den = jnp.sum(p_win, axis=-1, keepdims=True) + jnp.sum(p_ctx, axis=-1, keepdims=True)
            halves.append((_dot(p_win, v2) + _dot(p_ctx, vc2)) / den)
        o_ref[0, :, cs] = jnp.where(low, halves[0], halves[1])


def _neighbourhood_attention(qkv, qkv_c, rpb):
    b, l, _ = qkv.shape
    lc = qkv_c.shape[1]
    rows = l // GRID_W
    assert rows % NA_QROWS == 0 and rows >= NA_KROWS + 1
    nq = NA_QROWS * GRID_W
    nblk = rows // NA_QROWS
    bias = _na_bias_table(rpb, rows)

    def cfg(bi, i):
        return (jnp.where(i == 0, 0, jnp.where(i == nblk - 1, 2, 1)), 0, 0, 0)

    return pl.pallas_call(
        functools.partial(_na_kernel, rows=rows),
        grid=(b, nblk),
        in_specs=[pl.BlockSpec((1, nq, NA_W), lambda bi, i: (bi, i, 0)),
                  pl.BlockSpec((1, l, NA_W), lambda bi, i: (bi, 0, 1)),
                  pl.BlockSpec((1, l, NA_W), lambda bi, i: (bi, 0, 2)),
                  pl.BlockSpec((1, lc, NA_W), lambda bi, i: (bi, 0, 1)),
                  pl.BlockSpec((1, lc, NA_W), lambda bi, i: (bi, 0, 2)),
                  pl.BlockSpec((1, NA_HEADS, nq, NA_KROWS * GRID_W), cfg)],
        out_specs=pl.BlockSpec((1, nq, NA_W), lambda bi, i: (bi, i, 0)),
        out_shape=jax.ShapeDtypeStruct((b, l, NA_W), F32),
        compiler_params=_params("parallel", "arbitrary"),
        name="neighbourhood_attention",
    )(qkv, qkv, qkv, qkv_c, qkv_c, bias)


def _halo_specs(tl, width, l):
    nb8 = tl // 8
    last8 = l // 8 - 1
    return [pl.BlockSpec((1, tl, width), lambda bi, i: (bi, i, 0)),
            pl.BlockSpec((1, 8, width), lambda bi, i: (bi, jnp.maximum(i * nb8 - 1, 0), 0)),
            pl.BlockSpec((1, 8, width), lambda bi, i: (bi, jnp.minimum((i + 1) * nb8, last8), 0))]


def _dwconv_tile(x_ref, xp_ref, xn_ref, cw_ref, taps):
    i = pl.program_id(1)
    x = x_ref[0]
    tl = x.shape[0]
    prev = jnp.where(i > 0, xp_ref[0], 0.0)
    nxt = jnp.where(i < pl.num_programs(1) - 1, xn_ref[0], 0.0)
    xe = jnp.concatenate([prev, x, nxt], axis=0)
    half = taps // 2
    acc = jnp.zeros_like(x)
    for j in range(taps):
        off = 8 - half + j
        acc = acc + xe[off:off + tl] * cw_ref[j:j + 1, :]
    return acc


def _dnprep_kernel(x_ref, xp_ref, xn_ref, ab_ref, cw_ref, al_ref, dtb_ref, qkv_ref, gb_ref):
    t = _silu(_dwconv_tile(x_ref, xp_ref, xn_ref, cw_ref, DN_CONV))
    segs = []
    for hh in range(3 * DN_HEADS):
        seg = t[:, hh * DN_DK:(hh + 1) * DN_DK]
        if hh < 2 * DN_HEADS:
            seg = seg * lax.rsqrt(jnp.sum(seg * seg, axis=-1, keepdims=True) + EPS)
        if hh < DN_HEADS:
            seg = seg * (DN_DK ** -0.5)
        segs.append(seg)
    qkv_ref[0] = jnp.concatenate(segs, axis=1)
    ab = ab_ref[0]
    z = ab + dtb_ref[...]
    softplus = jnp.maximum(z, 0.0) + jnp.log(1.0 + jnp.exp(-jnp.abs(z)))
    g = -jnp.exp(al_ref[...]) * softplus
    lane = lax.broadcasted_iota(jnp.int32, ab.shape, 1)
    gb_ref[0] = jnp.where(lane < 2 * DN_HEADS, g, jax.nn.sigmoid(ab))


def _dn_prepare(qkv, ab, conv_w, a_log, dt_bias, tl):
    b, l, w = qkv.shape
    pad = LANES - 2 * DN_HEADS
    al = jnp.pad(a_log.reshape(1, 2 * DN_HEADS), ((0, 0), (0, pad)))
    dtb = jnp.pad(dt_bias.reshape(1, 2 * DN_HEADS), ((0, 0), (0, pad)))
    const = lambda shape: pl.BlockSpec(shape, lambda bi, i: (0, 0))
    return pl.pallas_call(
        _dnprep_kernel,
        grid=(b, l // tl),
        in_specs=_halo_specs(tl, w, l) + [pl.BlockSpec((1, tl, LANES), lambda bi, i: (bi, i, 0)),
                                         const((DN_CONV, w)), const((1, LANES)), const((1, LANES))],
        out_specs=[pl.BlockSpec((1, tl, w), lambda bi, i: (bi, i, 0)),
                   pl.BlockSpec((1, tl, LANES), lambda bi, i: (bi, i, 0))],
        out_shape=[jax.ShapeDtypeStruct((b, l, w), F32), jax.ShapeDtypeStruct((b, l, LANES), F32)],
        compiler_params=_params("parallel", "parallel"),
        name="dn_prepare",
    )(qkv, qkv, qkv, ab, conv_w, al, dtb)


DN_INV_BASE = 16


def _unit_triangular_inverse(nil, eye, row, col):
    c = nil.shape[0]
    sh = int(math.log2(DN_INV_BASE))
    diag = jnp.where((row >> sh) == (col >> sh), nil, 0.0)
    inv = eye - diag
    pw = diag
    for _ in range(sh - 1):
        pw = _dot(pw, pw)
        inv = inv + _dot(inv, pw)
    while (1 << sh) < c:
        off = jnp.where(((row >> (sh + 1)) == (col >> (sh + 1))) & ((row >> sh) != (col >> sh)), nil, 0.0)
        inv = inv - _dot(inv, _dot(off, inv))
        sh += 1
    return inv


def _dn_scan_kernel(xf_ref, xb_ref, gf_ref, gbk_ref, s0_ref, of_ref, ob_ref, sfin_ref, s_ref):
    t = pl.program_id(1)
    c = DN_CHUNK

    @pl.when(t == 0)
    def _():
        s_ref[...] = s0_ref[0]

    row = lax.broadcasted_iota(jnp.int32, (c, c), 0)
    col = lax.broadcasted_iota(jnp.int32, (c, c), 1)
    eye = (row == col).astype(F32)
    for d in range(2):
        x_ref, g_ref, o_ref = (xf_ref, gf_ref, of_ref) if d == 0 else (xb_ref, gbk_ref, ob_ref)
        incl = (row >= col) if d == 0 else (row <= col)
        strict = (row > col) if d == 0 else (row < col)
        tri = incl.astype(F32)
        gb = g_ref[0]
        for h in range(DN_HEADS):
            q = x_ref[0, :, h * DN_DK:(h + 1) * DN_DK]
            k = x_ref[0, :, DN_W + h * DN_DK:DN_W + (h + 1) * DN_DK]
            v = x_ref[0, :, 2 * DN_W + h * DN_DK:2 * DN_W + (h + 1) * DN_DK]
            ci = d * DN_HEADS + h
            g_b = jnp.broadcast_to(gb[:, ci:ci + 1], (c, c))
            beta = gb[:, 2 * DN_HEADS + ci:2 * DN_HEADS + ci + 1]
            gc_rows = jnp.dot(tri, g_b, precision=HI, preferred_element_type=F32)
            gc_cols = gc_rows.T
            decay = jnp.where(incl, jnp.exp(jnp.where(incl, gc_rows - gc_cols, 0.0)), 0.0)
            kb = k * beta
            vb = v * beta
            nil = jnp.where(strict, _dot_nt(kb, k) * decay, 0.0)
            aqk = _dot_nt(q, k) * decay
            egc = jnp.exp(gc_rows)
            inv = _unit_triangular_inverse(nil, eye, row, col)
            sol = _dot(inv, jnp.concatenate([vb, kb * egc], axis=1))
            u, w = sol[:, :DN_DK], sol[:, DN_DK:]
            s = s_ref[d, h]
            v_new = u - _dot(w, s)
            o_ref[0, :, h * DN_DK:(h + 1) * DN_DK] = _dot(q * egc, s) + _dot(aqk, v_new)
            g_last = gc_rows[c - 1:c, :] if d == 0 else gc_rows[0:1, :]
            s_ref[d, h] = s * jnp.exp(g_last) + _dot_tn(k * jnp.exp(g_last - gc_rows), v_new)

    @pl.when(t == pl.num_programs(1) - 1)
    def _():
        sfin_ref[0] = s_ref[...]


def _dn_scan(qkv, gb, s0):
    b, l, w = qkv.shape
    n = l // DN_CHUNK
    fwd = lambda bi, t: (bi, t, 0)
    bwd = lambda bi, t: (bi, n - 1 - t, 0)
    state = pl.BlockSpec((1, 2, DN_HEADS, DN_DK, DN_DK), lambda bi, t: (bi, 0, 0, 0, 0))
    return pl.pallas_call(
        _dn_scan_kernel,
        grid=(b, n),
        in_specs=[pl.BlockSpec((1, DN_CHUNK, w), fwd), pl.BlockSpec((1, DN_CHUNK, w), bwd),
                  pl.BlockSpec((1, DN_CHUNK, LANES), fwd), pl.BlockSpec((1, DN_CHUNK, LANES), bwd),
                  state],
        out_specs=[pl.BlockSpec((1, DN_CHUNK, DN_W), fwd), pl.BlockSpec((1, DN_CHUNK, DN_W), bwd),
                   state],
        out_shape=[jax.ShapeDtypeStruct((b, l, DN_W), F32), jax.ShapeDtypeStruct((b, l, DN_W), F32),
                   jax.ShapeDtypeStruct((b, 2, DN_HEADS, DN_DK, DN_DK), F32)],
        scratch_shapes=[pltpu.VMEM((2, DN_HEADS, DN_DK, DN_DK), F32)],
        compiler_params=_params("parallel", "arbitrary"),
        name="dn_scan",
    )(qkv, qkv, gb, gb, s0)


def _out0_kernel(x_ref, na_ref, of_ref, ob_ref, z_ref, gate_ref, ng_ref, w_ref, o_ref):
    o = of_ref[0] + ob_ref[0]
    segs = [na_ref[0]]
    for h in range(DN_HEADS):
        seg = o[:, h * DN_DK:(h + 1) * DN_DK]
        ms = jnp.mean(seg * seg, axis=-1, keepdims=True)
        segs.append(seg * lax.rsqrt(ms + EPS) * ng_ref[...])
    mix = jnp.concatenate(segs, axis=1) * _silu(z_ref[0])
    o_ref[0] = x_ref[0] + gate_ref[0] * _dot(mix, w_ref[...])


def _out0(x, na, o_f, o_b, z, gate, norm_g, w_out, tm):
    b, l, d = x.shape
    tile = lambda n: pl.BlockSpec((1, tm, n), lambda bi, i: (bi, i, 0))
    return pl.pallas_call(
        _out0_kernel,
        grid=(b, l // tm),
        in_specs=[tile(d), tile(NA_W), tile(DN_W), tile(DN_W), tile(d),
                  pl.BlockSpec((1, 1, d), lambda bi, i: (bi, 0, 0)),
                  pl.BlockSpec((1, DN_DK), lambda bi, i: (0, 0)),
                  pl.BlockSpec(w_out.shape, lambda bi, i: (0, 0))],
        out_specs=tile(d),
        out_shape=jax.ShapeDtypeStruct((b, l, d), F32),
        compiler_params=_params("parallel", "parallel"),
        name="out_proj0",
    )(x, na, o_f, o_b, z, gate, norm_g.reshape(1, DN_DK), w_out.astype(BF16))


def _conv3_kernel(x_ref, xp_ref, xn_ref, cw_ref, v_ref, x1_ref, x2_ref):
    u = _dwconv_tile(x_ref, xp_ref, xn_ref, cw_ref, 3)
    v_ref[0] = u[:, :HY_W].astype(v_ref.dtype)
    x1_ref[0] = u[:, HY_W:2 * HY_W]
    x2_ref[0] = u[:, 2 * HY_W:]


def _conv3(p_u, conv_w, tl):
    b, l, w = p_u.shape
    out = lambda: pl.BlockSpec((1, tl, HY_W), lambda bi, i: (bi, i, 0))
    return pl.pallas_call(
        _conv3_kernel,
        grid=(b, l // tl),
        in_specs=_halo_specs(tl, w, l) + [pl.BlockSpec((3, w), lambda bi, i: (0, 0))],
        out_specs=[out(), out(), out()],
        out_shape=[jax.ShapeDtypeStruct((b, l, HY_W), BF16), jax.ShapeDtypeStruct((b, l, HY_W), F32),
                   jax.ShapeDtypeStruct((b, l, HY_W), F32)],
        compiler_params=_params("parallel", "parallel"),
        name="hyena_conv3",
    )(p_u, p_u, p_u, conv_w)


def _filter_kernel(feat_ref, env_ref, w1_ref, b1_ref, f1_ref, w2_ref, b2_ref, f2_ref, w3_ref,
                   h_ref, ss_ref):
    hid = jnp.sin(f1_ref[...] * (jnp.dot(feat_ref[...], w1_ref[...], precision=HI,
                                         preferred_element_type=F32) + b1_ref[...]))
    hid = jnp.sin(f2_ref[...] * (jnp.dot(hid, w2_ref[...], precision=HI,
                                         preferred_element_type=F32) + b2_ref[...]))
    h = jnp.dot(hid, w3_ref[...], precision=HI, preferred_element_type=F32)
    h = h * jnp.concatenate([env_ref[...]] * 4, axis=1)
    h_ref[...] = h

    @pl.when(pl.program_id(0) == 0)
    def _():
        ss_ref[...] = jnp.zeros_like(ss_ref)

    ss_ref[...] += jnp.sum(h * h, axis=0, keepdims=True)


def _hyena_filters_raw(length, w1, b1, f1, w2, b2, f2, w3, tl):
    t = jnp.linspace(0.0, 1.0, length, dtype=F32)[:, None]
    wv = 2.0 * math.pi * jnp.arange(length, dtype=F32)[:, None] / length
    f = jnp.linspace(1e-4, HY_BANDS - 1, HY_BANDS, dtype=F32)[None, :]
    feats = jnp.concatenate([t, jnp.cos(f * wv), -jnp.sin(f * wv)], axis=-1)
    decay = jnp.abs(jnp.linspace(HY_DECAY_MIN, HY_DECAY_MAX, HY_W, dtype=F32))
    env = jnp.exp(-t * decay)
    ffn = w1.shape[1]
    pe, pf = LANES - HY_EMB, LANES - ffn
    feats = jnp.pad(feats, ((0, 0), (0, pe)))
    w1p = jnp.pad(w1, ((0, pe), (0, pf)))
    w2p = jnp.pad(w2, ((0, pf), (0, pf)))
    w3p = jnp.pad(w3, ((0, pf), (0, 0)))
    vec = lambda a: jnp.pad(a.reshape(1, ffn), ((0, 0), (0, pf)))
    n_out = w3.shape[1]
    const = lambda shape: pl.BlockSpec(shape, lambda i: (0, 0))
    return pl.pallas_call(
        _filter_kernel,
        grid=(length // tl,),
        in_specs=[pl.BlockSpec((tl, LANES), lambda i: (i, 0)), pl.BlockSpec((tl, HY_W), lambda i: (i, 0)),
                  const((LANES, LANES)), const((1, LANES)), const((1, LANES)),
                  const((LANES, LANES)), const((1, LANES)), const((1, LANES)), const((LANES, n_out))],
        out_specs=[pl.BlockSpec((tl, n_out), lambda i: (i, 0)), const((1, n_out))],
        out_shape=[jax.ShapeDtypeStruct((length, n_out), F32), jax.ShapeDtypeStruct((1, n_out), F32)],
        compiler_params=_params("arbitrary"),
        name="hyena_filter_ffn",
    )(feats, env, w1p, vec(b1), vec(f1), w2p, vec(b2), vec(f2), w3p)


def _filter_mix_kernel(h_ref, ss_ref, kf_ref, km_ref, ny_ref):
    i = pl.program_id(0)
    h = h_ref[...] * lax.rsqrt(ss_ref[...] + EPS)
    tl = h.shape[0]
    kfs, kms = [], []
    for o in range(2):
        hf = h[:, (2 * o) * HY_W:(2 * o + 1) * HY_W]
        hb = h[:, (2 * o + 1) * HY_W:(2 * o + 2) * HY_W]
        kfs.append(hf + hb)
        kms.append(hf - hb)
    kf = jnp.concatenate(kfs, axis=1)
    kf_ref[...] = kf.astype(kf_ref.dtype)
    km_ref[...] = jnp.concatenate(kms, axis=1).astype(km_ref.dtype)
    rowi = lax.broadcasted_iota(jnp.int32, (tl, 1), 0) + i * tl
    sign = jnp.where(rowi % 2 == 0, 1.0, -1.0)

    @pl.when(i == 0)
    def _():
        ny_ref[...] = jnp.zeros_like(ny_ref)

    ny_ref[...] += jnp.sum(kf * sign, axis=0, keepdims=True)


def _filter_mix(h_raw, ss, tl):
    length, n = h_raw.shape
    half = n // 2
    return pl.pallas_call(
        _filter_mix_kernel,
        grid=(length // tl,),
        in_specs=[pl.BlockSpec((tl, n), lambda i: (i, 0)), pl.BlockSpec((1, n), lambda i: (0, 0))],
        out_specs=[pl.BlockSpec((tl, half), lambda i: (i, 0)), pl.BlockSpec((tl, half), lambda i: (i, 0)),
                   pl.BlockSpec((1, half), lambda i: (0, 0))],
        out_shape=[jax.ShapeDtypeStruct((length, half), BF16), jax.ShapeDtypeStruct((length, half), BF16),
                   jax.ShapeDtypeStruct((1, half), F32)],
        compiler_params=_params("arbitrary"),
        name="hyena_filter_mix",
    )(h_raw, ss)


def _dft_matrices(length):
    n = 2 * length
    idx = jnp.arange(length, dtype=jnp.int32)
    m = (idx[:, None] * idx[None, :]) % n
    ang = m.astype(F32) * (2.0 * math.pi / n)
    return jnp.cos(ang).astype(BF16), (-jnp.sin(ang)).astype(BF16)


def _filter_spectrum_kernel(cm_ref, sm_ref, kf_ref, km_ref, ny_ref, skip_ref, hr_ref, hi_ref, *, n):
    i = pl.program_id(1)
    hr = jnp.dot(cm_ref[...], kf_ref[...], preferred_element_type=F32) + skip_ref[...]
    hi = jnp.dot(sm_ref[...], km_ref[...], preferred_element_type=F32)
    rowi = lax.broadcasted_iota(jnp.int32, (hr.shape[0], 1), 0) + i * hr.shape[0]
    first = rowi == 0
    hr_ref[...] = hr * jnp.where(first, 1.0 / n, 2.0 / n)
    hi_ref[...] = jnp.where(first, (ny_ref[...] + skip_ref[...]) * (1.0 / n), hi * (2.0 / n))


def _filter_spectrum(cm, sm, kf, km, ny, skip, tm, tn):
    length, cols = kf.shape
    mat = pl.BlockSpec((tm, length), lambda j, i: (i, 0))
    rhs = pl.BlockSpec((length, tn), lambda j, i: (0, j))
    vec = pl.BlockSpec((1, tn), lambda j, i: (0, j))
    out = pl.BlockSpec((tm, tn), lambda j, i: (i, j))
    return pl.pallas_call(
        functools.partial(_filter_spectrum_kernel, n=2 * length),
        grid=(cols // tn, length // tm),
        in_specs=[mat, mat, rhs, rhs, vec, vec],
        out_specs=[out, out],
        out_shape=[jax.ShapeDtypeStruct((length, cols), F32)] * 2,
        compiler_params=_params("parallel", "arbitrary"),
        name="hyena_filter_spectrum",
    )(cm, sm, kf, km, ny, skip.reshape(1, cols))


def _dft_fwd_kernel(cm_ref, sm_ref, u_ref, hr_ref, hi_ref, yr_ref, yi_ref):
    i = pl.program_id(2)
    u = u_ref[0]
    xr = jnp.dot(cm_ref[...], u, preferred_element_type=F32)
    xi = jnp.dot(sm_ref[...], u, preferred_element_type=F32)
    hr = hr_ref[...]
    hi = hi_ref[...]
    yr_ref[0] = (xr * hr - xi * hi).astype(yr_ref.dtype)
    yi_ref[0] = (xr * hi + xi * hr).astype(yi_ref.dtype)

    @pl.when(i == 0)
    def _():
        rowi = lax.broadcasted_iota(jnp.int32, (u.shape[0], 1), 0)
        sign = jnp.where(rowi % 2 == 0, 1.0, -1.0)
        nyq = jnp.sum(u.astype(F32) * sign, axis=0, keepdims=True)
        yi_ref[0, 0:1, :] = (nyq * hi[0:1, :]).astype(yi_ref.dtype)


def _dft_inv_kernel(cm_ref, sm_ref, yr_ref, yi_ref, gate_ref, o_ref):
    i = pl.program_id(2)
    yi = yi_ref[0]
    y = jnp.dot(cm_ref[...], yr_ref[0], preferred_element_type=F32)
    y = y + jnp.dot(sm_ref[...], yi, preferred_element_type=F32)
    tm = y.shape[0]
    rowi = lax.broadcasted_iota(jnp.int32, (tm, 1), 0) + i * tm
    sign = jnp.where(rowi % 2 == 0, 1.0, -1.0)
    y = y + sign * yi[0:1, :].astype(F32)
    o_ref[0] = (gate_ref[0] * y).astype(o_ref.dtype)


def _long_conv_gated(u, gate, cm, sm, hr, hi, order, out_dtype, tm, cb):
    b, length, c = u.shape
    nc = c // cb
    grid = (b, nc, length // tm)
    mat = pl.BlockSpec((tm, length), lambda bi, j, i: (i, 0))
    full = pl.BlockSpec((1, length, cb), lambda bi, j, i: (bi, 0, j))
    spec = pl.BlockSpec((tm, cb), lambda bi, j, i: (i, order * nc + j))
    tile = pl.BlockSpec((1, tm, cb), lambda bi, j, i: (bi, i, j))
    yr, yi = pl.pallas_call(
        _dft_fwd_kernel,
        grid=grid,
        in_specs=[mat, mat, full, spec, spec],
        out_specs=[tile, tile],
        out_shape=[jax.ShapeDtypeStruct((b, length, c), BF16)] * 2,
        compiler_params=_params("parallel", "parallel", "arbitrary"),
        name="hyena_dft_fwd",
    )(cm, sm, u, hr, hi)
    return pl.pallas_call(
        _dft_inv_kernel,
        grid=grid,
        in_specs=[mat, mat, full, full, tile],
        out_specs=tile,
        out_shape=jax.ShapeDtypeStruct((b, length, c), out_dtype),
        compiler_params=_params("parallel", "parallel", "arbitrary"),
        name="hyena_dft_inv",
    )(cm, sm, yr, yi, gate)


def _out1_kernel(x_ref, y_ref, z_ref, gate_ref, w_ref, fg_ref, o_ref):
    r = x_ref[0] + gate_ref[0] * _dot(y_ref[0] * _silu(z_ref[0]), w_ref[...])
    ms = jnp.mean(r * r, axis=-1, keepdims=True)
    o_ref[0] = r * lax.rsqrt(ms + EPS) * fg_ref[...]


def _out1(x, y, z, gate, w_out, final_g, tm):
    b, l, d = x.shape
    tile = pl.BlockSpec((1, tm, d), lambda bi, i: (bi, i, 0))
    return pl.pallas_call(
        _out1_kernel,
        grid=(b, l // tm),
        in_specs=[tile, tile, tile, pl.BlockSpec((1, 1, d), lambda bi, i: (bi, 0, 0)),
                  pl.BlockSpec(w_out.shape, lambda bi, i: (0, 0)), pl.BlockSpec((1, d), lambda bi, i: (0, 0))],
        out_specs=tile,
        out_shape=jax.ShapeDtypeStruct((b, l, d), F32),
        compiler_params=_params("parallel", "parallel"),
        name="out_proj1",
    )(x, y, z, gate, w_out.astype(BF16), final_g.reshape(1, d))


def _row_tile(l, want):
    return want if l % want == 0 else l


def _even_layer(x, ctx, c, c_ctx, norm_g, mod_w, mod_b, w_in, rpb, dn_conv, a_log, dt_bias, dn_norm_g, w_out):
    b, l, d = x.shape
    lc = ctx.shape[1]
    off_dn = 3 * NA_W
    off_ab = off_dn + DN_CONV_W
    off_z = off_ab + 4 * DN_HEADS
    rows = b + 1
    pad = (-rows) % 8
    cvecs = jnp.concatenate([c, c_ctx[None, :], jnp.zeros((pad, d), F32)], axis=0)
    m = _modulation(cvecs, mod_w, mod_b)
    shift, scale, gate = (m[:, i * d:(i + 1) * d] for i in range(3))
    lat = lambda a: a[:b, None, :]
    cx = lambda a: jnp.broadcast_to(a[b:b + 1, None, :], (b, 1, d))

    w_na = w_in[:, :off_dn]
    w_dn = w_in[:, off_dn:off_ab]
    w_ab = jnp.pad(w_in[:, off_ab:off_z], ((0, 0), (0, LANES - 4 * DN_HEADS)))
    w_z = w_in[:, off_z:]
    qkv_x, dnx, abx, z_x = _norm_proj(
        x, norm_g, lat(shift), lat(scale),
        [(w_na, "bf16", BF16), (w_dn, "bf16", F32), (w_ab, "f32", F32), (w_z, "bf16", F32)],
        _row_tile(l, 512))
    qkv_c, dnc, abc = _norm_proj(
        ctx, norm_g, cx(shift), cx(scale),
        [(w_na, "bf16", BF16), (w_dn, "bf16", F32), (w_ab, "f32", F32)], _row_tile(lc, 256))

    na_x = _neighbourhood_attention(qkv_x, qkv_c, rpb)

    dn_c, gb_c = _dn_prepare(dnc, abc, dn_conv, a_log, dt_bias, _row_tile(lc, 256))
    dn_x, gb_x = _dn_prepare(dnx, abx, dn_conv, a_log, dt_bias, _row_tile(l, 512))
    s0 = jnp.zeros((b, 2, DN_HEADS, DN_DK, DN_DK), F32)
    _, _, s_ctx = _dn_scan(dn_c, gb_c, s0)
    o_f, o_b, _ = _dn_scan(dn_x, gb_x, s_ctx)

    return _out0(x, na_x, o_f, o_b, z_x, lat(gate), dn_norm_g, w_out, _row_tile(l, 512))


def _hyena_layer(x, c, norm_g, mod_w, mod_b, w_in, conv_w, fw1, fb1, ff1, fw2, fb2, ff2, fw3, skip, w_out,
                 final_g):
    b, l, d = x.shape
    pad = (-b) % 8
    cvecs = jnp.concatenate([c, jnp.zeros((pad, d), F32)], axis=0)
    m = _modulation(cvecs, mod_w, mod_b)
    shift, scale, gate = (m[:b, None, i * d:(i + 1) * d] for i in range(3))
    p_u, gz = _norm_proj(x, norm_g, shift, scale,
                         [(w_in[:, :3 * HY_W], "bf16", F32), (w_in[:, 3 * HY_W:], "bf16", F32)],
                         _row_tile(l, 512))
    v, x1, x2 = _conv3(p_u, conv_w, _row_tile(l, 512))

    h_raw, ss = _hyena_filters_raw(l, fw1, fb1, ff1, fw2, fb2, ff2, fw3, _row_tile(l, 256))
    kf, km, ny = _filter_mix(h_raw, ss, _row_tile(l, 256))
    cm, sm = _dft_matrices(l)
    tm = _row_tile(l, 512)
    hr, hi = _filter_spectrum(cm, sm, kf, km, ny, skip, tm, 512)

    z = _long_conv_gated(v, x1, cm, sm, hr, hi, 0, BF16, tm, 512)
    y = _long_conv_gated(z, x2, cm, sm, hr, hi, 1, F32, tm, 512)
    return _out1(x, y, gz, gate, w_out, final_g, _row_tile(l, 512))


def kernel(x, c, ctx, c_ctx, e_norm_g, e_mod_w, e_mod_b, e_w_in, e_na_rpb, e_dn_conv, e_dn_a_log, e_dn_dt_bias, e_dn_norm_g, e_w_out, o_norm_g, o_mod_w, o_mod_b, o_w_in, o_hy_conv, o_ffn_w1, o_ffn_b1, o_ffn_f1, o_ffn_w2, o_ffn_b2, o_ffn_f2, o_ffn_w3, o_hy_skip, o_w_out, final_norm_g):
    x = _even_layer(x, ctx, c, c_ctx, e_norm_g[0], e_mod_w[0], e_mod_b[0], e_w_in[0], e_na_rpb[0],
                    e_dn_conv[0], e_dn_a_log[0], e_dn_dt_bias[0], e_dn_norm_g[0], e_w_out[0])
    return _hyena_layer(x, c, o_norm_g[0], o_mod_w[0], o_mod_b[0], o_w_in[0], o_hy_conv[0],
                        o_ffn_w1[0], o_ffn_b1[0], o_ffn_f1[0], o_ffn_w2[0], o_ffn_b2[0], o_ffn_f2[0],
                        o_ffn_w3[0], o_hy_skip[0], o_w_out[0], final_norm_g)
```

```python
import functools
import math

import numpy as np
import jax
import jax.numpy as jnp
from jax import lax
from jax.experimental import pallas as pl
from jax.experimental.pallas import tpu as pltpu

F32 = jnp.float32
BF16 = jnp.bfloat16
HI = lax.Precision.HIGHEST
EPS = 1e-6
NEG = -1e30

LANES = 128
VMEM_LIMIT_BYTES = 56 * 1024 * 1024

GRID_W = 64
NA_HEADS = 8
NA_DH = 64
NA_W = NA_HEADS * NA_DH
NA_WIN_R = 8
NA_WIN_C = 16
NA_QROWS = 4
NA_KROWS = NA_QROWS + NA_WIN_R

DN_HEADS = 4
DN_DK = 128
DN_W = DN_HEADS * DN_DK
DN_CONV_W = 3 * DN_W
DN_CONV = 5
DN_CHUNK = 128

HY_W = 1024
HY_EMB = 33
HY_BANDS = (HY_EMB - 1) // 2
HY_DECAY_MIN = math.log(1e-2) / 1.5
HY_DECAY_MAX = math.log(1e-2) / 0.3


def _params(*sem):
    return pltpu.CompilerParams(dimension_semantics=sem, vmem_limit_bytes=VMEM_LIMIT_BYTES)


def _silu(v):
    return v * jax.nn.sigmoid(v)


def _dot(a, b):
    return jnp.dot(a.astype(BF16), b.astype(BF16), preferred_element_type=F32)


def _dot_nt(a, b):
    return lax.dot_general(a.astype(BF16), b.astype(BF16), (((1,), (1,)), ((), ())),
                           preferred_element_type=F32)


def _dot_tn(a, b):
    return lax.dot_general(a.astype(BF16), b.astype(BF16), (((0,), (0,)), ((), ())),
                           preferred_element_type=F32)


def _mod_kernel(c_ref, w_ref, b_ref, o_ref):
    o_ref[...] = jnp.dot(_silu(c_ref[...]), w_ref[...], precision=HI,
                         preferred_element_type=F32) + b_ref[...]


def _modulation(cvecs, w, b):
    r, d = cvecs.shape
    n = w.shape[1]
    tn = 512
    return pl.pallas_call(
        _mod_kernel,
        grid=(n // tn,),
        in_specs=[pl.BlockSpec((r, d), lambda j: (0, 0)),
                  pl.BlockSpec((d, tn), lambda j: (0, j)),
                  pl.BlockSpec((1, tn), lambda j: (0, j))],
        out_specs=pl.BlockSpec((r, tn), lambda j: (0, j)),
        out_shape=jax.ShapeDtypeStruct((r, n), F32),
        compiler_params=_params("parallel"),
        name="adaln_mod",
    )(cvecs, w, b.reshape(1, n))


def _proj_kernel(x_ref, g_ref, sh_ref, sc_ref, *refs, precs):
    n = len(precs)
    w_refs, o_refs = refs[:n], refs[n:]
    x = x_ref[0]
    ms = jnp.mean(x * x, axis=-1, keepdims=True)
    h = x * lax.rsqrt(ms + EPS) * g_ref[...] * (1.0 + sc_ref[0]) + sh_ref[0]
    hb = h.astype(BF16)
    for w_ref, o_ref, prec in zip(w_refs, o_refs, precs):
        if prec == "bf16":
            r = jnp.dot(hb, w_ref[...], preferred_element_type=F32)
        else:
            r = jnp.dot(h, w_ref[...], precision=HI, preferred_element_type=F32)
        o_ref[0] = r.astype(o_ref.dtype)


def _norm_proj(x, norm_g, shift, scale, sections, tm):
    b, l, d = x.shape
    ws, precs, out_shapes, out_specs, w_specs = [], [], [], [], []
    for w, prec, odt in sections:
        ws.append(w.astype(BF16) if prec == "bf16" else w)
        precs.append(prec)
        n = w.shape[1]
        w_specs.append(pl.BlockSpec((d, n), lambda bi, i: (0, 0)))
        out_specs.append(pl.BlockSpec((1, tm, n), lambda bi, i: (bi, i, 0)))
        out_shapes.append(jax.ShapeDtypeStruct((b, l, n), odt))
    vec = pl.BlockSpec((1, 1, d), lambda bi, i: (bi, 0, 0))
    return pl.pallas_call(
        functools.partial(_proj_kernel, precs=tuple(precs)),
        grid=(b, l // tm),
        in_specs=[pl.BlockSpec((1, tm, d), lambda bi, i: (bi, i, 0)),
                  pl.BlockSpec((1, d), lambda bi, i: (0, 0)), vec, vec] + w_specs,
        out_specs=out_specs,
        out_shape=out_shapes,
        compiler_params=_params("parallel", "parallel"),
        name="norm_proj",
    )(x, norm_g.reshape(1, d), shift, scale, *ws)


def _na_block_geometry(rows):
    return ((0, 0), (NA_QROWS, 0), (rows - NA_QROWS, rows - NA_KROWS))


def _na_bias_kernel(rc_ref, o_ref, *, rows):
    wr = min(NA_WIN_R, rows)
    qc = lax.broadcasted_iota(jnp.int32, (GRID_W, LANES), 0)
    lane = lax.broadcasted_iota(jnp.int32, (GRID_W, LANES), 1)
    kc = lane & (GRID_W - 1)
    c0 = jnp.clip(qc - NA_WIN_C // 2, 0, GRID_W - NA_WIN_C)
    col_ok = (kc >= c0) & (kc < c0 + NA_WIN_C)
    neg = jnp.full((GRID_W, LANES), NEG, F32)
    tiles = []
    for dr in range(2 * NA_WIN_R - 1):
        base = jnp.broadcast_to(rc_ref[0, dr:dr + 1, :], (GRID_W, LANES))
        tiles.append(jnp.where(col_ok, pltpu.roll(base, 0, 1, stride=1, stride_axis=0), neg))
    for g, (r_first, k_first) in enumerate(_na_block_geometry(rows)):
        for i in range(NA_QROWS):
            qr = r_first + i
            r0 = min(max(qr - wr // 2, 0), rows - wr)
            for jp in range(NA_KROWS // 2):
                halves = []
                for j in (2 * jp, 2 * jp + 1):
                    kr = k_first + j
                    halves.append(tiles[kr - qr + NA_WIN_R - 1] if r0 <= kr < r0 + wr else neg)
                o_ref[g, 0, i * GRID_W:(i + 1) * GRID_W, jp * LANES:(jp + 1) * LANES] = jnp.where(
                    lane < GRID_W, halves[0], halves[1])


def _na_bias_table(rpb, rows):
    wc = NA_WIN_C
    fill = jnp.full(rpb.shape[:2] + (GRID_W - (2 * wc - 1),), NEG, F32)
    ring = jnp.concatenate([rpb[..., wc - 1:], fill, rpb[..., :wc - 1]], axis=-1)
    ring = jnp.concatenate([ring, ring], axis=-1)
    nq, nk = NA_QROWS * GRID_W, NA_KROWS * GRID_W
    return pl.pallas_call(
        functools.partial(_na_bias_kernel, rows=rows),
        grid=(NA_HEADS,),
        in_specs=[pl.BlockSpec((1, 2 * NA_WIN_R - 1, LANES), lambda h: (h, 0, 0))],
        out_specs=pl.BlockSpec((3, 1, nq, nk), lambda h: (0, h, 0, 0)),
        out_shape=jax.ShapeDtypeStruct((3, NA_HEADS, nq, nk), F32),
        compiler_params=_params("parallel"),
        name="na_bias_table",
    )(ring)


def _na_kernel(q_ref, k_ref, v_ref, kc_ref, vc_ref, bias_ref, o_ref, *, rows):
    blk = pl.program_id(1)
    nq = NA_QROWS * GRID_W
    nk = NA_KROWS * GRID_W
    k_first = jnp.clip(blk * NA_QROWS - NA_WIN_R // 2, 0, rows - NA_KROWS)
    start = pl.multiple_of(k_first * GRID_W, GRID_W)
    lane = lax.broadcasted_iota(jnp.int32, (nq, LANES), 1)
    low = lane < NA_DH
    for p in range(NA_W // LANES):
        cs = slice(p * LANES, (p + 1) * LANES)
        q2 = q_ref[0, :, cs]
        k2 = k_ref[0, pl.ds(start, nk), cs]
        v2 = v_ref[0, pl.ds(start, nk), cs]
        kc2 = kc_ref[0, :, cs]
        vc2 = vc_ref[0, :, cs]
        halves = []
        for hh in range(2):
            sel = low if hh == 0 else jnp.logical_not(low)
            qh = jnp.where(sel, q2, jnp.zeros_like(q2))
            s_win = _dot_nt(qh, k2) * (NA_DH ** -0.5) + bias_ref[0, 2 * p + hh]
            s_ctx = _dot_nt(qh, kc2) * (NA_DH ** -0.5)
            m = jnp.maximum(jnp.max(s_win, axis=-1, keepdims=True),
                            jnp.max(s_ctx, axis=-1, keepdims=True))
            p_win = jnp.exp(s_win - m)
            p_ctx = jnp.exp(s_ctx - m)
            den = jnp.sum(p_win, axis=-1, keepdims=True) + jnp.sum(p_ctx, axis=-1, keepdims=True)
            halves.append((_dot(p_win, v2) + _dot(p_ctx, vc2)) / den)
        o_ref[0, :, cs] = jnp.where(low, halves[0], halves[1])


def _neighbourhood_attention(qkv, qkv_c, rpb):
    b, l, _ = qkv.shape
    lc = qkv_c.shape[1]
    rows = l // GRID_W
    assert rows % NA_QROWS == 0 and rows >= NA_KROWS + 1
    nq = NA_QROWS * GRID_W
    nblk = rows // NA_QROWS
    bias = _na_bias_table(rpb, rows)

    def cfg(bi, i):
        return (jnp.where(i == 0, 0, jnp.where(i == nblk - 1, 2, 1)), 0, 0, 0)

    return pl.pallas_call(
        functools.partial(_na_kernel, rows=rows),
        grid=(b, nblk),
        in_specs=[pl.BlockSpec((1, nq, NA_W), lambda bi, i: (bi, i, 0)),
                  pl.BlockSpec((1, l, NA_W), lambda bi, i: (bi, 0, 1)),
                  pl.BlockSpec((1, l, NA_W), lambda bi, i: (bi, 0, 2)),
                  pl.BlockSpec((1, lc, NA_W), lambda bi, i: (bi, 0, 1)),
                  pl.BlockSpec((1, lc, NA_W), lambda bi, i: (bi, 0, 2)),
                  pl.BlockSpec((1, NA_HEADS, nq, NA_KROWS * GRID_W), cfg)],
        out_specs=pl.BlockSpec((1, nq, NA_W), lambda bi, i: (bi, i, 0)),
        out_shape=jax.ShapeDtypeStruct((b, l, NA_W), F32),
        compiler_params=_params("parallel", "arbitrary"),
        name="neighbourhood_attention",
    )(qkv, qkv, qkv, qkv_c, qkv_c, bias)


def _halo_specs(tl, width, l):
    nb8 = tl // 8
    last8 = l // 8 - 1
    return [pl.BlockSpec((1, tl, width), lambda bi, i: (bi, i, 0)),
            pl.BlockSpec((1, 8, width), lambda bi, i: (bi, jnp.maximum(i * nb8 - 1, 0), 0)),
            pl.BlockSpec((1, 8, width), lambda bi, i: (bi, jnp.minimum((i + 1) * nb8, last8), 0))]


def _dwconv_tile(x_ref, xp_ref, xn_ref, cw_ref, taps):
    i = pl.program_id(1)
    x = x_ref[0]
    tl = x.shape[0]
    prev = jnp.where(i > 0, xp_ref[0], 0.0)
    nxt = jnp.where(i < pl.num_programs(1) - 1, xn_ref[0], 0.0)
    xe = jnp.concatenate([prev, x, nxt], axis=0)
    half = taps // 2
    acc = jnp.zeros_like(x)
    for j in range(taps):
        off = 8 - half + j
        acc = acc + xe[off:off + tl] * cw_ref[j:j + 1, :]
    return acc


def _dnprep_kernel(x_ref, xp_ref, xn_ref, ab_ref, cw_ref, al_ref, dtb_ref, qkv_ref, gb_ref):
    t = _silu(_dwconv_tile(x_ref, xp_ref, xn_ref, cw_ref, DN_CONV))
    segs = []
    for hh in range(3 * DN_HEADS):
        seg = t[:, hh * DN_DK:(hh + 1) * DN_DK]
        if hh < 2 * DN_HEADS:
            seg = seg * lax.rsqrt(jnp.sum(seg * seg, axis=-1, keepdims=True) + EPS)
        if hh < DN_HEADS:
            seg = seg * (DN_DK ** -0.5)
        segs.append(seg)
    qkv_ref[0] = jnp.concatenate(segs, axis=1)
    ab = ab_ref[0]
    z = ab + dtb_ref[...]
    softplus = jnp.maximum(z, 0.0) + jnp.log(1.0 + jnp.exp(-jnp.abs(z)))
    g = -jnp.exp(al_ref[...]) * softplus
    lane = lax.broadcasted_iota(jnp.int32, ab.shape, 1)
    gb_ref[0] = jnp.where(lane < 2 * DN_HEADS, g, jax.nn.sigmoid(ab))


def _dn_prepare(qkv, ab, conv_w, a_log, dt_bias, tl):
    b, l, w = qkv.shape
    pad = LANES - 2 * DN_HEADS
    al = jnp.pad(a_log.reshape(1, 2 * DN_HEADS), ((0, 0), (0, pad)))
    dtb = jnp.pad(dt_bias.reshape(1, 2 * DN_HEADS), ((0, 0), (0, pad)))
    const = lambda shape: pl.BlockSpec(shape, lambda bi, i: (0, 0))
    return pl.pallas_call(
        _dnprep_kernel,
        grid=(b, l // tl),
        in_specs=_halo_specs(tl, w, l) + [pl.BlockSpec((1, tl, LANES), lambda bi, i: (bi, i, 0)),
                                         const((DN_CONV, w)), const((1, LANES)), const((1, LANES))],
        out_specs=[pl.BlockSpec((1, tl, w), lambda bi, i: (bi, i, 0)),
                   pl.BlockSpec((1, tl, LANES), lambda bi, i: (bi, i, 0))],
        out_shape=[jax.ShapeDtypeStruct((b, l, w), F32), jax.ShapeDtypeStruct((b, l, LANES), F32)],
        compiler_params=_params("parallel", "parallel"),
        name="dn_prepare",
    )(qkv, qkv, qkv, ab, conv_w, al, dtb)


DN_INV_BASE = 16


def _unit_triangular_inverse(nil, eye, row, col):
    c = nil.shape[-1]
    sh = int(math.log2(DN_INV_BASE))
    diag = jnp.where((row >> sh) == (col >> sh), nil, 0.0)
    inv = eye - diag
    pw = diag
    for _ in range(sh - 1):
        pw = _bdot(pw, pw)
        inv = inv + _bdot(inv, pw)
    while (1 << sh) < c:
        off = jnp.where(((row >> (sh + 1)) == (col >> (sh + 1))) & ((row >> sh) != (col >> sh)), nil, 0.0)
        inv = inv - _bdot(inv, _bdot(off, inv))
        sh += 1
    return inv


def _bdot(a, b):
    return lax.dot_general(a.astype(BF16), b.astype(BF16), (((2,), (1,)), ((0,), (0,))),
                           preferred_element_type=F32)


def _bdot_nt(a, b):
    return lax.dot_general(a.astype(BF16), b.astype(BF16), (((2,), (2,)), ((0,), (0,))),
                           preferred_element_type=F32)


def _bdot_tn(a, b):
    return lax.dot_general(a.astype(BF16), b.astype(BF16), (((1,), (1,)), ((0,), (0,))),
                           preferred_element_type=F32)


def _dn_scan_kernel(xf_ref, xb_ref, gf_ref, gbk_ref, s0_ref, of_ref, ob_ref, sfin_ref, s_ref):
    t = pl.program_id(1)
    c = DN_CHUNK
    nh = DN_HEADS

    @pl.when(t == 0)
    def _():
        s_ref[...] = s0_ref[0]

    row = lax.broadcasted_iota(jnp.int32, (c, c), 0)
    col = lax.broadcasted_iota(jnp.int32, (c, c), 1)
    eye = (row == col).astype(F32)
    incl = jnp.stack([row >= col] * nh + [row <= col] * nh)
    strict = jnp.stack([row > col] * nh + [row < col] * nh)
    qs, ks, vs, g_bs, betas = [], [], [], [], []
    for d, (x_ref, g_ref) in enumerate(((xf_ref, gf_ref), (xb_ref, gbk_ref))):
        gb = g_ref[0]
        for h in range(nh):
            qs.append(x_ref[0, :, h * DN_DK:(h + 1) * DN_DK])
            ks.append(x_ref[0, :, DN_W + h * DN_DK:DN_W + (h + 1) * DN_DK])
            vs.append(x_ref[0, :, 2 * DN_W + h * DN_DK:2 * DN_W + (h + 1) * DN_DK])
            ci = d * nh + h
            g_bs.append(jnp.broadcast_to(gb[:, ci:ci + 1], (c, c)))
            betas.append(gb[:, 2 * nh + ci:2 * nh + ci + 1])
    q, k, v, g_b, beta = (jnp.stack(a) for a in (qs, ks, vs, g_bs, betas))
    gc_rows = lax.dot_general(incl.astype(F32), g_b, (((2,), (1,)), ((0,), (0,))), precision=HI,
                              preferred_element_type=F32)
    gc_cols = jnp.swapaxes(gc_rows, 1, 2)
    decay = jnp.where(incl, jnp.exp(jnp.where(incl, gc_rows - gc_cols, 0.0)), 0.0)
    kb = k * beta
    vb = v * beta
    nil = jnp.where(strict, _bdot_nt(kb, k) * decay, 0.0)
    aqk = _bdot_nt(q, k) * decay
    egc = jnp.exp(gc_rows)
    inv = _unit_triangular_inverse(nil, eye, row, col)
    u = _bdot(inv, vb)
    w = _bdot(inv, kb * egc)
    s = s_ref[...]
    v_new = u - _bdot(w, s)
    o = _bdot(q * egc, s) + _bdot(aqk, v_new)
    g_last = jnp.concatenate([gc_rows[:nh, c - 1:c, :], gc_rows[nh:, 0:1, :]], axis=0)
    s_ref[...] = s * jnp.exp(g_last) + _bdot_tn(k * jnp.exp(g_last - gc_rows), v_new)
    for h in range(nh):
        of_ref[0, :, h * DN_DK:(h + 1) * DN_DK] = o[h]
        ob_ref[0, :, h * DN_DK:(h + 1) * DN_DK] = o[nh + h]

    @pl.when(t == pl.num_programs(1) - 1)
    def _():
        sfin_ref[0] = s_ref[...]


def _dn_scan(qkv, gb, s0):
    b, l, w = qkv.shape
    n = l // DN_CHUNK
    fwd = lambda bi, t: (bi, t, 0)
    bwd = lambda bi, t: (bi, n - 1 - t, 0)
    state = pl.BlockSpec((1, 2 * DN_HEADS, DN_DK, DN_DK), lambda bi, t: (bi, 0, 0, 0))
    return pl.pallas_call(
        _dn_scan_kernel,
        grid=(b, n),
        in_specs=[pl.BlockSpec((1, DN_CHUNK, w), fwd), pl.BlockSpec((1, DN_CHUNK, w), bwd),
                  pl.BlockSpec((1, DN_CHUNK, LANES), fwd), pl.BlockSpec((1, DN_CHUNK, LANES), bwd),
                  state],
        out_specs=[pl.BlockSpec((1, DN_CHUNK, DN_W), fwd), pl.BlockSpec((1, DN_CHUNK, DN_W), bwd),
                   state],
        out_shape=[jax.ShapeDtypeStruct((b, l, DN_W), F32), jax.ShapeDtypeStruct((b, l, DN_W), F32),
                   jax.ShapeDtypeStruct((b, 2 * DN_HEADS, DN_DK, DN_DK), F32)],
        scratch_shapes=[pltpu.VMEM((2 * DN_HEADS, DN_DK, DN_DK), F32)],
        compiler_params=_params("parallel", "arbitrary"),
        name="dn_scan",
    )(qkv, qkv, gb, gb, s0)


def _out0_kernel(x_ref, na_ref, of_ref, ob_ref, z_ref, gate_ref, ng_ref, w_ref, o_ref):
    o = of_ref[0] + ob_ref[0]
    segs = [na_ref[0]]
    for h in range(DN_HEADS):
        seg = o[:, h * DN_DK:(h + 1) * DN_DK]
        ms = jnp.mean(seg * seg, axis=-1, keepdims=True)
        segs.append(seg * lax.rsqrt(ms + EPS) * ng_ref[...])
    mix = jnp.concatenate(segs, axis=1) * _silu(z_ref[0])
    o_ref[0] = x_ref[0] + gate_ref[0] * _dot(mix, w_ref[...])


def _out0(x, na, o_f, o_b, z, gate, norm_g, w_out, tm):
    b, l, d = x.shape
    tile = lambda n: pl.BlockSpec((1, tm, n), lambda bi, i: (bi, i, 0))
    return pl.pallas_call(
        _out0_kernel,
        grid=(b, l // tm),
        in_specs=[tile(d), tile(NA_W), tile(DN_W), tile(DN_W), tile(d),
                  pl.BlockSpec((1, 1, d), lambda bi, i: (bi, 0, 0)),
                  pl.BlockSpec((1, DN_DK), lambda bi, i: (0, 0)),
                  pl.BlockSpec(w_out.shape, lambda bi, i: (0, 0))],
        out_specs=tile(d),
        out_shape=jax.ShapeDtypeStruct((b, l, d), F32),
        compiler_params=_params("parallel", "parallel"),
        name="out_proj0",
    )(x, na, o_f, o_b, z, gate, norm_g.reshape(1, DN_DK), w_out.astype(BF16))


def _conv3_kernel(x_ref, xp_ref, xn_ref, cw_ref, v_ref, x1_ref, x2_ref):
    u = _dwconv_tile(x_ref, xp_ref, xn_ref, cw_ref, 3)
    v_ref[0] = u[:, :HY_W].astype(v_ref.dtype)
    x1_ref[0] = u[:, HY_W:2 * HY_W]
    x2_ref[0] = u[:, 2 * HY_W:]


def _conv3(p_u, conv_w, tl):
    b, l, w = p_u.shape
    out = lambda: pl.BlockSpec((1, tl, HY_W), lambda bi, i: (bi, i, 0))
    return pl.pallas_call(
        _conv3_kernel,
        grid=(b, l // tl),
        in_specs=_halo_specs(tl, w, l) + [pl.BlockSpec((3, w), lambda bi, i: (0, 0))],
        out_specs=[out(), out(), out()],
        out_shape=[jax.ShapeDtypeStruct((b, l, HY_W), BF16), jax.ShapeDtypeStruct((b, l, HY_W), F32),
                   jax.ShapeDtypeStruct((b, l, HY_W), F32)],
        compiler_params=_params("parallel", "parallel"),
        name="hyena_conv3",
    )(p_u, p_u, p_u, conv_w)


def _filter_kernel(feat_ref, env_ref, w1_ref, b1_ref, f1_ref, w2_ref, b2_ref, f2_ref, w3_ref,
                   h_ref, ss_ref):
    hid = jnp.sin(f1_ref[...] * (jnp.dot(feat_ref[...], w1_ref[...], precision=HI,
                                         preferred_element_type=F32) + b1_ref[...]))
    hid = jnp.sin(f2_ref[...] * (jnp.dot(hid, w2_ref[...], precision=HI,
                                         preferred_element_type=F32) + b2_ref[...]))
    h = jnp.dot(hid, w3_ref[...], precision=HI, preferred_element_type=F32)
    h = h * jnp.concatenate([env_ref[...]] * 4, axis=1)
    h_ref[...] = h

    @pl.when(pl.program_id(0) == 0)
    def _():
        ss_ref[...] = jnp.zeros_like(ss_ref)

    ss_ref[...] += jnp.sum(h * h, axis=0, keepdims=True)


def _hyena_filters_raw(length, w1, b1, f1, w2, b2, f2, w3, tl):
    t = jnp.linspace(0.0, 1.0, length, dtype=F32)[:, None]
    wv = 2.0 * math.pi * jnp.arange(length, dtype=F32)[:, None] / length
    f = jnp.linspace(1e-4, HY_BANDS - 1, HY_BANDS, dtype=F32)[None, :]
    feats = jnp.concatenate([t, jnp.cos(f * wv), -jnp.sin(f * wv)], axis=-1)
    decay = jnp.abs(jnp.linspace(HY_DECAY_MIN, HY_DECAY_MAX, HY_W, dtype=F32))
    env = jnp.exp(-t * decay)
    ffn = w1.shape[1]
    pe, pf = LANES - HY_EMB, LANES - ffn
    feats = jnp.pad(feats, ((0, 0), (0, pe)))
    w1p = jnp.pad(w1, ((0, pe), (0, pf)))
    w2p = jnp.pad(w2, ((0, pf), (0, pf)))
    w3p = jnp.pad(w3, ((0, pf), (0, 0)))
    vec = lambda a: jnp.pad(a.reshape(1, ffn), ((0, 0), (0, pf)))
    n_out = w3.shape[1]
    const = lambda shape: pl.BlockSpec(shape, lambda i: (0, 0))
    return pl.pallas_call(
        _filter_kernel,
        grid=(length // tl,),
        in_specs=[pl.BlockSpec((tl, LANES), lambda i: (i, 0)), pl.BlockSpec((tl, HY_W), lambda i: (i, 0)),
                  const((LANES, LANES)), const((1, LANES)), const((1, LANES)),
                  const((LANES, LANES)), const((1, LANES)), const((1, LANES)), const((LANES, n_out))],
        out_specs=[pl.BlockSpec((tl, n_out), lambda i: (i, 0)), const((1, n_out))],
        out_shape=[jax.ShapeDtypeStruct((length, n_out), F32), jax.ShapeDtypeStruct((1, n_out), F32)],
        compiler_params=_params("arbitrary"),
        name="hyena_filter_ffn",
    )(feats, env, w1p, vec(b1), vec(f1), w2p, vec(b2), vec(f2), w3p)


def _filter_mix_kernel(h_ref, ss_ref, kf_ref, km_ref, ny_ref):
    i = pl.program_id(0)
    h = h_ref[...] * lax.rsqrt(ss_ref[...] + EPS)
    tl = h.shape[0]
    kfs, kms = [], []
    for o in range(2):
        hf = h[:, (2 * o) * HY_W:(2 * o + 1) * HY_W]
        hb = h[:, (2 * o + 1) * HY_W:(2 * o + 2) * HY_W]
        kfs.append(hf + hb)
        kms.append(hf - hb)
    kf = jnp.concatenate(kfs, axis=1)
    kf_ref[...] = kf.astype(kf_ref.dtype)
    km_ref[...] = jnp.concatenate(kms, axis=1).astype(km_ref.dtype)
    rowi = lax.broadcasted_iota(jnp.int32, (tl, 1), 0) + i * tl
    sign = jnp.where(rowi % 2 == 0, 1.0, -1.0)

    @pl.when(i == 0)
    def _():
        ny_ref[...] = jnp.zeros_like(ny_ref)

    ny_ref[...] += jnp.sum(kf * sign, axis=0, keepdims=True)


def _filter_mix(h_raw, ss, tl):
    length, n = h_raw.shape
    half = n // 2
    return pl.pallas_call(
        _filter_mix_kernel,
        grid=(length // tl,),
        in_specs=[pl.BlockSpec((tl, n), lambda i: (i, 0)), pl.BlockSpec((1, n), lambda i: (0, 0))],
        out_specs=[pl.BlockSpec((tl, half), lambda i: (i, 0)), pl.BlockSpec((tl, half), lambda i: (i, 0)),
                   pl.BlockSpec((1, half), lambda i: (0, 0))],
        out_shape=[jax.ShapeDtypeStruct((length, half), BF16), jax.ShapeDtypeStruct((length, half), BF16),
                   jax.ShapeDtypeStruct((1, half), F32)],
        compiler_params=_params("arbitrary"),
        name="hyena_filter_mix",
    )(h_raw, ss)


def _dft_matrices(length):
    n = 2 * length
    idx = jnp.arange(length, dtype=jnp.int32)
    m = (idx[:, None] * idx[None, :]) % n
    ang = m.astype(F32) * (2.0 * math.pi / n)
    return jnp.cos(ang).astype(BF16), (-jnp.sin(ang)).astype(BF16)


def _filter_spectrum_kernel(cm_ref, sm_ref, kf_ref, km_ref, ny_ref, skip_ref, hr_ref, hi_ref, *, n):
    i = pl.program_id(1)
    hr = jnp.dot(cm_ref[...], kf_ref[...], preferred_element_type=F32) + skip_ref[...]
    hi = jnp.dot(sm_ref[...], km_ref[...], preferred_element_type=F32)
    rowi = lax.broadcasted_iota(jnp.int32, (hr.shape[0], 1), 0) + i * hr.shape[0]
    first = rowi == 0
    hr_ref[...] = hr * jnp.where(first, 1.0 / n, 2.0 / n)
    hi_ref[...] = jnp.where(first, (ny_ref[...] + skip_ref[...]) * (1.0 / n), hi * (2.0 / n))


def _filter_spectrum(cm, sm, kf, km, ny, skip, tm, tn):
    length, cols = kf.shape
    mat = pl.BlockSpec((tm, length), lambda j, i: (i, 0))
    rhs = pl.BlockSpec((length, tn), lambda j, i: (0, j))
    vec = pl.BlockSpec((1, tn), lambda j, i: (0, j))
    out = pl.BlockSpec((tm, tn), lambda j, i: (i, j))
    return pl.pallas_call(
        functools.partial(_filter_spectrum_kernel, n=2 * length),
        grid=(cols // tn, length // tm),
        in_specs=[mat, mat, rhs, rhs, vec, vec],
        out_specs=[out, out],
        out_shape=[jax.ShapeDtypeStruct((length, cols), F32)] * 2,
        compiler_params=_params("parallel", "arbitrary"),
        name="hyena_filter_spectrum",
    )(cm, sm, kf, km, ny, skip.reshape(1, cols))


def _dft_fwd_kernel(cm_ref, sm_ref, u_ref, hr_ref, hi_ref, yr_ref, yi_ref):
    i = pl.program_id(2)
    u = u_ref[0]
    xr = jnp.dot(cm_ref[...], u, preferred_element_type=F32)
    xi = jnp.dot(sm_ref[...], u, preferred_element_type=F32)
    hr = hr_ref[...]
    hi = hi_ref[...]
    yr_ref[0] = (xr * hr - xi * hi).astype(yr_ref.dtype)
    yi_ref[0] = (xr * hi + xi * hr).astype(yi_ref.dtype)

    @pl.when(i == 0)
    def _():
        rowi = lax.broadcasted_iota(jnp.int32, (u.shape[0], 1), 0)
        sign = jnp.where(rowi % 2 == 0, 1.0, -1.0)
        nyq = jnp.sum(u.astype(F32) * sign, axis=0, keepdims=True)
        yi_ref[0, 0:1, :] = (nyq * hi[0:1, :]).astype(yi_ref.dtype)


def _dft_inv_kernel(cm_ref, sm_ref, yr_ref, yi_ref, gate_ref, o_ref):
    i = pl.program_id(2)
    yi = yi_ref[0]
    y = jnp.dot(cm_ref[...], yr_ref[0], preferred_element_type=F32)
    y = y + jnp.dot(sm_ref[...], yi, preferred_element_type=F32)
    tm = y.shape[0]
    rowi = lax.broadcasted_iota(jnp.int32, (tm, 1), 0) + i * tm
    sign = jnp.where(rowi % 2 == 0, 1.0, -1.0)
    y = y + sign * yi[0:1, :].astype(F32)
    o_ref[0] = (gate_ref[0] * y).astype(o_ref.dtype)


def _long_conv_gated(u, gate, cm, sm, hr, hi, order, out_dtype, tm, cb):
    b, length, c = u.shape
    nc = c // cb
    grid = (b, nc, length // tm)
    mat = pl.BlockSpec((tm, length), lambda bi, j, i: (i, 0))
    full = pl.BlockSpec((1, length, cb), lambda bi, j, i: (bi, 0, j))
    spec = pl.BlockSpec((tm, cb), lambda bi, j, i: (i, order * nc + j))
    tile = pl.BlockSpec((1, tm, cb), lambda bi, j, i: (bi, i, j))
    yr, yi = pl.pallas_call(
        _dft_fwd_kernel,
        grid=grid,
        in_specs=[mat, mat, full, spec, spec],
        out_specs=[tile, tile],
        out_shape=[jax.ShapeDtypeStruct((b, length, c), BF16)] * 2,
        compiler_params=_params("parallel", "parallel", "arbitrary"),
        name="hyena_dft_fwd",
    )(cm, sm, u, hr, hi)
    return pl.pallas_call(
        _dft_inv_kernel,
        grid=grid,
        in_specs=[mat, mat, full, full, tile],
        out_specs=tile,
        out_shape=jax.ShapeDtypeStruct((b, length, c), out_dtype),
        compiler_params=_params("parallel", "parallel", "arbitrary"),
        name="hyena_dft_inv",
    )(cm, sm, yr, yi, gate)


def _out1_kernel(x_ref, y_ref, z_ref, gate_ref, w_ref, fg_ref, o_ref):
    r = x_ref[0] + gate_ref[0] * _dot(y_ref[0] * _silu(z_ref[0]), w_ref[...])
    ms = jnp.mean(r * r, axis=-1, keepdims=True)
    o_ref[0] = r * lax.rsqrt(ms + EPS) * fg_ref[...]


def _out1(x, y, z, gate, w_out, final_g, tm):
    b, l, d = x.shape
    tile = pl.BlockSpec((1, tm, d), lambda bi, i: (bi, i, 0))
    return pl.pallas_call(
        _out1_kernel,
        grid=(b, l // tm),
        in_specs=[tile, tile, tile, pl.BlockSpec((1, 1, d), lambda bi, i: (bi, 0, 0)),
                  pl.BlockSpec(w_out.shape, lambda bi, i: (0, 0)), pl.BlockSpec((1, d), lambda bi, i: (0, 0))],
        out_specs=tile,
        out_shape=jax.ShapeDtypeStruct((b, l, d), F32),
        compiler_params=_params("parallel", "parallel"),
        name="out_proj1",
    )(x, y, z, gate, w_out.astype(BF16), final_g.reshape(1, d))


def _row_tile(l, want):
    return want if l % want == 0 else l


def _even_layer(x, ctx, c, c_ctx, norm_g, mod_w, mod_b, w_in, rpb, dn_conv, a_log, dt_bias, dn_norm_g, w_out):
    b, l, d = x.shape
    lc = ctx.shape[1]
    off_dn = 3 * NA_W
    off_ab = off_dn + DN_CONV_W
    off_z = off_ab + 4 * DN_HEADS
    rows = b + 1
    pad = (-rows) % 8
    cvecs = jnp.concatenate([c, c_ctx[None, :], jnp.zeros((pad, d), F32)], axis=0)
    m = _modulation(cvecs, mod_w, mod_b)
    shift, scale, gate = (m[:, i * d:(i + 1) * d] for i in range(3))
    lat = lambda a: a[:b, None, :]
    cx = lambda a: jnp.broadcast_to(a[b:b + 1, None, :], (b, 1, d))

    w_na = w_in[:, :off_dn]
    w_dn = w_in[:, off_dn:off_ab]
    w_ab = jnp.pad(w_in[:, off_ab:off_z], ((0, 0), (0, LANES - 4 * DN_HEADS)))
    w_z = w_in[:, off_z:]
    qkv_x, dnx, abx, z_x = _norm_proj(
        x, norm_g, lat(shift), lat(scale),
        [(w_na, "bf16", BF16), (w_dn, "bf16", F32), (w_ab, "f32", F32), (w_z, "bf16", F32)],
        _row_tile(l, 512))
    qkv_c, dnc, abc = _norm_proj(
        ctx, norm_g, cx(shift), cx(scale),
        [(w_na, "bf16", BF16), (w_dn, "bf16", F32), (w_ab, "f32", F32)], _row_tile(lc, 256))

    na_x = _neighbourhood_attention(qkv_x, qkv_c, rpb)

    dn_c, gb_c = _dn_prepare(dnc, abc, dn_conv, a_log, dt_bias, _row_tile(lc, 256))
    dn_x, gb_x = _dn_prepare(dnx, abx, dn_conv, a_log, dt_bias, _row_tile(l, 512))
    s0 = jnp.zeros((b, 2 * DN_HEADS, DN_DK, DN_DK), F32)
    _, _, s_ctx = _dn_scan(dn_c, gb_c, s0)
    o_f, o_b, _ = _dn_scan(dn_x, gb_x, s_ctx)

    return _out0(x, na_x, o_f, o_b, z_x, lat(gate), dn_norm_g, w_out, _row_tile(l, 512))


def _hyena_layer(x, c, norm_g, mod_w, mod_b, w_in, conv_w, fw1, fb1, ff1, fw2, fb2, ff2, fw3, skip, w_out,
                 final_g):
    b, l, d = x.shape
    pad = (-b) % 8
    cvecs = jnp.concatenate([c, jnp.zeros((pad, d), F32)], axis=0)
    m = _modulation(cvecs, mod_w, mod_b)
    shift, scale, gate = (m[:b, None, i * d:(i + 1) * d] for i in range(3))
    p_u, gz = _norm_proj(x, norm_g, shift, scale,
                         [(w_in[:, :3 * HY_W], "bf16", F32), (w_in[:, 3 * HY_W:], "bf16", F32)],
                         _row_tile(l, 512))
    v, x1, x2 = _conv3(p_u, conv_w, _row_tile(l, 512))

    h_raw, ss = _hyena_filters_raw(l, fw1, fb1, ff1, fw2, fb2, ff2, fw3, _row_tile(l, 256))
    kf, km, ny = _filter_mix(h_raw, ss, _row_tile(l, 256))
    cm, sm = _dft_matrices(l)
    tm = _row_tile(l, 512)
    hr, hi = _filter_spectrum(cm, sm, kf, km, ny, skip, tm, 512)

    z = _long_conv_gated(v, x1, cm, sm, hr, hi, 0, BF16, tm, 512)
    y = _long_conv_gated(z, x2, cm, sm, hr, hi, 1, F32, tm, 512)
    return _out1(x, y, gz, gate, w_out, final_g, _row_tile(l, 512))


def kernel(x, c, ctx, c_ctx, e_norm_g, e_mod_w, e_mod_b, e_w_in, e_na_rpb, e_dn_conv, e_dn_a_log, e_dn_dt_bias, e_dn_norm_g, e_w_out, o_norm_g, o_mod_w, o_mod_b, o_w_in, o_hy_conv, o_ffn_w1, o_ffn_b1, o_ffn_f1, o_ffn_w2, o_ffn_b2, o_ffn_f2, o_ffn_w3, o_hy_skip, o_w_out, final_norm_g):
    x = _even_layer(x, ctx, c, c_ctx, e_norm_g[0], e_mod_w[0], e_mod_b[0], e_w_in[0], e_na_rpb[0],
                    e_dn_conv[0], e_dn_a_log[0], e_dn_dt_bias[0], e_dn_norm_g[0], e_w_out[0])
    return _hyena_layer(x, c, o_norm_g[0], o_mod_w[0], o_mod_b[0], o_w_in[0], o_hy_conv[0],
                        o_ffn_w1[0], o_ffn_b1[0], o_ffn_f1[0], o_ffn_w2[0], o_ffn_b2[0], o_ffn_f2[0],
                        o_ffn_w3[0], o_hy_skip[0], o_w_out[0], final_norm_g)
```

```python
import functools
import math

import numpy as np
import jax
import jax.numpy as jnp
from jax import lax
from jax.experimental import pallas as pl
from jax.experimental.pallas import tpu as pltpu

F32 = jnp.float32
BF16 = jnp.bfloat16
HI = lax.Precision.HIGHEST
EPS = 1e-6
NEG = -1e30

LANES = 128
VMEM_LIMIT_BYTES = 56 * 1024 * 1024

GRID_W = 64
NA_HEADS = 8
NA_DH = 64
NA_W = NA_HEADS * NA_DH
NA_WIN_R = 8
NA_WIN_C = 16
NA_QROWS = 4
NA_KROWS = NA_QROWS + NA_WIN_R

DN_HEADS = 4
DN_DK = 128
DN_W = DN_HEADS * DN_DK
DN_CONV_W = 3 * DN_W
DN_CONV = 5
DN_CHUNK = 128

HY_W = 1024
HY_EMB = 33
HY_BANDS = (HY_EMB - 1) // 2
HY_DECAY_MIN = math.log(1e-2) / 1.5
HY_DECAY_MAX = math.log(1e-2) / 0.3


def _params(*sem):
    return pltpu.CompilerParams(dimension_semantics=sem, vmem_limit_bytes=VMEM_LIMIT_BYTES)


def _silu(v):
    return v * jax.nn.sigmoid(v)


def _dot(a, b):
    return jnp.dot(a.astype(BF16), b.astype(BF16), preferred_element_type=F32)


def _dot_nt(a, b):
    return lax.dot_general(a.astype(BF16), b.astype(BF16), (((1,), (1,)), ((), ())),
                           preferred_element_type=F32)


def _dot_tn(a, b):
    return lax.dot_general(a.astype(BF16), b.astype(BF16), (((0,), (0,)), ((), ())),
                           preferred_element_type=F32)


def _mod_kernel(c_ref, w_ref, b_ref, o_ref):
    o_ref[...] = jnp.dot(_silu(c_ref[...]), w_ref[...], precision=HI,
                         preferred_element_type=F32) + b_ref[...]


def _modulation(cvecs, w, b):
    r, d = cvecs.shape
    n = w.shape[1]
    tn = 512
    return pl.pallas_call(
        _mod_kernel,
        grid=(n // tn,),
        in_specs=[pl.BlockSpec((r, d), lambda j: (0, 0)),
                  pl.BlockSpec((d, tn), lambda j: (0, j)),
                  pl.BlockSpec((1, tn), lambda j: (0, j))],
        out_specs=pl.BlockSpec((r, tn), lambda j: (0, j)),
        out_shape=jax.ShapeDtypeStruct((r, n), F32),
        compiler_params=_params("parallel"),
        name="adaln_mod",
    )(cvecs, w, b.reshape(1, n))


def _proj_kernel(x_ref, g_ref, sh_ref, sc_ref, *refs, precs):
    n = len(precs)
    w_refs, o_refs = refs[:n], refs[n:]
    x = x_ref[0]
    ms = jnp.mean(x * x, axis=-1, keepdims=True)
    h = x * lax.rsqrt(ms + EPS) * g_ref[...] * (1.0 + sc_ref[0]) + sh_ref[0]
    hb = h.astype(BF16)
    for w_ref, o_ref, prec in zip(w_refs, o_refs, precs):
        if prec == "bf16":
            r = jnp.dot(hb, w_ref[...], preferred_element_type=F32)
        else:
            r = jnp.dot(h, w_ref[...], precision=HI, preferred_element_type=F32)
        o_ref[0] = r.astype(o_ref.dtype)


def _norm_proj(x, norm_g, shift, scale, sections, tm):
    b, l, d = x.shape
    ws, precs, out_shapes, out_specs, w_specs = [], [], [], [], []
    for w, prec, odt in sections:
        ws.append(w.astype(BF16) if prec == "bf16" else w)
        precs.append(prec)
        n = w.shape[1]
        w_specs.append(pl.BlockSpec((d, n), lambda bi, i: (0, 0)))
        out_specs.append(pl.BlockSpec((1, tm, n), lambda bi, i: (bi, i, 0)))
        out_shapes.append(jax.ShapeDtypeStruct((b, l, n), odt))
    vec = pl.BlockSpec((1, 1, d), lambda bi, i: (bi, 0, 0))
    return pl.pallas_call(
        functools.partial(_proj_kernel, precs=tuple(precs)),
        grid=(b, l // tm),
        in_specs=[pl.BlockSpec((1, tm, d), lambda bi, i: (bi, i, 0)),
                  pl.BlockSpec((1, d), lambda bi, i: (0, 0)), vec, vec] + w_specs,
        out_specs=out_specs,
        out_shape=out_shapes,
        compiler_params=_params("parallel", "parallel"),
        name="norm_proj",
    )(x, norm_g.reshape(1, d), shift, scale, *ws)


def _na_block_geometry(rows):
    return ((0, 0), (NA_QROWS, 0), (rows - NA_QROWS, rows - NA_KROWS))


def _na_bias_kernel(rc_ref, o_ref, *, rows):
    wr = min(NA_WIN_R, rows)
    qc = lax.broadcasted_iota(jnp.int32, (GRID_W, LANES), 0)
    lane = lax.broadcasted_iota(jnp.int32, (GRID_W, LANES), 1)
    kc = lane & (GRID_W - 1)
    c0 = jnp.clip(qc - NA_WIN_C // 2, 0, GRID_W - NA_WIN_C)
    col_ok = (kc >= c0) & (kc < c0 + NA_WIN_C)
    neg = jnp.full((GRID_W, LANES), NEG, F32)
    tiles = []
    for dr in range(2 * NA_WIN_R - 1):
        base = jnp.broadcast_to(rc_ref[0, dr:dr + 1, :], (GRID_W, LANES))
        tiles.append(jnp.where(col_ok, pltpu.roll(base, 0, 1, stride=1, stride_axis=0), neg))
    for g, (r_first, k_first) in enumerate(_na_block_geometry(rows)):
        for i in range(NA_QROWS):
            qr = r_first + i
            r0 = min(max(qr - wr // 2, 0), rows - wr)
            for jp in range(NA_KROWS // 2):
                halves = []
                for j in (2 * jp, 2 * jp + 1):
                    kr = k_first + j
                    halves.append(tiles[kr - qr + NA_WIN_R - 1] if r0 <= kr < r0 + wr else neg)
                o_ref[g, 0, i * GRID_W:(i + 1) * GRID_W, jp * LANES:(jp + 1) * LANES] = jnp.where(
                    lane < GRID_W, halves[0], halves[1])


def _na_bias_table(rpb, rows):
    wc = NA_WIN_C
    fill = jnp.full(rpb.shape[:2] + (GRID_W - (2 * wc - 1),), NEG, F32)
    ring = jnp.concatenate([rpb[..., wc - 1:], fill, rpb[..., :wc - 1]], axis=-1)
    ring = jnp.concatenate([ring, ring], axis=-1)
    nq, nk = NA_QROWS * GRID_W, NA_KROWS * GRID_W
    return pl.pallas_call(
        functools.partial(_na_bias_kernel, rows=rows),
        grid=(NA_HEADS,),
        in_specs=[pl.BlockSpec((1, 2 * NA_WIN_R - 1, LANES), lambda h: (h, 0, 0))],
        out_specs=pl.BlockSpec((3, 1, nq, nk), lambda h: (0, h, 0, 0)),
        out_shape=jax.ShapeDtypeStruct((3, NA_HEADS, nq, nk), F32),
        compiler_params=_params("parallel"),
        name="na_bias_table",
    )(ring)


def _na_kernel(q_ref, k_ref, v_ref, kc_ref, vc_ref, bias_ref, o_ref, *, rows):
    blk = pl.program_id(1)
    nq = NA_QROWS * GRID_W
    nk = NA_KROWS * GRID_W
    k_first = jnp.clip(blk * NA_QROWS - NA_WIN_R // 2, 0, rows - NA_KROWS)
    start = pl.multiple_of(k_first * GRID_W, GRID_W)
    lane = lax.broadcasted_iota(jnp.int32, (nq, LANES), 1)
    low = lane < NA_DH
    for p in range(NA_W // LANES):
        cs = slice(p * LANES, (p + 1) * LANES)
        q2 = q_ref[0, :, cs]
        k2 = k_ref[0, pl.ds(start, nk), cs]
        v2 = v_ref[0, pl.ds(start, nk), cs]
        kc2 = kc_ref[0, :, cs]
        vc2 = vc_ref[0, :, cs]
        halves = []
        for hh in range(2):
            sel = low if hh == 0 else jnp.logical_not(low)
            qh = jnp.where(sel, q2, jnp.zeros_like(q2))
            s_win = _dot_nt(qh, k2) * (NA_DH ** -0.5) + bias_ref[0, 2 * p + hh]
            s_ctx = _dot_nt(qh, kc2) * (NA_DH ** -0.5)
            m = jnp.maximum(jnp.max(s_win, axis=-1, keepdims=True),
                            jnp.max(s_ctx, axis=-1, keepdims=True))
            p_win = jnp.exp(s_win - m)
            p_ctx = jnp.exp(s_ctx - m)
            den = jnp.sum(p_win, axis=-1, keepdims=True) + jnp.sum(p_ctx, axis=-1, keepdims=True)
            halves.append((_dot(p_win, v2) + _dot(p_ctx, vc2)) / den)
        o_ref[0, :, cs] = jnp.where(low, halves[0], halves[1])


def _neighbourhood_attention(qkv, qkv_c, rpb):
    b, l, _ = qkv.shape
    lc = qkv_c.shape[1]
    rows = l // GRID_W
    assert rows % NA_QROWS == 0 and rows >= NA_KROWS + 1
    nq = NA_QROWS * GRID_W
    nblk = rows // NA_QROWS
    bias = _na_bias_table(rpb, rows)

    def cfg(bi, i):
        return (jnp.where(i == 0, 0, jnp.where(i == nblk - 1, 2, 1)), 0, 0, 0)

    return pl.pallas_call(
        functools.partial(_na_kernel, rows=rows),
        grid=(b, nblk),
        in_specs=[pl.BlockSpec((1, nq, NA_W), lambda bi, i: (bi, i, 0)),
                  pl.BlockSpec((1, l, NA_W), lambda bi, i: (bi, 0, 1)),
                  pl.BlockSpec((1, l, NA_W), lambda bi, i: (bi, 0, 2)),
                  pl.BlockSpec((1, lc, NA_W), lambda bi, i: (bi, 0, 1)),
                  pl.BlockSpec((1, lc, NA_W), lambda bi, i: (bi, 0, 2)),
                  pl.BlockSpec((1, NA_HEADS, nq, NA_KROWS * GRID_W), cfg)],
        out_specs=pl.BlockSpec((1, nq, NA_W), lambda bi, i: (bi, i, 0)),
        out_shape=jax.ShapeDtypeStruct((b, l, NA_W), F32),
        compiler_params=_params("parallel", "arbitrary"),
        name="neighbourhood_attention",
    )(qkv, qkv, qkv, qkv_c, qkv_c, bias)


def _halo_specs(tl, width, l):
    nb8 = tl // 8
    last8 = l // 8 - 1
    return [pl.BlockSpec((1, tl, width), lambda bi, i: (bi, i, 0)),
            pl.BlockSpec((1, 8, width), lambda bi, i: (bi, jnp.maximum(i * nb8 - 1, 0), 0)),
            pl.BlockSpec((1, 8, width), lambda bi, i: (bi, jnp.minimum((i + 1) * nb8, last8), 0))]


def _dwconv_tile(x_ref, xp_ref, xn_ref, cw_ref, taps):
    i = pl.program_id(1)
    x = x_ref[0]
    tl = x.shape[0]
    prev = jnp.where(i > 0, xp_ref[0], 0.0)
    nxt = jnp.where(i < pl.num_programs(1) - 1, xn_ref[0], 0.0)
    xe = jnp.concatenate([prev, x, nxt], axis=0)
    half = taps // 2
    acc = jnp.zeros_like(x)
    for j in range(taps):
        off = 8 - half + j
        acc = acc + xe[off:off + tl] * cw_ref[j:j + 1, :]
    return acc


def _dnprep_kernel(x_ref, xp_ref, xn_ref, ab_ref, cw_ref, al_ref, dtb_ref, qkv_ref, gb_ref):
    t = _silu(_dwconv_tile(x_ref, xp_ref, xn_ref, cw_ref, DN_CONV))
    segs = []
    for hh in range(3 * DN_HEADS):
        seg = t[:, hh * DN_DK:(hh + 1) * DN_DK]
        if hh < 2 * DN_HEADS:
            seg = seg * lax.rsqrt(jnp.sum(seg * seg, axis=-1, keepdims=True) + EPS)
        if hh < DN_HEADS:
            seg = seg * (DN_DK ** -0.5)
        segs.append(seg)
    qkv_ref[0] = jnp.concatenate(segs, axis=1)
    ab = ab_ref[0]
    z = ab + dtb_ref[...]
    softplus = jnp.maximum(z, 0.0) + jnp.log(1.0 + jnp.exp(-jnp.abs(z)))
    g = -jnp.exp(al_ref[...]) * softplus
    lane = lax.broadcasted_iota(jnp.int32, ab.shape, 1)
    gb_ref[0] = jnp.where(lane < 2 * DN_HEADS, g, jax.nn.sigmoid(ab))


def _dn_prepare(qkv, ab, conv_w, a_log, dt_bias, tl):
    b, l, w = qkv.shape
    pad = LANES - 2 * DN_HEADS
    al = jnp.pad(a_log.reshape(1, 2 * DN_HEADS), ((0, 0), (0, pad)))
    dtb = jnp.pad(dt_bias.reshape(1, 2 * DN_HEADS), ((0, 0), (0, pad)))
    const = lambda shape: pl.BlockSpec(shape, lambda bi, i: (0, 0))
    return pl.pallas_call(
        _dnprep_kernel,
        grid=(b, l // tl),
        in_specs=_halo_specs(tl, w, l) + [pl.BlockSpec((1, tl, LANES), lambda bi, i: (bi, i, 0)),
                                         const((DN_CONV, w)), const((1, LANES)), const((1, LANES))],
        out_specs=[pl.BlockSpec((1, tl, w), lambda bi, i: (bi, i, 0)),
                   pl.BlockSpec((1, tl, LANES), lambda bi, i: (bi, i, 0))],
        out_shape=[jax.ShapeDtypeStruct((b, l, w), F32), jax.ShapeDtypeStruct((b, l, LANES), F32)],
        compiler_params=_params("parallel", "parallel"),
        name="dn_prepare",
    )(qkv, qkv, qkv, ab, conv_w, al, dtb)


DN_INV_BASE = 16


def _unit_triangular_inverse(nil, eye, row, col):
    c = nil.shape[-1]
    sh = int(math.log2(DN_INV_BASE))
    diag = jnp.where((row >> sh) == (col >> sh), nil, 0.0)
    inv = eye - diag
    pw = diag
    for _ in range(sh - 1):
        pw = _bdot(pw, pw)
        inv = inv + _bdot(inv, pw)
    while (1 << sh) < c:
        off = jnp.where(((row >> (sh + 1)) == (col >> (sh + 1))) & ((row >> sh) != (col >> sh)), nil, 0.0)
        inv = inv - _bdot(inv, _bdot(off, inv))
        sh += 1
    return inv


def _bdot(a, b):
    return lax.dot_general(a.astype(BF16), b.astype(BF16), (((2,), (1,)), ((0,), (0,))),
                           preferred_element_type=F32)


def _bdot_nt(a, b):
    return lax.dot_general(a.astype(BF16), b.astype(BF16), (((2,), (2,)), ((0,), (0,))),
                           preferred_element_type=F32)


def _bdot_tn(a, b):
    return lax.dot_general(a.astype(BF16), b.astype(BF16), (((1,), (1,)), ((0,), (0,))),
                           preferred_element_type=F32)


def _dn_scan_kernel(xf_ref, xb_ref, gf_ref, gbk_ref, s0_ref, of_ref, ob_ref, sfin_ref, s_ref):
    t = pl.program_id(1)
    c = DN_CHUNK
    nh = DN_HEADS

    @pl.when(t == 0)
    def _():
        s_ref[...] = s0_ref[0]

    row = lax.broadcasted_iota(jnp.int32, (c, c), 0)
    col = lax.broadcasted_iota(jnp.int32, (c, c), 1)
    eye = (row == col).astype(F32)
    incl = jnp.stack([row >= col] * nh + [row <= col] * nh)
    strict = jnp.stack([row > col] * nh + [row < col] * nh)
    qs, ks, vs, g_bs, betas = [], [], [], [], []
    for d, (x_ref, g_ref) in enumerate(((xf_ref, gf_ref), (xb_ref, gbk_ref))):
        gb = g_ref[0]
        for h in range(nh):
            qs.append(x_ref[0, :, h * DN_DK:(h + 1) * DN_DK])
            ks.append(x_ref[0, :, DN_W + h * DN_DK:DN_W + (h + 1) * DN_DK])
            vs.append(x_ref[0, :, 2 * DN_W + h * DN_DK:2 * DN_W + (h + 1) * DN_DK])
            ci = d * nh + h
            g_bs.append(jnp.broadcast_to(gb[:, ci:ci + 1], (c, c)))
            betas.append(gb[:, 2 * nh + ci:2 * nh + ci + 1])
    q, k, v, g_b, beta = (jnp.stack(a) for a in (qs, ks, vs, g_bs, betas))
    gc_rows = lax.dot_general(incl.astype(F32), g_b, (((2,), (1,)), ((0,), (0,))), precision=HI,
                              preferred_element_type=F32)
    gc_cols = jnp.swapaxes(gc_rows, 1, 2)
    decay = jnp.where(incl, jnp.exp(jnp.where(incl, gc_rows - gc_cols, 0.0)), 0.0)
    kb = k * beta
    vb = v * beta
    nil = jnp.where(strict, _bdot_nt(kb, k) * decay, 0.0)
    aqk = _bdot_nt(q, k) * decay
    egc = jnp.exp(gc_rows)
    inv = _unit_triangular_inverse(nil, eye, row, col)
    u = _bdot(inv, vb)
    w = _bdot(inv, kb * egc)
    s = s_ref[...]
    v_new = u - _bdot(w, s)
    o = _bdot(q * egc, s) + _bdot(aqk, v_new)
    g_last = jnp.concatenate([gc_rows[:nh, c - 1:c, :], gc_rows[nh:, 0:1, :]], axis=0)
    s_ref[...] = s * jnp.exp(g_last) + _bdot_tn(k * jnp.exp(g_last - gc_rows), v_new)
    for h in range(nh):
        of_ref[0, :, h * DN_DK:(h + 1) * DN_DK] = o[h]
        ob_ref[0, :, h * DN_DK:(h + 1) * DN_DK] = o[nh + h]

    @pl.when(t == pl.num_programs(1) - 1)
    def _():
        sfin_ref[0] = s_ref[...]


def _dn_scan(qkv, gb, s0):
    b, l, w = qkv.shape
    n = l // DN_CHUNK
    fwd = lambda bi, t: (bi, t, 0)
    bwd = lambda bi, t: (bi, n - 1 - t, 0)
    state = pl.BlockSpec((1, 2 * DN_HEADS, DN_DK, DN_DK), lambda bi, t: (bi, 0, 0, 0))
    return pl.pallas_call(
        _dn_scan_kernel,
        grid=(b, n),
        in_specs=[pl.BlockSpec((1, DN_CHUNK, w), fwd), pl.BlockSpec((1, DN_CHUNK, w), bwd),
                  pl.BlockSpec((1, DN_CHUNK, LANES), fwd), pl.BlockSpec((1, DN_CHUNK, LANES), bwd),
                  state],
        out_specs=[pl.BlockSpec((1, DN_CHUNK, DN_W), fwd), pl.BlockSpec((1, DN_CHUNK, DN_W), bwd),
                   state],
        out_shape=[jax.ShapeDtypeStruct((b, l, DN_W), F32), jax.ShapeDtypeStruct((b, l, DN_W), F32),
                   jax.ShapeDtypeStruct((b, 2 * DN_HEADS, DN_DK, DN_DK), F32)],
        scratch_shapes=[pltpu.VMEM((2 * DN_HEADS, DN_DK, DN_DK), F32)],
        compiler_params=_params("parallel", "arbitrary"),
        name="dn_scan",
    )(qkv, qkv, gb, gb, s0)


def _out0_kernel(x_ref, na_ref, of_ref, ob_ref, z_ref, gate_ref, ng_ref, w_ref, o_ref):
    o = of_ref[0] + ob_ref[0]
    segs = [na_ref[0]]
    for h in range(DN_HEADS):
        seg = o[:, h * DN_DK:(h + 1) * DN_DK]
        ms = jnp.mean(seg * seg, axis=-1, keepdims=True)
        segs.append(seg * lax.rsqrt(ms + EPS) * ng_ref[...])
    mix = jnp.concatenate(segs, axis=1) * _silu(z_ref[0])
    o_ref[0] = x_ref[0] + gate_ref[0] * _dot(mix, w_ref[...])


def _out0(x, na, o_f, o_b, z, gate, norm_g, w_out, tm):
    b, l, d = x.shape
    tile = lambda n: pl.BlockSpec((1, tm, n), lambda bi, i: (bi, i, 0))
    return pl.pallas_call(
        _out0_kernel,
        grid=(b, l // tm),
        in_specs=[tile(d), tile(NA_W), tile(DN_W), tile(DN_W), tile(d),
                  pl.BlockSpec((1, 1, d), lambda bi, i: (bi, 0, 0)),
                  pl.BlockSpec((1, DN_DK), lambda bi, i: (0, 0)),
                  pl.BlockSpec(w_out.shape, lambda bi, i: (0, 0))],
        out_specs=tile(d),
        out_shape=jax.ShapeDtypeStruct((b, l, d), F32),
        compiler_params=_params("parallel", "parallel"),
        name="out_proj0",
    )(x, na, o_f, o_b, z, gate, norm_g.reshape(1, DN_DK), w_out.astype(BF16))


def _conv3_kernel(x_ref, cw_ref, o_ref):
    x = x_ref[0]
    p_n, q_n, cw = x.shape
    zero = jnp.zeros((1, 1, cw), x.dtype)
    first_prev = jnp.concatenate([zero, x[p_n - 1:p_n, :q_n - 1]], axis=1)
    last_next = jnp.concatenate([x[0:1, 1:], zero], axis=1)
    prev = jnp.concatenate([first_prev, x[:p_n - 1]], axis=0)
    nxt = jnp.concatenate([x[1:], last_next], axis=0)
    w = cw_ref[...]
    o_ref[0] = prev * w[0:1, :] + x * w[1:2, :] + nxt * w[2:3, :]


def _conv3(p_u, conv_w, cw):
    b, p_n, q_n, w = p_u.shape
    blk = pl.BlockSpec((1, p_n, q_n, cw), lambda bi, j: (bi, 0, 0, j))
    return pl.pallas_call(
        _conv3_kernel,
        grid=(b, w // cw),
        in_specs=[blk, pl.BlockSpec((3, cw), lambda bi, j: (0, j))],
        out_specs=blk,
        out_shape=jax.ShapeDtypeStruct(p_u.shape, F32),
        compiler_params=_params("parallel", "parallel"),
        name="hyena_conv3",
    )(p_u, conv_w)


def _filter_kernel(feat_ref, env_ref, w1_ref, b1_ref, f1_ref, w2_ref, b2_ref, f2_ref, w3_ref,
                   h_ref, ss_ref):
    hid = jnp.sin(f1_ref[...] * (jnp.dot(feat_ref[...], w1_ref[...], precision=HI,
                                         preferred_element_type=F32) + b1_ref[...]))
    hid = jnp.sin(f2_ref[...] * (jnp.dot(hid, w2_ref[...], precision=HI,
                                         preferred_element_type=F32) + b2_ref[...]))
    h = jnp.dot(hid, w3_ref[...], precision=HI, preferred_element_type=F32)
    h = h * jnp.concatenate([env_ref[...]] * 4, axis=1)
    h_ref[...] = h

    @pl.when(pl.program_id(0) == 0)
    def _():
        ss_ref[...] = jnp.zeros_like(ss_ref)

    ss_ref[...] += jnp.sum(h * h, axis=0, keepdims=True)


def _hyena_filters_raw(length, w1, b1, f1, w2, b2, f2, w3, tl):
    t = jnp.linspace(0.0, 1.0, length, dtype=F32)[:, None]
    wv = 2.0 * math.pi * jnp.arange(length, dtype=F32)[:, None] / length
    f = jnp.linspace(1e-4, HY_BANDS - 1, HY_BANDS, dtype=F32)[None, :]
    feats = jnp.concatenate([t, jnp.cos(f * wv), -jnp.sin(f * wv)], axis=-1)
    decay = jnp.abs(jnp.linspace(HY_DECAY_MIN, HY_DECAY_MAX, HY_W, dtype=F32))
    env = jnp.exp(-t * decay)
    feats, env = _time_major_to_pq(feats[None])[0], _time_major_to_pq(env[None])[0]
    feats, env = feats.reshape(length, -1), env.reshape(length, -1)
    ffn = w1.shape[1]
    pe, pf = LANES - HY_EMB, LANES - ffn
    feats = jnp.pad(feats, ((0, 0), (0, pe)))
    w1p = jnp.pad(w1, ((0, pe), (0, pf)))
    w2p = jnp.pad(w2, ((0, pf), (0, pf)))
    w3p = jnp.pad(w3, ((0, pf), (0, 0)))
    vec = lambda a: jnp.pad(a.reshape(1, ffn), ((0, 0), (0, pf)))
    n_out = w3.shape[1]
    const = lambda shape: pl.BlockSpec(shape, lambda i: (0, 0))
    return pl.pallas_call(
        _filter_kernel,
        grid=(length // tl,),
        in_specs=[pl.BlockSpec((tl, LANES), lambda i: (i, 0)), pl.BlockSpec((tl, HY_W), lambda i: (i, 0)),
                  const((LANES, LANES)), const((1, LANES)), const((1, LANES)),
                  const((LANES, LANES)), const((1, LANES)), const((1, LANES)), const((LANES, n_out))],
        out_specs=[pl.BlockSpec((tl, n_out), lambda i: (i, 0)), const((1, n_out))],
        out_shape=[jax.ShapeDtypeStruct((length, n_out), F32), jax.ShapeDtypeStruct((1, n_out), F32)],
        compiler_params=_params("arbitrary"),
        name="hyena_filter_ffn",
    )(feats, env, w1p, vec(b1), vec(f1), w2p, vec(b2), vec(f2), w3p)


def _filter_norm_kernel(h_ref, ss_ref, o_ref):
    o_ref[...] = (h_ref[...] * lax.rsqrt(ss_ref[...] + EPS)).astype(o_ref.dtype)


def _filter_normalise(h_raw, ss, tl):
    length, n = h_raw.shape
    return pl.pallas_call(
        _filter_norm_kernel,
        grid=(length // tl,),
        in_specs=[pl.BlockSpec((tl, n), lambda i: (i, 0)), pl.BlockSpec((1, n), lambda i: (0, 0))],
        out_specs=pl.BlockSpec((tl, n), lambda i: (i, 0)),
        out_shape=jax.ShapeDtypeStruct((length, n), BF16),
        compiler_params=_params("parallel"),
        name="hyena_filter_norm",
    )(h_raw, ss)


HY_P = 64
HY_KQ_PAD = 8
HY_ROW_PAD = 8


def _time_major_to_pq(a):
    b, l, c = a.shape
    return a.reshape(b, l // HY_P, HY_P, c).transpose(0, 2, 1, 3)


def _pq_to_time_major(a):
    b, p, q, c = a.shape
    return a.transpose(0, 2, 1, 3).reshape(b, p * q, c)


def _fft_tables(length):
    p_n = HY_P
    q_n = length // p_n
    qn2 = 2 * q_n
    n = 2 * length
    kq_n = q_n + HY_KQ_PAD
    kq = np.arange(kq_n)
    live = (kq <= q_n)[:, None]
    a1 = 2.0 * np.pi * ((kq[:, None] * np.arange(q_n)[None, :]) % qn2) / qn2
    c1, s1 = np.cos(a1) * live, np.sin(a1) * live
    g1 = np.concatenate([c1, -s1], axis=0)
    g1i = np.concatenate([c1.T, -s1.T], axis=1)
    pp = np.arange(p_n)
    a2 = 2.0 * np.pi * ((pp[:, None] * pp[None, :]) % p_n) / p_n
    c2, s2 = np.cos(a2), np.sin(a2)
    f2 = np.block([[c2, s2], [-s2, c2]])
    f2i = np.block([[c2, -s2], [s2, c2]])
    at = 2.0 * np.pi * (kq[:, None] * pp[None, :]) / n
    bf = lambda m: jnp.asarray(m, F32).astype(BF16)
    lanes = lambda m: jnp.broadcast_to(jnp.asarray(m, F32)[:, :, None], (kq_n, p_n, LANES))
    return bf(g1), bf(g1i), bf(f2), bf(f2i), lanes(np.cos(at)), lanes(np.sin(at))


def _fftconv_kernel(x_ref, *refs, spectrum_only):
    if spectrum_only:
        g1_ref, f2_ref, twc_ref, tws_ref, o_ref, sre_ref, sim_ref = refs
    else:
        gate_ref, h_ref, g1_ref, g1i_ref, f2_ref, f2i_ref, twc_ref, tws_ref, o_ref, sre_ref, sim_ref = refs
    p_n = x_ref.shape[1]
    kq_n = g1_ref.shape[0] // 2
    pitch = p_n + HY_ROW_PAD

    def stage1(p, carry):
        a = jnp.dot(g1_ref[...], x_ref[0, p].astype(BF16), preferred_element_type=F32)
        sre_ref[pl.ds(p, kq_n, stride=pitch), :] = a[:kq_n]
        sim_ref[pl.ds(p, kq_n, stride=pitch), :] = a[kq_n:]
        return carry

    lax.fori_loop(0, p_n, stage1, 0, unroll=8)

    def stage2(kq, carry):
        base = pl.multiple_of(kq * pitch, 8)
        ar = sre_ref[pl.ds(base, p_n), :]
        ai = sim_ref[pl.ds(base, p_n), :]
        c = twc_ref[kq]
        s = tws_ref[kq]
        sr = ar * c + ai * s
        si = ai * c - ar * s
        x = jnp.dot(f2_ref[...], jnp.concatenate([sr, si], axis=0).astype(BF16),
                    preferred_element_type=F32)
        if spectrum_only:
            o_ref[kq] = x
            return carry
        xr, xi = x[:p_n], x[p_n:]
        h = h_ref[0, kq]
        hr, hi = h[:p_n], h[p_n:]
        y = jnp.concatenate([xr * hr - xi * hi, xr * hi + xi * hr], axis=0)
        z = jnp.dot(f2i_ref[...], y.astype(BF16), preferred_element_type=F32)
        zr, zi = z[:p_n], z[p_n:]
        sre_ref[pl.ds(base, p_n), :] = zr * c - zi * s
        sim_ref[pl.ds(base, p_n), :] = zr * s + zi * c
        return carry

    lax.fori_loop(0, kq_n, stage2, 0, unroll=8)
    if spectrum_only:
        return

    def stage3(p, carry):
        zr = sre_ref[pl.ds(p, kq_n, stride=pitch), :]
        zi = sim_ref[pl.ds(p, kq_n, stride=pitch), :]
        y = jnp.dot(g1i_ref[...], jnp.concatenate([zr, zi], axis=0).astype(BF16),
                    preferred_element_type=F32)
        o_ref[0, p] = (gate_ref[0, p] * y).astype(o_ref.dtype)
        return carry

    lax.fori_loop(0, p_n, stage3, 0, unroll=8)


def _fft_scratch(kq_n, p_n):
    rows = kq_n * (p_n + HY_ROW_PAD)
    return [pltpu.VMEM((rows, LANES), F32), pltpu.VMEM((rows, LANES), F32)]


def _const_spec(a):
    return pl.BlockSpec(a.shape, lambda *_: (0,) * a.ndim)


def _filter_spectra(h_pq, tables):
    g1, _, f2, _, twc, tws = tables
    _, p_n, q_n, c = h_pq.shape
    kq_n = g1.shape[0] // 2
    return pl.pallas_call(
        functools.partial(_fftconv_kernel, spectrum_only=True),
        grid=(c // LANES,),
        in_specs=[pl.BlockSpec((1, p_n, q_n, LANES), lambda j: (0, 0, 0, j)),
                  _const_spec(g1), _const_spec(f2), _const_spec(twc), _const_spec(tws)],
        out_specs=pl.BlockSpec((kq_n, 2 * p_n, LANES), lambda j: (0, 0, j)),
        out_shape=jax.ShapeDtypeStruct((kq_n, 2 * p_n, c), F32),
        scratch_shapes=_fft_scratch(kq_n, p_n),
        compiler_params=_params("parallel"),
        name="hyena_filter_fft",
    )(h_pq, g1, f2, twc, tws)


def _filter_response_kernel(x_ref, skip_ref, o_ref, *, q_n, n):
    kq = pl.program_id(0)
    coef = jnp.where((kq == 0) | (kq == q_n), 1.0 / n, 2.0 / n)
    x = x_ref[0]
    p_n = x.shape[0] // 2
    re, im = x[:p_n], x[p_n:]
    for o in range(2):
        f = slice(2 * o * HY_W, (2 * o + 1) * HY_W)
        bk = slice((2 * o + 1) * HY_W, (2 * o + 2) * HY_W)
        hr = (re[:, f] + re[:, bk] + skip_ref[o:o + 1, :]) * coef
        hi = (im[:, f] - im[:, bk]) * coef
        o_ref[o, 0] = jnp.concatenate([hr, hi], axis=0)


def _filter_response(spec, skip, q_n):
    kq_n, p2, c = spec.shape
    return pl.pallas_call(
        functools.partial(_filter_response_kernel, q_n=q_n, n=2 * q_n * (p2 // 2)),
        grid=(kq_n,),
        in_specs=[pl.BlockSpec((1, p2, c), lambda k: (k, 0, 0)), pl.BlockSpec(skip.shape, lambda k: (0, 0))],
        out_specs=pl.BlockSpec((2, 1, p2, HY_W), lambda k: (0, k, 0, 0)),
        out_shape=jax.ShapeDtypeStruct((2, kq_n, p2, HY_W), F32),
        compiler_params=_params("parallel"),
        name="hyena_filter_response",
    )(spec, skip)


def _long_conv_gated(u, u_block0, gate, gate_block0, resp, order, tables, out_dtype):
    g1, g1i, f2, f2i, twc, tws = tables
    b, p_n, q_n, _ = u.shape
    kq_n = g1.shape[0] // 2
    blk = lambda off: pl.BlockSpec((1, p_n, q_n, LANES), lambda bi, j: (bi, 0, 0, off + j))
    return pl.pallas_call(
        functools.partial(_fftconv_kernel, spectrum_only=False),
        grid=(b, HY_W // LANES),
        in_specs=[blk(u_block0), blk(gate_block0),
                  pl.BlockSpec((1, kq_n, 2 * p_n, LANES), lambda bi, j: (order, 0, 0, j)),
                  _const_spec(g1), _const_spec(g1i), _const_spec(f2), _const_spec(f2i),
                  _const_spec(twc), _const_spec(tws)],
        out_specs=blk(0),
        out_shape=jax.ShapeDtypeStruct((b, p_n, q_n, HY_W), out_dtype),
        scratch_shapes=_fft_scratch(kq_n, p_n),
        compiler_params=_params("parallel", "parallel"),
        name="hyena_fftconv",
    )(u, gate, resp, g1, g1i, f2, f2i, twc, tws)


def _out1_kernel(x_ref, y_ref, z_ref, gate_ref, w_ref, fg_ref, o_ref):
    r = x_ref[0] + gate_ref[0] * _dot(y_ref[0] * _silu(z_ref[0]), w_ref[...])
    ms = jnp.mean(r * r, axis=-1, keepdims=True)
    o_ref[0] = r * lax.rsqrt(ms + EPS) * fg_ref[...]


def _out1(x, y, z, gate, w_out, final_g, tm):
    b, l, d = x.shape
    tile = pl.BlockSpec((1, tm, d), lambda bi, i: (bi, i, 0))
    return pl.pallas_call(
        _out1_kernel,
        grid=(b, l // tm),
        in_specs=[tile, tile, tile, pl.BlockSpec((1, 1, d), lambda bi, i: (bi, 0, 0)),
                  pl.BlockSpec(w_out.shape, lambda bi, i: (0, 0)), pl.BlockSpec((1, d), lambda bi, i: (0, 0))],
        out_specs=tile,
        out_shape=jax.ShapeDtypeStruct((b, l, d), F32),
        compiler_params=_params("parallel", "parallel"),
        name="out_proj1",
    )(x, y, z, gate, w_out.astype(BF16), final_g.reshape(1, d))


def _row_tile(l, want):
    return want if l % want == 0 else l


def _even_layer(x, ctx, c, c_ctx, norm_g, mod_w, mod_b, w_in, rpb, dn_conv, a_log, dt_bias, dn_norm_g, w_out):
    b, l, d = x.shape
    lc = ctx.shape[1]
    off_dn = 3 * NA_W
    off_ab = off_dn + DN_CONV_W
    off_z = off_ab + 4 * DN_HEADS
    rows = b + 1
    pad = (-rows) % 8
    cvecs = jnp.concatenate([c, c_ctx[None, :], jnp.zeros((pad, d), F32)], axis=0)
    m = _modulation(cvecs, mod_w, mod_b)
    shift, scale, gate = (m[:, i * d:(i + 1) * d] for i in range(3))
    lat = lambda a: a[:b, None, :]
    cx = lambda a: jnp.broadcast_to(a[b:b + 1, None, :], (b, 1, d))

    w_na = w_in[:, :off_dn]
    w_dn = w_in[:, off_dn:off_ab]
    w_ab = jnp.pad(w_in[:, off_ab:off_z], ((0, 0), (0, LANES - 4 * DN_HEADS)))
    w_z = w_in[:, off_z:]
    qkv_x, dnx, abx, z_x = _norm_proj(
        x, norm_g, lat(shift), lat(scale),
        [(w_na, "bf16", BF16), (w_dn, "bf16", F32), (w_ab, "f32", F32), (w_z, "bf16", F32)],
        _row_tile(l, 512))
    qkv_c, dnc, abc = _norm_proj(
        ctx, norm_g, cx(shift), cx(scale),
        [(w_na, "bf16", BF16), (w_dn, "bf16", F32), (w_ab, "f32", F32)], _row_tile(lc, 256))

    na_x = _neighbourhood_attention(qkv_x, qkv_c, rpb)

    dn_c, gb_c = _dn_prepare(dnc, abc, dn_conv, a_log, dt_bias, _row_tile(lc, 256))
    dn_x, gb_x = _dn_prepare(dnx, abx, dn_conv, a_log, dt_bias, _row_tile(l, 512))
    s0 = jnp.zeros((b, 2 * DN_HEADS, DN_DK, DN_DK), F32)
    _, _, s_ctx = _dn_scan(dn_c, gb_c, s0)
    o_f, o_b, _ = _dn_scan(dn_x, gb_x, s_ctx)

    return _out0(x, na_x, o_f, o_b, z_x, lat(gate), dn_norm_g, w_out, _row_tile(l, 512))


def _hyena_layer(x, c, norm_g, mod_w, mod_b, w_in, conv_w, fw1, fb1, ff1, fw2, fb2, ff2, fw3, skip, w_out,
                 final_g):
    b, l, d = x.shape
    pad = (-b) % 8
    cvecs = jnp.concatenate([c, jnp.zeros((pad, d), F32)], axis=0)
    m = _modulation(cvecs, mod_w, mod_b)
    shift, scale, gate = (m[:b, None, i * d:(i + 1) * d] for i in range(3))
    q_n = l // HY_P
    xt = _time_major_to_pq(x).reshape(b, l, d)
    p_u, gz = _norm_proj(xt, norm_g, shift, scale,
                         [(w_in[:, :3 * HY_W], "bf16", F32), (w_in[:, 3 * HY_W:], "bf16", F32)],
                         _row_tile(l, 512))
    u = _conv3(p_u.reshape(b, HY_P, q_n, 3 * HY_W), conv_w, 2 * LANES)

    h_raw, ss = _hyena_filters_raw(l, fw1, fb1, ff1, fw2, fb2, ff2, fw3, _row_tile(l, 256))
    h = _filter_normalise(h_raw, ss, _row_tile(l, 256))
    tables = _fft_tables(l)
    resp = _filter_response(_filter_spectra(h.reshape(1, HY_P, q_n, h.shape[1]), tables), skip, q_n)

    nb = HY_W // LANES
    z = _long_conv_gated(u, 0, u, nb, resp, 0, tables, BF16)
    y = _long_conv_gated(z, 0, u, 2 * nb, resp, 1, tables, F32)
    out = _out1(xt, y.reshape(b, l, HY_W), gz, gate, w_out, final_g, _row_tile(l, 512))
    return _pq_to_time_major(out.reshape(b, HY_P, q_n, d))


def kernel(x, c, ctx, c_ctx, e_norm_g, e_mod_w, e_mod_b, e_w_in, e_na_rpb, e_dn_conv, e_dn_a_log, e_dn_dt_bias, e_dn_norm_g, e_w_out, o_norm_g, o_mod_w, o_mod_b, o_w_in, o_hy_conv, o_ffn_w1, o_ffn_b1, o_ffn_f1, o_ffn_w2, o_ffn_b2, o_ffn_f2, o_ffn_w3, o_hy_skip, o_w_out, final_norm_g):
    x = _even_layer(x, ctx, c, c_ctx, e_norm_g[0], e_mod_w[0], e_mod_b[0], e_w_in[0], e_na_rpb[0],
                    e_dn_conv[0], e_dn_a_log[0], e_dn_dt_bias[0], e_dn_norm_g[0], e_w_out[0])
    return _hyena_layer(x, c, o_norm_g[0], o_mod_w[0], o_mod_b[0], o_w_in[0], o_hy_conv[0],
                        o_ffn_w1[0], o_ffn_b1[0], o_ffn_f1[0], o_ffn_w2[0], o_ffn_b2[0], o_ffn_f2[0],
                        o_ffn_w3[0], o_hy_skip[0], o_w_out[0], final_norm_g)
```

```python
import functools
import math

import numpy as np
import jax
import jax.numpy as jnp
from jax import lax
from jax.experimental import pallas as pl
from jax.experimental.pallas import tpu as pltpu

F32 = jnp.float32
BF16 = jnp.bfloat16
HI = lax.Precision.HIGHEST
EPS = 1e-6
NEG = -1e30

LANES = 128
VMEM_LIMIT_BYTES = 56 * 1024 * 1024

GRID_W = 64
NA_HEADS = 8
NA_DH = 64
NA_W = NA_HEADS * NA_DH
NA_WIN_R = 8
NA_WIN_C = 16
NA_QROWS = 4
NA_KROWS = NA_QROWS + NA_WIN_R

DN_HEADS = 4
DN_DK = 128
DN_W = DN_HEADS * DN_DK
DN_CONV_W = 3 * DN_W
DN_CONV = 5
DN_CHUNK = 128

HY_W = 1024
HY_EMB = 33
HY_BANDS = (HY_EMB - 1) // 2
HY_DECAY_MIN = math.log(1e-2) / 1.5
HY_DECAY_MAX = math.log(1e-2) / 0.3


def _params(*sem):
    return pltpu.CompilerParams(dimension_semantics=sem, vmem_limit_bytes=VMEM_LIMIT_BYTES)


def _silu(v):
    return v * jax.nn.sigmoid(v)


def _dot(a, b):
    return jnp.dot(a.astype(BF16), b.astype(BF16), preferred_element_type=F32)


def _dot_nt(a, b):
    return lax.dot_general(a.astype(BF16), b.astype(BF16), (((1,), (1,)), ((), ())),
                           preferred_element_type=F32)


def _dot_tn(a, b):
    return lax.dot_general(a.astype(BF16), b.astype(BF16), (((0,), (0,)), ((), ())),
                           preferred_element_type=F32)


def _mod_kernel(c_ref, w_ref, b_ref, o_ref):
    o_ref[...] = jnp.dot(_silu(c_ref[...]), w_ref[...], precision=HI,
                         preferred_element_type=F32) + b_ref[...]


def _modulation(cvecs, w, b):
    r, d = cvecs.shape
    n = w.shape[1]
    tn = 512
    return pl.pallas_call(
        _mod_kernel,
        grid=(n // tn,),
        in_specs=[pl.BlockSpec((r, d), lambda j: (0, 0)),
                  pl.BlockSpec((d, tn), lambda j: (0, j)),
                  pl.BlockSpec((1, tn), lambda j: (0, j))],
        out_specs=pl.BlockSpec((r, tn), lambda j: (0, j)),
        out_shape=jax.ShapeDtypeStruct((r, n), F32),
        compiler_params=_params("parallel"),
        name="adaln_mod",
    )(cvecs, w, b.reshape(1, n))


def _proj_kernel(x_ref, g_ref, sh_ref, sc_ref, *refs, precs):
    n = len(precs)
    w_refs, o_refs = refs[:n], refs[n:]
    x = x_ref[0]
    ms = jnp.mean(x * x, axis=-1, keepdims=True)
    h = x * lax.rsqrt(ms + EPS) * g_ref[...] * (1.0 + sc_ref[0]) + sh_ref[0]
    hb = h.astype(BF16)
    for w_ref, o_ref, prec in zip(w_refs, o_refs, precs):
        if prec == "bf16":
            r = jnp.dot(hb, w_ref[...], preferred_element_type=F32)
        else:
            r = jnp.dot(h, w_ref[...], precision=HI, preferred_element_type=F32)
        o_ref[0] = r.astype(o_ref.dtype)


def _norm_proj(x, norm_g, shift, scale, sections, tm):
    b, l, d = x.shape
    ws, precs, out_shapes, out_specs, w_specs = [], [], [], [], []
    for w, prec, odt in sections:
        ws.append(w.astype(BF16) if prec == "bf16" else w)
        precs.append(prec)
        n = w.shape[1]
        w_specs.append(pl.BlockSpec((d, n), lambda bi, i: (0, 0)))
        out_specs.append(pl.BlockSpec((1, tm, n), lambda bi, i: (bi, i, 0)))
        out_shapes.append(jax.ShapeDtypeStruct((b, l, n), odt))
    vec = pl.BlockSpec((1, 1, d), lambda bi, i: (bi, 0, 0))
    return pl.pallas_call(
        functools.partial(_proj_kernel, precs=tuple(precs)),
        grid=(b, l // tm),
        in_specs=[pl.BlockSpec((1, tm, d), lambda bi, i: (bi, i, 0)),
                  pl.BlockSpec((1, d), lambda bi, i: (0, 0)), vec, vec] + w_specs,
        out_specs=out_specs,
        out_shape=out_shapes,
        compiler_params=_params("parallel", "parallel"),
        name="norm_proj",
    )(x, norm_g.reshape(1, d), shift, scale, *ws)


def _na_block_geometry(rows):
    return ((0, 0), (NA_QROWS, 0), (rows - NA_QROWS, rows - NA_KROWS))


def _na_bias_kernel(rc_ref, o_ref, *, rows):
    wr = min(NA_WIN_R, rows)
    qc = lax.broadcasted_iota(jnp.int32, (GRID_W, LANES), 0)
    lane = lax.broadcasted_iota(jnp.int32, (GRID_W, LANES), 1)
    kc = lane & (GRID_W - 1)
    c0 = jnp.clip(qc - NA_WIN_C // 2, 0, GRID_W - NA_WIN_C)
    col_ok = (kc >= c0) & (kc < c0 + NA_WIN_C)
    neg = jnp.full((GRID_W, LANES), NEG, F32)
    tiles = []
    for dr in range(2 * NA_WIN_R - 1):
        base = jnp.broadcast_to(rc_ref[0, dr:dr + 1, :], (GRID_W, LANES))
        tiles.append(jnp.where(col_ok, pltpu.roll(base, 0, 1, stride=1, stride_axis=0), neg))
    for g, (r_first, k_first) in enumerate(_na_block_geometry(rows)):
        for i in range(NA_QROWS):
            qr = r_first + i
            r0 = min(max(qr - wr // 2, 0), rows - wr)
            for jp in range(NA_KROWS // 2):
                halves = []
                for j in (2 * jp, 2 * jp + 1):
                    kr = k_first + j
                    halves.append(tiles[kr - qr + NA_WIN_R - 1] if r0 <= kr < r0 + wr else neg)
                o_ref[g, 0, i * GRID_W:(i + 1) * GRID_W, jp * LANES:(jp + 1) * LANES] = jnp.where(
                    lane < GRID_W, halves[0], halves[1])


def _na_bias_table(rpb, rows):
    wc = NA_WIN_C
    fill = jnp.full(rpb.shape[:2] + (GRID_W - (2 * wc - 1),), NEG, F32)
    ring = jnp.concatenate([rpb[..., wc - 1:], fill, rpb[..., :wc - 1]], axis=-1)
    ring = jnp.concatenate([ring, ring], axis=-1)
    nq, nk = NA_QROWS * GRID_W, NA_KROWS * GRID_W
    return pl.pallas_call(
        functools.partial(_na_bias_kernel, rows=rows),
        grid=(NA_HEADS,),
        in_specs=[pl.BlockSpec((1, 2 * NA_WIN_R - 1, LANES), lambda h: (h, 0, 0))],
        out_specs=pl.BlockSpec((3, 1, nq, nk), lambda h: (0, h, 0, 0)),
        out_shape=jax.ShapeDtypeStruct((3, NA_HEADS, nq, nk), F32),
        compiler_params=_params("parallel"),
        name="na_bias_table",
    )(ring)


def _na_kernel(q_ref, k_ref, v_ref, kc_ref, vc_ref, bias_ref, o_ref, *, rows):
    blk = pl.program_id(1)
    nq = NA_QROWS * GRID_W
    nk = NA_KROWS * GRID_W
    k_first = jnp.clip(blk * NA_QROWS - NA_WIN_R // 2, 0, rows - NA_KROWS)
    start = pl.multiple_of(k_first * GRID_W, GRID_W)
    lane = lax.broadcasted_iota(jnp.int32, (nq, LANES), 1)
    low = lane < NA_DH
    for p in range(NA_W // LANES):
        cs = slice(p * LANES, (p + 1) * LANES)
        q2 = q_ref[0, :, cs]
        k2 = k_ref[0, pl.ds(start, nk), cs]
        v2 = v_ref[0, pl.ds(start, nk), cs]
        kc2 = kc_ref[0, :, cs]
        vc2 = vc_ref[0, :, cs]
        halves = []
        for hh in range(2):
            sel = low if hh == 0 else jnp.logical_not(low)
            qh = jnp.where(sel, q2, jnp.zeros_like(q2))
            s_win = _dot_nt(qh, k2) * (NA_DH ** -0.5) + bias_ref[0, 2 * p + hh]
            s_ctx = _dot_nt(qh, kc2) * (NA_DH ** -0.5)
            m = jnp.maximum(jnp.max(s_win, axis=-1, keepdims=True),
                            jnp.max(s_ctx, axis=-1, keepdims=True))
            p_win = jnp.exp(s_win - m)
            p_ctx = jnp.exp(s_ctx - m)
            den = jnp.sum(p_win, axis=-1, keepdims=True) + jnp.sum(p_ctx, axis=-1, keepdims=True)
            halves.append((_dot(p_win, v2) + _dot(p_ctx, vc2)) / den)
        o_ref[0, :, cs] = jnp.where(low, halves[0], halves[1]).astype(o_ref.dtype)


def _neighbourhood_attention(qkv, qkv_c, rpb):
    b, l, _ = qkv.shape
    lc = qkv_c.shape[1]
    rows = l // GRID_W
    assert rows % NA_QROWS == 0 and rows >= NA_KROWS + 1
    nq = NA_QROWS * GRID_W
    nblk = rows // NA_QROWS
    bias = _na_bias_table(rpb, rows)

    def cfg(bi, i):
        return (jnp.where(i == 0, 0, jnp.where(i == nblk - 1, 2, 1)), 0, 0, 0)

    return pl.pallas_call(
        functools.partial(_na_kernel, rows=rows),
        grid=(b, nblk),
        in_specs=[pl.BlockSpec((1, nq, NA_W), lambda bi, i: (bi, i, 0)),
                  pl.BlockSpec((1, l, NA_W), lambda bi, i: (bi, 0, 1)),
                  pl.BlockSpec((1, l, NA_W), lambda bi, i: (bi, 0, 2)),
                  pl.BlockSpec((1, lc, NA_W), lambda bi, i: (bi, 0, 1)),
                  pl.BlockSpec((1, lc, NA_W), lambda bi, i: (bi, 0, 2)),
                  pl.BlockSpec((1, NA_HEADS, nq, NA_KROWS * GRID_W), cfg)],
        out_specs=pl.BlockSpec((1, nq, NA_W), lambda bi, i: (bi, i, 0)),
        out_shape=jax.ShapeDtypeStruct((b, l, NA_W), BF16),
        compiler_params=_params("parallel", "arbitrary"),
        name="neighbourhood_attention",
    )(qkv, qkv, qkv, qkv_c, qkv_c, bias)


HALO_ROWS = 16


def _halo_specs(tl, width, l):
    nbh = tl // HALO_ROWS
    last = l // HALO_ROWS - 1
    return [pl.BlockSpec((1, tl, width), lambda bi, i: (bi, i, 0)),
            pl.BlockSpec((1, HALO_ROWS, width), lambda bi, i: (bi, jnp.maximum(i * nbh - 1, 0), 0)),
            pl.BlockSpec((1, HALO_ROWS, width), lambda bi, i: (bi, jnp.minimum((i + 1) * nbh, last), 0))]


def _dwconv_tile(x_ref, xp_ref, xn_ref, cw_ref, taps):
    i = pl.program_id(1)
    x = x_ref[0].astype(F32)
    tl = x.shape[0]
    prev = jnp.where(i > 0, xp_ref[0].astype(F32), 0.0)
    nxt = jnp.where(i < pl.num_programs(1) - 1, xn_ref[0].astype(F32), 0.0)
    xe = jnp.concatenate([prev, x, nxt], axis=0)
    half = taps // 2
    acc = jnp.zeros_like(x)
    for j in range(taps):
        off = HALO_ROWS - half + j
        acc = acc + xe[off:off + tl] * cw_ref[j:j + 1, :]
    return acc


def _dnprep_kernel(x_ref, xp_ref, xn_ref, ab_ref, cw_ref, al_ref, dtb_ref, qkv_ref, gb_ref):
    t = _silu(_dwconv_tile(x_ref, xp_ref, xn_ref, cw_ref, DN_CONV))
    segs = []
    for hh in range(3 * DN_HEADS):
        seg = t[:, hh * DN_DK:(hh + 1) * DN_DK]
        if hh < 2 * DN_HEADS:
            seg = seg * lax.rsqrt(jnp.sum(seg * seg, axis=-1, keepdims=True) + EPS)
        if hh < DN_HEADS:
            seg = seg * (DN_DK ** -0.5)
        segs.append(seg)
    qkv_ref[0] = jnp.concatenate(segs, axis=1).astype(qkv_ref.dtype)
    ab = ab_ref[0]
    z = ab + dtb_ref[...]
    softplus = jnp.maximum(z, 0.0) + jnp.log(1.0 + jnp.exp(-jnp.abs(z)))
    g = -jnp.exp(al_ref[...]) * softplus
    lane = lax.broadcasted_iota(jnp.int32, ab.shape, 1)
    gb_ref[0] = jnp.where(lane < 2 * DN_HEADS, g, jax.nn.sigmoid(ab))


def _dn_prepare(qkv, ab, conv_w, a_log, dt_bias, tl):
    b, l, w = qkv.shape
    pad = LANES - 2 * DN_HEADS
    al = jnp.pad(a_log.reshape(1, 2 * DN_HEADS), ((0, 0), (0, pad)))
    dtb = jnp.pad(dt_bias.reshape(1, 2 * DN_HEADS), ((0, 0), (0, pad)))
    const = lambda shape: pl.BlockSpec(shape, lambda bi, i: (0, 0))
    return pl.pallas_call(
        _dnprep_kernel,
        grid=(b, l // tl),
        in_specs=_halo_specs(tl, w, l) + [pl.BlockSpec((1, tl, LANES), lambda bi, i: (bi, i, 0)),
                                         const((DN_CONV, w)), const((1, LANES)), const((1, LANES))],
        out_specs=[pl.BlockSpec((1, tl, w), lambda bi, i: (bi, i, 0)),
                   pl.BlockSpec((1, tl, LANES), lambda bi, i: (bi, i, 0))],
        out_shape=[jax.ShapeDtypeStruct((b, l, w), BF16), jax.ShapeDtypeStruct((b, l, LANES), F32)],
        compiler_params=_params("parallel", "parallel"),
        name="dn_prepare",
    )(qkv, qkv, qkv, ab, conv_w, al, dtb)


DN_INV_BASE = 16


def _unit_triangular_inverse(nil, eye, row, col):
    c = nil.shape[-1]
    sh = int(math.log2(DN_INV_BASE))
    diag = jnp.where((row >> sh) == (col >> sh), nil, 0.0)
    inv = eye - diag
    pw = diag
    for _ in range(sh - 1):
        pw = _bdot(pw, pw)
        inv = inv + _bdot(inv, pw)
    while (1 << sh) < c:
        off = jnp.where(((row >> (sh + 1)) == (col >> (sh + 1))) & ((row >> sh) != (col >> sh)), nil, 0.0)
        inv = inv - _bdot(inv, _bdot(off, inv))
        sh += 1
    return inv


def _bdot(a, b):
    return lax.dot_general(a.astype(BF16), b.astype(BF16), (((2,), (1,)), ((0,), (0,))),
                           preferred_element_type=F32)


def _bdot_nt(a, b):
    return lax.dot_general(a.astype(BF16), b.astype(BF16), (((2,), (2,)), ((0,), (0,))),
                           preferred_element_type=F32)


def _bdot_tn(a, b):
    return lax.dot_general(a.astype(BF16), b.astype(BF16), (((1,), (1,)), ((0,), (0,))),
                           preferred_element_type=F32)


DN_STEP_CHUNKS = 2


def _dn_scan_kernel(xf_ref, xb_ref, gf_ref, gbk_ref, s0_ref, of_ref, ob_ref, sfin_ref, s_ref, *, nsub):
    t = pl.program_id(1)
    c = DN_CHUNK
    nh = DN_HEADS
    nb = 2 * nh

    @pl.when(t == 0)
    def _():
        s_ref[...] = s0_ref[0]

    row = lax.broadcasted_iota(jnp.int32, (c, c), 0)
    col = lax.broadcasted_iota(jnp.int32, (c, c), 1)
    eye = (row == col).astype(F32)
    incl = jnp.stack(([row >= col] * nh + [row <= col] * nh) * nsub)
    strict = jnp.stack(([row > col] * nh + [row < col] * nh) * nsub)
    rowg = lax.broadcasted_iota(jnp.int32, (c, LANES), 0)

    def chunk_rows(d, sq):
        first = (sq if d == 0 else nsub - 1 - sq) * c
        return slice(first, first + c)

    qs, ks, vs, gcs, betas = [], [], [], [], []
    for sq in range(nsub):
        for d, (x_ref, g_ref) in enumerate(((xf_ref, gf_ref), (xb_ref, gbk_ref))):
            rs = chunk_rows(d, sq)
            gb = g_ref[0, rs, :]
            cum = gb
            step = 1
            while step < c:
                if d == 0:
                    cum = cum + jnp.where(rowg >= step, pltpu.roll(cum, step, 0), 0.0)
                else:
                    cum = cum + jnp.where(rowg < c - step, pltpu.roll(cum, c - step, 0), 0.0)
                step *= 2
            for h in range(nh):
                qs.append(x_ref[0, rs, h * DN_DK:(h + 1) * DN_DK].astype(F32))
                ks.append(x_ref[0, rs, DN_W + h * DN_DK:DN_W + (h + 1) * DN_DK].astype(F32))
                vs.append(x_ref[0, rs, 2 * DN_W + h * DN_DK:2 * DN_W + (h + 1) * DN_DK].astype(F32))
                ci = d * nh + h
                gcs.append(jnp.broadcast_to(cum[:, ci:ci + 1], (c, c)))
                betas.append(gb[:, 2 * nh + ci:2 * nh + ci + 1])
    q, k, v, gc_rows, beta = (jnp.stack(a) for a in (qs, ks, vs, gcs, betas))
    gc_cols = jnp.swapaxes(gc_rows, 1, 2)
    decay = jnp.where(incl, jnp.exp(jnp.where(incl, gc_rows - gc_cols, 0.0)), 0.0)
    kb = k * beta
    vb = v * beta
    kq_k = _bdot_nt(jnp.concatenate([kb, q], axis=1), k)
    nil = jnp.where(strict, kq_k[:, :c] * decay, 0.0)
    aqk = kq_k[:, c:] * decay
    egc = jnp.exp(gc_rows)
    inv = _unit_triangular_inverse(nil, eye, row, col)
    sol = _bdot(inv, jnp.concatenate([vb, kb * egc], axis=2))
    u, w = sol[:, :, :DN_DK], sol[:, :, DN_DK:]
    wq = jnp.concatenate([w, q * egc], axis=1)
    s = s_ref[...]
    for sq in range(nsub):
        e = slice(sq * nb, (sq + 1) * nb)
        gc = gc_rows[e]
        wq_s = _bdot(wq[e], s)
        v_new = u[e] - wq_s[:, :c]
        o = wq_s[:, c:] + _bdot(aqk[e], v_new)
        g_last = jnp.concatenate([gc[:nh, c - 1:c, :], gc[nh:, 0:1, :]], axis=0)
        s = s * jnp.exp(g_last) + _bdot_tn(k[e] * jnp.exp(g_last - gc), v_new)
        for h in range(nh):
            of_ref[0, chunk_rows(0, sq), h * DN_DK:(h + 1) * DN_DK] = o[h].astype(of_ref.dtype)
            ob_ref[0, chunk_rows(1, sq), h * DN_DK:(h + 1) * DN_DK] = o[nh + h].astype(ob_ref.dtype)
    s_ref[...] = s

    @pl.when(t == pl.num_programs(1) - 1)
    def _():
        sfin_ref[0] = s_ref[...]


def _dn_scan(qkv, gb, s0):
    b, l, w = qkv.shape
    nsub = DN_STEP_CHUNKS if l % (DN_STEP_CHUNKS * DN_CHUNK) == 0 else 1
    rows = nsub * DN_CHUNK
    n = l // rows
    fwd = lambda bi, t: (bi, t, 0)
    bwd = lambda bi, t: (bi, n - 1 - t, 0)
    state = pl.BlockSpec((1, 2 * DN_HEADS, DN_DK, DN_DK), lambda bi, t: (bi, 0, 0, 0))
    return pl.pallas_call(
        functools.partial(_dn_scan_kernel, nsub=nsub),
        grid=(b, n),
        in_specs=[pl.BlockSpec((1, rows, w), fwd), pl.BlockSpec((1, rows, w), bwd),
                  pl.BlockSpec((1, rows, LANES), fwd), pl.BlockSpec((1, rows, LANES), bwd),
                  state],
        out_specs=[pl.BlockSpec((1, rows, DN_W), fwd), pl.BlockSpec((1, rows, DN_W), bwd),
                   state],
        out_shape=[jax.ShapeDtypeStruct((b, l, DN_W), BF16), jax.ShapeDtypeStruct((b, l, DN_W), BF16),
                   jax.ShapeDtypeStruct((b, 2 * DN_HEADS, DN_DK, DN_DK), F32)],
        scratch_shapes=[pltpu.VMEM((2 * DN_HEADS, DN_DK, DN_DK), F32)],
        compiler_params=_params("parallel", "arbitrary"),
        name="dn_scan",
    )(qkv, qkv, gb, gb, s0)


def _out0_kernel(x_ref, na_ref, of_ref, ob_ref, z_ref, gate_ref, ng_ref, w_ref, o_ref):
    o = of_ref[0].astype(F32) + ob_ref[0].astype(F32)
    segs = [na_ref[0].astype(F32)]
    for h in range(DN_HEADS):
        seg = o[:, h * DN_DK:(h + 1) * DN_DK]
        ms = jnp.mean(seg * seg, axis=-1, keepdims=True)
        segs.append(seg * lax.rsqrt(ms + EPS) * ng_ref[...])
    mix = jnp.concatenate(segs, axis=1) * _silu(z_ref[0].astype(F32))
    o_ref[0] = x_ref[0] + gate_ref[0] * _dot(mix, w_ref[...])


def _out0(x, na, o_f, o_b, z, gate, norm_g, w_out, tm):
    b, l, d = x.shape
    tile = lambda n: pl.BlockSpec((1, tm, n), lambda bi, i: (bi, i, 0))
    return pl.pallas_call(
        _out0_kernel,
        grid=(b, l // tm),
        in_specs=[tile(d), tile(NA_W), tile(DN_W), tile(DN_W), tile(d),
                  pl.BlockSpec((1, 1, d), lambda bi, i: (bi, 0, 0)),
                  pl.BlockSpec((1, DN_DK), lambda bi, i: (0, 0)),
                  pl.BlockSpec(w_out.shape, lambda bi, i: (0, 0))],
        out_specs=tile(d),
        out_shape=jax.ShapeDtypeStruct((b, l, d), F32),
        compiler_params=_params("parallel", "parallel"),
        name="out_proj0",
    )(x, na, o_f, o_b, z, gate, norm_g.reshape(1, DN_DK), w_out.astype(BF16))


def _conv3_kernel(x_ref, cw_ref, o_ref):
    x = x_ref[0].astype(F32)
    p_n, q_n, cw = x.shape
    zero = jnp.zeros((1, 1, cw), x.dtype)
    first_prev = jnp.concatenate([zero, x[p_n - 1:p_n, :q_n - 1]], axis=1)
    last_next = jnp.concatenate([x[0:1, 1:], zero], axis=1)
    prev = jnp.concatenate([first_prev, x[:p_n - 1]], axis=0)
    nxt = jnp.concatenate([x[1:], last_next], axis=0)
    w = cw_ref[...]
    o_ref[0] = (prev * w[0:1, :] + x * w[1:2, :] + nxt * w[2:3, :]).astype(o_ref.dtype)


def _conv3(p_u, conv_w, cw):
    b, p_n, q_n, w = p_u.shape
    blk = pl.BlockSpec((1, p_n, q_n, cw), lambda bi, j: (bi, 0, 0, j))
    return pl.pallas_call(
        _conv3_kernel,
        grid=(b, w // cw),
        in_specs=[blk, pl.BlockSpec((3, cw), lambda bi, j: (0, j))],
        out_specs=blk,
        out_shape=jax.ShapeDtypeStruct(p_u.shape, BF16),
        compiler_params=_params("parallel", "parallel"),
        name="hyena_conv3",
    )(p_u, conv_w)


def _filter_kernel(feat_ref, env_ref, w1_ref, b1_ref, f1_ref, w2_ref, b2_ref, f2_ref, w3_ref,
                   h_ref, ss_ref):
    hid = jnp.sin(f1_ref[...] * (jnp.dot(feat_ref[...], w1_ref[...], precision=HI,
                                         preferred_element_type=F32) + b1_ref[...]))
    hid = jnp.sin(f2_ref[...] * (jnp.dot(hid, w2_ref[...], precision=HI,
                                         preferred_element_type=F32) + b2_ref[...]))
    h = jnp.dot(hid, w3_ref[...], precision=HI, preferred_element_type=F32)
    h = h * jnp.concatenate([env_ref[...]] * 4, axis=1)
    h_ref[...] = h

    @pl.when(pl.program_id(0) == 0)
    def _():
        ss_ref[...] = jnp.zeros_like(ss_ref)

    ss_ref[...] += jnp.sum(h * h, axis=0, keepdims=True)


def _hyena_filters_raw(length, w1, b1, f1, w2, b2, f2, w3, tl):
    t = jnp.linspace(0.0, 1.0, length, dtype=F32)[:, None]
    wv = 2.0 * math.pi * jnp.arange(length, dtype=F32)[:, None] / length
    f = jnp.linspace(1e-4, HY_BANDS - 1, HY_BANDS, dtype=F32)[None, :]
    feats = jnp.concatenate([t, jnp.cos(f * wv), -jnp.sin(f * wv)], axis=-1)
    decay = jnp.abs(jnp.linspace(HY_DECAY_MIN, HY_DECAY_MAX, HY_W, dtype=F32))
    env = jnp.exp(-t * decay)
    feats, env = _time_major_to_pq(feats[None])[0], _time_major_to_pq(env[None])[0]
    feats, env = feats.reshape(length, -1), env.reshape(length, -1)
    ffn = w1.shape[1]
    pe, pf = LANES - HY_EMB, LANES - ffn
    feats = jnp.pad(feats, ((0, 0), (0, pe)))
    w1p = jnp.pad(w1, ((0, pe), (0, pf)))
    w2p = jnp.pad(w2, ((0, pf), (0, pf)))
    w3p = jnp.pad(w3, ((0, pf), (0, 0)))
    vec = lambda a: jnp.pad(a.reshape(1, ffn), ((0, 0), (0, pf)))
    n_out = w3.shape[1]
    const = lambda shape: pl.BlockSpec(shape, lambda i: (0, 0))
    return pl.pallas_call(
        _filter_kernel,
        grid=(length // tl,),
        in_specs=[pl.BlockSpec((tl, LANES), lambda i: (i, 0)), pl.BlockSpec((tl, HY_W), lambda i: (i, 0)),
                  const((LANES, LANES)), const((1, LANES)), const((1, LANES)),
                  const((LANES, LANES)), const((1, LANES)), const((1, LANES)), const((LANES, n_out))],
        out_specs=[pl.BlockSpec((tl, n_out), lambda i: (i, 0)), const((1, n_out))],
        out_shape=[jax.ShapeDtypeStruct((length, n_out), F32), jax.ShapeDtypeStruct((1, n_out), F32)],
        compiler_params=_params("arbitrary"),
        name="hyena_filter_ffn",
    )(feats, env, w1p, vec(b1), vec(f1), w2p, vec(b2), vec(f2), w3p)


def _filter_norm_kernel(h_ref, ss_ref, o_ref):
    o_ref[...] = (h_ref[...] * lax.rsqrt(ss_ref[...] + EPS)).astype(o_ref.dtype)


def _filter_normalise(h_raw, ss, tl):
    length, n = h_raw.shape
    return pl.pallas_call(
        _filter_norm_kernel,
        grid=(length // tl,),
        in_specs=[pl.BlockSpec((tl, n), lambda i: (i, 0)), pl.BlockSpec((1, n), lambda i: (0, 0))],
        out_specs=pl.BlockSpec((tl, n), lambda i: (i, 0)),
        out_shape=jax.ShapeDtypeStruct((length, n), BF16),
        compiler_params=_params("parallel"),
        name="hyena_filter_norm",
    )(h_raw, ss)


HY_P = 64
HY_PAIR_UNROLL = 16
HY_SLAB_UNROLL = 24
HY_KQ_PAD = 8
HY_ROW_PAD = 8


def _time_major_to_pq(a):
    b, l, c = a.shape
    return a.reshape(b, l // HY_P, HY_P, c).transpose(0, 2, 1, 3)


def _pq_to_time_major(a):
    b, p, q, c = a.shape
    return a.transpose(0, 2, 1, 3).reshape(b, p * q, c)


def _fft_tables(length):
    p_n = HY_P
    q_n = length // p_n
    qn2 = 2 * q_n
    n = 2 * length
    kq_n = q_n + HY_KQ_PAD
    kq = np.arange(kq_n)
    live = (kq <= q_n)[:, None]
    a1 = 2.0 * np.pi * ((kq[:, None] * np.arange(q_n)[None, :]) % qn2) / qn2
    c1, s1 = np.cos(a1) * live, np.sin(a1) * live
    g1 = np.concatenate([c1, -s1], axis=0)
    g1i = np.concatenate([c1.T, -s1.T], axis=1)
    bf = lambda m: jnp.asarray(m, F32).astype(BF16)
    pp = jnp.arange(p_n, dtype=jnp.int32)
    k = jnp.arange(kq_n, dtype=jnp.int32)[:, None, None] + qn2 * pp[None, :, None]
    ang = ((k * pp[None, None, :]) % n).astype(F32) * (2.0 * math.pi / n)
    c2, s2 = jnp.cos(ang), jnp.sin(ang)
    f2t = jnp.concatenate([jnp.concatenate([c2, s2], axis=2), jnp.concatenate([-s2, c2], axis=2)], axis=1)
    c2t, s2t = jnp.swapaxes(c2, 1, 2), jnp.swapaxes(s2, 1, 2)
    f2it = jnp.concatenate([jnp.concatenate([c2t, -s2t], axis=2), jnp.concatenate([s2t, c2t], axis=2)], axis=1)
    return bf(g1), bf(g1i), f2t.astype(BF16), f2it.astype(BF16)


def _fft_forward(x_ref, g1_ref, sre_ref, sim_ref):
    p_n = x_ref.shape[1]
    kq_n = g1_ref.shape[0] // 2
    pitch = p_n + HY_ROW_PAD

    def body(i, carry):
        p = 2 * i
        pair = jnp.concatenate([x_ref[0, p], x_ref[0, p + 1]], axis=1).astype(BF16)
        a = jnp.dot(g1_ref[...], pair, preferred_element_type=F32)
        for off in range(2):
            lanes = slice(off * LANES, (off + 1) * LANES)
            sre_ref[pl.ds(p + off, kq_n, stride=pitch), :] = a[:kq_n, lanes]
            sim_ref[pl.ds(p + off, kq_n, stride=pitch), :] = a[kq_n:, lanes]
        return carry

    lax.fori_loop(0, p_n // 2, body, 0, unroll=HY_PAIR_UNROLL)


def _fft_slab_spectrum(kq, f2t_ref, sre_ref, sim_ref, p_n):
    base = pl.multiple_of(kq * (p_n + HY_ROW_PAD), 8)
    slab = jnp.concatenate([sre_ref[pl.ds(base, p_n), :], sim_ref[pl.ds(base, p_n), :]], axis=0)
    return base, jnp.dot(f2t_ref[kq], slab.astype(BF16), preferred_element_type=F32)


def _fftconv_kernel(x_ref, gate_ref, h_ref, g1_ref, g1i_ref, f2t_ref, f2it_ref, o_ref, sre_ref, sim_ref):
    p_n = x_ref.shape[1]
    kq_n = g1_ref.shape[0] // 2
    pitch = p_n + HY_ROW_PAD
    _fft_forward(x_ref, g1_ref, sre_ref, sim_ref)

    def stage2(kq, carry):
        base, x = _fft_slab_spectrum(kq, f2t_ref, sre_ref, sim_ref, p_n)
        xr, xi = x[:p_n], x[p_n:]
        h = h_ref[0, kq].astype(F32)
        hr, hi = h[:p_n], h[p_n:]
        y = jnp.concatenate([xr * hr - xi * hi, xr * hi + xi * hr], axis=0)
        z = jnp.dot(f2it_ref[kq], y.astype(BF16), preferred_element_type=F32)
        sre_ref[pl.ds(base, p_n), :] = z[:p_n]
        sim_ref[pl.ds(base, p_n), :] = z[p_n:]
        return carry

    lax.fori_loop(0, kq_n, stage2, 0, unroll=HY_SLAB_UNROLL)

    def stage3(i, carry):
        p = 2 * i
        pair = []
        for off in range(2):
            zr = sre_ref[pl.ds(p + off, kq_n, stride=pitch), :]
            zi = sim_ref[pl.ds(p + off, kq_n, stride=pitch), :]
            pair.append(jnp.concatenate([zr, zi], axis=0))
        y = jnp.dot(g1i_ref[...], jnp.concatenate(pair, axis=1).astype(BF16), preferred_element_type=F32)
        for off in range(2):
            o_ref[0, p + off] = (gate_ref[0, p + off].astype(F32)
                                 * y[:, off * LANES:(off + 1) * LANES]).astype(o_ref.dtype)
        return carry

    lax.fori_loop(0, p_n // 2, stage3, 0, unroll=HY_PAIR_UNROLL)


def _filter_response_kernel(hf_ref, hb_ref, skip_ref, g1_ref, f2t_ref, o_ref, sre_ref, sim_ref, spec_ref,
                            *, q_n, n):
    p_n = hf_ref.shape[1]
    kq_n = g1_ref.shape[0] // 2
    _fft_forward(hf_ref, g1_ref, sre_ref, sim_ref)

    def keep(kq, carry):
        _, x = _fft_slab_spectrum(kq, f2t_ref, sre_ref, sim_ref, p_n)
        spec_ref[kq] = x
        return carry

    lax.fori_loop(0, kq_n, keep, 0, unroll=HY_SLAB_UNROLL)
    _fft_forward(hb_ref, g1_ref, sre_ref, sim_ref)

    def combine(kq, carry):
        _, xb = _fft_slab_spectrum(kq, f2t_ref, sre_ref, sim_ref, p_n)
        xf = spec_ref[kq]
        coef = jnp.where((kq == 0) | (kq == q_n), 1.0 / n, 2.0 / n)
        hr = (xf[:p_n] + xb[:p_n] + skip_ref[0]) * coef
        hi = (xf[p_n:] - xb[p_n:]) * coef
        o_ref[0, kq] = jnp.concatenate([hr, hi], axis=0).astype(o_ref.dtype)
        return carry

    lax.fori_loop(0, kq_n, combine, 0, unroll=HY_SLAB_UNROLL)


def _fft_scratch(kq_n, p_n):
    rows = kq_n * (p_n + HY_ROW_PAD)
    return [pltpu.VMEM((rows, LANES), F32), pltpu.VMEM((rows, LANES), F32)]


def _const_spec(a):
    return pl.BlockSpec(a.shape, lambda *_: (0,) * a.ndim)


def _filter_response(h_pq, skip, tables):
    g1, _, f2t, _ = tables
    _, p_n, q_n, _ = h_pq.shape
    kq_n = g1.shape[0] // 2
    nb = HY_W // LANES
    taps = lambda d: pl.BlockSpec((1, p_n, q_n, LANES), lambda o, j: (0, 0, 0, (2 * o + d) * nb + j))
    return pl.pallas_call(
        functools.partial(_filter_response_kernel, q_n=q_n, n=2 * p_n * q_n),
        grid=(2, nb),
        in_specs=[taps(0), taps(1), pl.BlockSpec((1, 1, LANES), lambda o, j: (o, 0, j)),
                  _const_spec(g1), _const_spec(f2t)],
        out_specs=pl.BlockSpec((1, kq_n, 2 * p_n, LANES), lambda o, j: (o, 0, 0, j)),
        out_shape=jax.ShapeDtypeStruct((2, kq_n, 2 * p_n, HY_W), BF16),
        scratch_shapes=_fft_scratch(kq_n, p_n) + [pltpu.VMEM((kq_n, 2 * p_n, LANES), F32)],
        compiler_params=_params("parallel", "parallel"),
        name="hyena_filter_response",
    )(h_pq, h_pq, skip.reshape(2, 1, HY_W), g1, f2t)


def _long_conv_gated(u, u_block0, gate, gate_block0, resp, order, tables, out_dtype):
    g1, g1i, f2t, f2it = tables
    b, p_n, q_n, _ = u.shape
    kq_n = g1.shape[0] // 2
    blk = lambda off: pl.BlockSpec((1, p_n, q_n, LANES), lambda j, bi: (bi, 0, 0, off + j))
    return pl.pallas_call(
        _fftconv_kernel,
        grid=(HY_W // LANES, b),
        in_specs=[blk(u_block0), blk(gate_block0),
                  pl.BlockSpec((1, kq_n, 2 * p_n, LANES), lambda j, bi: (order, 0, 0, j)),
                  _const_spec(g1), _const_spec(g1i), _const_spec(f2t), _const_spec(f2it)],
        out_specs=blk(0),
        out_shape=jax.ShapeDtypeStruct((b, p_n, q_n, HY_W), out_dtype),
        scratch_shapes=_fft_scratch(kq_n, p_n),
        compiler_params=_params("parallel", "parallel"),
        name="hyena_fftconv",
    )(u, gate, resp, g1, g1i, f2t, f2it)


def _out1_kernel(x_ref, y_ref, z_ref, gate_ref, w_ref, fg_ref, o_ref):
    r = x_ref[0] + gate_ref[0] * _dot(y_ref[0].astype(F32) * _silu(z_ref[0].astype(F32)), w_ref[...])
    ms = jnp.mean(r * r, axis=-1, keepdims=True)
    o_ref[0] = r * lax.rsqrt(ms + EPS) * fg_ref[...]


def _out1(x, y, z, gate, w_out, final_g, tm):
    b, l, d = x.shape
    tile = pl.BlockSpec((1, tm, d), lambda bi, i: (bi, i, 0))
    return pl.pallas_call(
        _out1_kernel,
        grid=(b, l // tm),
        in_specs=[tile, tile, tile, pl.BlockSpec((1, 1, d), lambda bi, i: (bi, 0, 0)),
                  pl.BlockSpec(w_out.shape, lambda bi, i: (0, 0)), pl.BlockSpec((1, d), lambda bi, i: (0, 0))],
        out_specs=tile,
        out_shape=jax.ShapeDtypeStruct((b, l, d), F32),
        compiler_params=_params("parallel", "parallel"),
        name="out_proj1",
    )(x, y, z, gate, w_out.astype(BF16), final_g.reshape(1, d))


def _row_tile(l, want):
    return want if l % want == 0 else l


def _even_layer(x, ctx, c, c_ctx, norm_g, mod_w, mod_b, w_in, rpb, dn_conv, a_log, dt_bias, dn_norm_g, w_out):
    b, l, d = x.shape
    lc = ctx.shape[1]
    off_dn = 3 * NA_W
    off_ab = off_dn + DN_CONV_W
    off_z = off_ab + 4 * DN_HEADS
    rows = b + 1
    pad = (-rows) % 8
    cvecs = jnp.concatenate([c, c_ctx[None, :], jnp.zeros((pad, d), F32)], axis=0)
    m = _modulation(cvecs, mod_w, mod_b)
    shift, scale, gate = (m[:, i * d:(i + 1) * d] for i in range(3))
    lat = lambda a: a[:b, None, :]
    cx = lambda a: jnp.broadcast_to(a[b:b + 1, None, :], (b, 1, d))

    w_na = w_in[:, :off_dn]
    w_dn = w_in[:, off_dn:off_ab]
    w_ab = jnp.pad(w_in[:, off_ab:off_z], ((0, 0), (0, LANES - 4 * DN_HEADS)))
    w_z = w_in[:, off_z:]
    qkv_x, dnx, abx, z_x = _norm_proj(
        x, norm_g, lat(shift), lat(scale),
        [(w_na, "bf16", BF16), (w_dn, "bf16", BF16), (w_ab, "f32", F32), (w_z, "bf16", BF16)],
        _row_tile(l, 512))
    qkv_c, dnc, abc = _norm_proj(
        ctx, norm_g, cx(shift), cx(scale),
        [(w_na, "bf16", BF16), (w_dn, "bf16", BF16), (w_ab, "f32", F32)], _row_tile(lc, 256))

    na_x = _neighbourhood_attention(qkv_x, qkv_c, rpb)

    dn_c, gb_c = _dn_prepare(dnc, abc, dn_conv, a_log, dt_bias, _row_tile(lc, 256))
    dn_x, gb_x = _dn_prepare(dnx, abx, dn_conv, a_log, dt_bias, _row_tile(l, 512))
    s0 = jnp.zeros((b, 2 * DN_HEADS, DN_DK, DN_DK), F32)
    _, _, s_ctx = _dn_scan(dn_c, gb_c, s0)
    o_f, o_b, _ = _dn_scan(dn_x, gb_x, s_ctx)

    return _out0(x, na_x, o_f, o_b, z_x, lat(gate), dn_norm_g, w_out, _row_tile(l, 512))


def _hyena_layer(x, c, norm_g, mod_w, mod_b, w_in, conv_w, fw1, fb1, ff1, fw2, fb2, ff2, fw3, skip, w_out,
                 final_g):
    b, l, d = x.shape
    pad = (-b) % 8
    cvecs = jnp.concatenate([c, jnp.zeros((pad, d), F32)], axis=0)
    m = _modulation(cvecs, mod_w, mod_b)
    shift, scale, gate = (m[:b, None, i * d:(i + 1) * d] for i in range(3))
    q_n = l // HY_P
    xt = _time_major_to_pq(x).reshape(b, l, d)
    p_u, gz = _norm_proj(xt, norm_g, shift, scale,
                         [(w_in[:, :3 * HY_W], "bf16", BF16), (w_in[:, 3 * HY_W:], "bf16", BF16)],
                         _row_tile(l, 512))
    u = _conv3(p_u.reshape(b, HY_P, q_n, 3 * HY_W), conv_w, 2 * LANES)

    h_raw, ss = _hyena_filters_raw(l, fw1, fb1, ff1, fw2, fb2, ff2, fw3, _row_tile(l, 256))
    h = _filter_normalise(h_raw, ss, _row_tile(l, 256))
    tables = _fft_tables(l)
    resp = _filter_response(h.reshape(1, HY_P, q_n, h.shape[1]), skip, tables)

    nb = HY_W // LANES
    z = _long_conv_gated(u, 0, u, nb, resp, 0, tables, BF16)
    y = _long_conv_gated(z, 0, u, 2 * nb, resp, 1, tables, BF16)
    out = _out1(xt, y.reshape(b, l, HY_W), gz, gate, w_out, final_g, _row_tile(l, 512))
    return _pq_to_time_major(out.reshape(b, HY_P, q_n, d))


def kernel(x, c, ctx, c_ctx, e_norm_g, e_mod_w, e_mod_b, e_w_in, e_na_rpb, e_dn_conv, e_dn_a_log, e_dn_dt_bias, e_dn_norm_g, e_w_out, o_norm_g, o_mod_w, o_mod_b, o_w_in, o_hy_conv, o_ffn_w1, o_ffn_b1, o_ffn_f1, o_ffn_w2, o_ffn_b2, o_ffn_f2, o_ffn_w3, o_hy_skip, o_w_out, final_norm_g):
    x = _even_layer(x, ctx, c, c_ctx, e_norm_g[0], e_mod_w[0], e_mod_b[0], e_w_in[0], e_na_rpb[0],
                    e_dn_conv[0], e_dn_a_log[0], e_dn_dt_bias[0], e_dn_norm_g[0], e_w_out[0])
    return _hyena_layer(x, c, o_norm_g[0], o_mod_w[0], o_mod_b[0], o_w_in[0], o_hy_conv[0],
                        o_ffn_w1[0], o_ffn_b1[0], o_ffn_f1[0], o_ffn_w2[0], o_ffn_b2[0], o_ffn_f2[0],
                        o_ffn_w3[0], o_hy_skip[0], o_w_out[0], final_norm_g)
```

```python
import functools
import math

import numpy as np
import jax
import jax.numpy as jnp
from jax import lax
from jax.experimental import pallas as pl
from jax.experimental.pallas import tpu as pltpu

F32 = jnp.float32
BF16 = jnp.bfloat16
HI = lax.Precision.HIGHEST
EPS = 1e-6
NEG = -1e30
LOG2E = math.log2(math.e)

LANES = 128
VMEM_LIMIT_BYTES = 56 * 1024 * 1024

GRID_W = 64
NA_HEADS = 8
NA_DH = 64
NA_W = NA_HEADS * NA_DH
NA_WIN_R = 8
NA_WIN_C = 16
NA_QROWS = 4
NA_KROWS = NA_QROWS + NA_WIN_R

DN_HEADS = 4
DN_DK = 128
DN_W = DN_HEADS * DN_DK
DN_CONV_W = 3 * DN_W
DN_CONV = 5
DN_CHUNK = 128

HY_W = 1024
HY_EMB = 33
HY_BANDS = (HY_EMB - 1) // 2
HY_DECAY_MIN = math.log(1e-2) / 1.5
HY_DECAY_MAX = math.log(1e-2) / 0.3


def _params(*sem):
    return pltpu.CompilerParams(dimension_semantics=sem, vmem_limit_bytes=VMEM_LIMIT_BYTES)


def _silu(v):
    return v * jax.nn.sigmoid(v)


def _dot(a, b):
    return jnp.dot(a.astype(BF16), b.astype(BF16), preferred_element_type=F32)


def _dot_nt(a, b):
    return lax.dot_general(a.astype(BF16), b.astype(BF16), (((1,), (1,)), ((), ())),
                           preferred_element_type=F32)


def _dot_tn(a, b):
    return lax.dot_general(a.astype(BF16), b.astype(BF16), (((0,), (0,)), ((), ())),
                           preferred_element_type=F32)


def _mod_kernel(c_ref, w_ref, b_ref, o_ref):
    o_ref[...] = jnp.dot(_silu(c_ref[...]), w_ref[...], precision=HI,
                         preferred_element_type=F32) + b_ref[...]


def _modulation(cvecs, w, b):
    r, d = cvecs.shape
    n = w.shape[1]
    tn = 512
    return pl.pallas_call(
        _mod_kernel,
        grid=(n // tn,),
        in_specs=[pl.BlockSpec((r, d), lambda j: (0, 0)),
                  pl.BlockSpec((d, tn), lambda j: (0, j)),
                  pl.BlockSpec((1, tn), lambda j: (0, j))],
        out_specs=pl.BlockSpec((r, tn), lambda j: (0, j)),
        out_shape=jax.ShapeDtypeStruct((r, n), F32),
        compiler_params=_params("parallel"),
        name="adaln_mod",
    )(cvecs, w, b.reshape(1, n))


def _proj_kernel(x_ref, g_ref, sh_ref, sc_ref, *refs, precs):
    n = len(precs)
    w_refs, o_refs = refs[:n], refs[n:]
    x = x_ref[0]
    ms = jnp.mean(x * x, axis=-1, keepdims=True)
    h = x * lax.rsqrt(ms + EPS) * g_ref[...] * (1.0 + sc_ref[0]) + sh_ref[0]
    hb = h.astype(BF16)
    for w_ref, o_ref, prec in zip(w_refs, o_refs, precs):
        if prec == "bf16":
            r = jnp.dot(hb, w_ref[...], preferred_element_type=F32)
        else:
            r = jnp.dot(h, w_ref[...], precision=HI, preferred_element_type=F32)
        o_ref[0] = r.astype(o_ref.dtype)


def _norm_proj(x, norm_g, shift, scale, sections, tm):
    b, l, d = x.shape
    ws, precs, out_shapes, out_specs, w_specs = [], [], [], [], []
    for w, prec, odt in sections:
        ws.append(w.astype(BF16) if prec == "bf16" else w)
        precs.append(prec)
        n = w.shape[1]
        w_specs.append(pl.BlockSpec((d, n), lambda bi, i: (0, 0)))
        out_specs.append(pl.BlockSpec((1, tm, n), lambda bi, i: (bi, i, 0)))
        out_shapes.append(jax.ShapeDtypeStruct((b, l, n), odt))
    vec = pl.BlockSpec((1, 1, d), lambda bi, i: (bi, 0, 0))
    return pl.pallas_call(
        functools.partial(_proj_kernel, precs=tuple(precs)),
        grid=(b, l // tm),
        in_specs=[pl.BlockSpec((1, tm, d), lambda bi, i: (bi, i, 0)),
                  pl.BlockSpec((1, d), lambda bi, i: (0, 0)), vec, vec] + w_specs,
        out_specs=out_specs,
        out_shape=out_shapes,
        compiler_params=_params("parallel", "parallel"),
        name="norm_proj",
    )(x, norm_g.reshape(1, d), shift, scale, *ws)


def _na_block_geometry(rows):
    return ((0, 0), (NA_QROWS, 0), (rows - NA_QROWS, rows - NA_KROWS))


def _na_bias_kernel(rc_ref, o_ref, *, rows):
    wr = min(NA_WIN_R, rows)
    qc = lax.broadcasted_iota(jnp.int32, (GRID_W, LANES), 0)
    lane = lax.broadcasted_iota(jnp.int32, (GRID_W, LANES), 1)
    kc = lane & (GRID_W - 1)
    c0 = jnp.clip(qc - NA_WIN_C // 2, 0, GRID_W - NA_WIN_C)
    col_ok = (kc >= c0) & (kc < c0 + NA_WIN_C)
    neg = jnp.full((GRID_W, LANES), NEG, F32)
    tiles = []
    for dr in range(2 * NA_WIN_R - 1):
        base = jnp.broadcast_to(rc_ref[0, dr:dr + 1, :], (GRID_W, LANES))
        tiles.append(jnp.where(col_ok, pltpu.roll(base, 0, 1, stride=1, stride_axis=0) * LOG2E, neg))
    for g, (r_first, k_first) in enumerate(_na_block_geometry(rows)):
        for i in range(NA_QROWS):
            qr = r_first + i
            r0 = min(max(qr - wr // 2, 0), rows - wr)
            for jp in range(NA_KROWS // 2):
                halves = []
                for j in (2 * jp, 2 * jp + 1):
                    kr = k_first + j
                    halves.append(tiles[kr - qr + NA_WIN_R - 1] if r0 <= kr < r0 + wr else neg)
                o_ref[g, 0, i * GRID_W:(i + 1) * GRID_W, jp * LANES:(jp + 1) * LANES] = jnp.where(
                    lane < GRID_W, halves[0], halves[1])


def _na_bias_table(rpb, rows):
    wc = NA_WIN_C
    fill = jnp.full(rpb.shape[:2] + (GRID_W - (2 * wc - 1),), NEG, F32)
    ring = jnp.concatenate([rpb[..., wc - 1:], fill, rpb[..., :wc - 1]], axis=-1)
    ring = jnp.concatenate([ring, ring], axis=-1)
    nq, nk = NA_QROWS * GRID_W, NA_KROWS * GRID_W
    return pl.pallas_call(
        functools.partial(_na_bias_kernel, rows=rows),
        grid=(NA_HEADS,),
        in_specs=[pl.BlockSpec((1, 2 * NA_WIN_R - 1, LANES), lambda h: (h, 0, 0))],
        out_specs=pl.BlockSpec((3, 1, nq, nk), lambda h: (0, h, 0, 0)),
        out_shape=jax.ShapeDtypeStruct((3, NA_HEADS, nq, nk), F32),
        compiler_params=_params("parallel"),
        name="na_bias_table",
    )(ring)


def _na_kernel(q_ref, k_ref, v_ref, kc_ref, vc_ref, bias_ref, o_ref, *, rows):
    blk = pl.program_id(1)
    nq = NA_QROWS * GRID_W
    nk = NA_KROWS * GRID_W
    k_first = jnp.clip(blk * NA_QROWS - NA_WIN_R // 2, 0, rows - NA_KROWS)
    start = pl.multiple_of(k_first * GRID_W, GRID_W)
    lane = lax.broadcasted_iota(jnp.int32, (nq, LANES), 1)
    low = lane < NA_DH
    for p in range(NA_W // LANES):
        cs = slice(p * LANES, (p + 1) * LANES)
        q2 = q_ref[0, :, cs]
        k2 = k_ref[0, pl.ds(start, nk), cs]
        v2 = v_ref[0, pl.ds(start, nk), cs]
        kc2 = kc_ref[0, :, cs]
        vc2 = vc_ref[0, :, cs]
        halves = []
        for hh in range(2):
            sel = low if hh == 0 else jnp.logical_not(low)
            qh = jnp.where(sel, q2, jnp.zeros_like(q2))
            s_win = _dot_nt(qh, k2) + bias_ref[0, 2 * p + hh]
            s_ctx = _dot_nt(qh, kc2)
            m = jnp.maximum(jnp.max(s_win, axis=-1, keepdims=True),
                            jnp.max(s_ctx, axis=-1, keepdims=True))
            p_win = jnp.exp2(s_win - m)
            p_ctx = jnp.exp2(s_ctx - m)
            den = jnp.sum(p_win, axis=-1, keepdims=True) + jnp.sum(p_ctx, axis=-1, keepdims=True)
            halves.append((_dot(p_win, v2) + _dot(p_ctx, vc2)) / den)
        o_ref[0, :, cs] = jnp.where(low, halves[0], halves[1]).astype(o_ref.dtype)


def _neighbourhood_attention(qkv, qkv_c, rpb):
    b, l, _ = qkv.shape
    lc = qkv_c.shape[1]
    rows = l // GRID_W
    assert rows % NA_QROWS == 0 and rows >= NA_KROWS + 1
    nq = NA_QROWS * GRID_W
    nblk = rows // NA_QROWS
    bias = _na_bias_table(rpb, rows)

    def cfg(bi, i):
        return (jnp.where(i == 0, 0, jnp.where(i == nblk - 1, 2, 1)), 0, 0, 0)

    return pl.pallas_call(
        functools.partial(_na_kernel, rows=rows),
        grid=(b, nblk),
        in_specs=[pl.BlockSpec((1, nq, NA_W), lambda bi, i: (bi, i, 0)),
                  pl.BlockSpec((1, l, NA_W), lambda bi, i: (bi, 0, 1)),
                  pl.BlockSpec((1, l, NA_W), lambda bi, i: (bi, 0, 2)),
                  pl.BlockSpec((1, lc, NA_W), lambda bi, i: (bi, 0, 1)),
                  pl.BlockSpec((1, lc, NA_W), lambda bi, i: (bi, 0, 2)),
                  pl.BlockSpec((1, NA_HEADS, nq, NA_KROWS * GRID_W), cfg)],
        out_specs=pl.BlockSpec((1, nq, NA_W), lambda bi, i: (bi, i, 0)),
        out_shape=jax.ShapeDtypeStruct((b, l, NA_W), BF16),
        compiler_params=_params("parallel", "arbitrary"),
        name="neighbourhood_attention",
    )(qkv, qkv, qkv, qkv_c, qkv_c, bias)


HALO_ROWS = 16


def _halo_specs(tl, width, l):
    nbh = tl // HALO_ROWS
    last = l // HALO_ROWS - 1
    return [pl.BlockSpec((1, tl, width), lambda bi, i: (bi, i, 0)),
            pl.BlockSpec((1, HALO_ROWS, width), lambda bi, i: (bi, jnp.maximum(i * nbh - 1, 0), 0)),
            pl.BlockSpec((1, HALO_ROWS, width), lambda bi, i: (bi, jnp.minimum((i + 1) * nbh, last), 0))]


def _dwconv_tile(x_ref, xp_ref, xn_ref, cw_ref, taps):
    i = pl.program_id(1)
    x = x_ref[0].astype(F32)
    tl = x.shape[0]
    prev = jnp.where(i > 0, xp_ref[0].astype(F32), 0.0)
    nxt = jnp.where(i < pl.num_programs(1) - 1, xn_ref[0].astype(F32), 0.0)
    xe = jnp.concatenate([prev, x, nxt], axis=0)
    half = taps // 2
    acc = jnp.zeros_like(x)
    for j in range(taps):
        off = HALO_ROWS - half + j
        acc = acc + xe[off:off + tl] * cw_ref[j:j + 1, :]
    return acc


def _dnprep_kernel(x_ref, xp_ref, xn_ref, ab_ref, cw_ref, al_ref, dtb_ref, qkv_ref, gb_ref):
    t = _silu(_dwconv_tile(x_ref, xp_ref, xn_ref, cw_ref, DN_CONV))
    segs = []
    for hh in range(3 * DN_HEADS):
        seg = t[:, hh * DN_DK:(hh + 1) * DN_DK]
        if hh < 2 * DN_HEADS:
            seg = seg * lax.rsqrt(jnp.sum(seg * seg, axis=-1, keepdims=True) + EPS)
        if hh < DN_HEADS:
            seg = seg * (DN_DK ** -0.5)
        segs.append(seg)
    qkv_ref[0] = jnp.concatenate(segs, axis=1).astype(qkv_ref.dtype)
    ab = ab_ref[0]
    z = ab + dtb_ref[...]
    softplus = jnp.maximum(z, 0.0) + jnp.log(1.0 + jnp.exp(-jnp.abs(z)))
    g = -jnp.exp(al_ref[...]) * softplus
    lane = lax.broadcasted_iota(jnp.int32, ab.shape, 1)
    gb_ref[0] = jnp.where(lane < 2 * DN_HEADS, g, jax.nn.sigmoid(ab))


def _dn_prepare(qkv, ab, conv_w, a_log, dt_bias, tl):
    b, l, w = qkv.shape
    pad = LANES - 2 * DN_HEADS
    al = jnp.pad(a_log.reshape(1, 2 * DN_HEADS), ((0, 0), (0, pad)))
    dtb = jnp.pad(dt_bias.reshape(1, 2 * DN_HEADS), ((0, 0), (0, pad)))
    const = lambda shape: pl.BlockSpec(shape, lambda bi, i: (0, 0))
    return pl.pallas_call(
        _dnprep_kernel,
        grid=(b, l // tl),
        in_specs=_halo_specs(tl, w, l) + [pl.BlockSpec((1, tl, LANES), lambda bi, i: (bi, i, 0)),
                                         const((DN_CONV, w)), const((1, LANES)), const((1, LANES))],
        out_specs=[pl.BlockSpec((1, tl, w), lambda bi, i: (bi, i, 0)),
                   pl.BlockSpec((1, tl, LANES), lambda bi, i: (bi, i, 0))],
        out_shape=[jax.ShapeDtypeStruct((b, l, w), BF16), jax.ShapeDtypeStruct((b, l, LANES), F32)],
        compiler_params=_params("parallel", "parallel"),
        name="dn_prepare",
    )(qkv, qkv, qkv, ab, conv_w, al, dtb)


DN_INV_BASE = 16


def _unit_triangular_inverse(nil, eye, row, col):
    c = nil.shape[-1]
    sh = int(math.log2(DN_INV_BASE))
    diag = jnp.where((row >> sh) == (col >> sh), nil, 0.0)
    inv = eye - diag
    pw = diag
    for _ in range(sh - 1):
        pw = _bdot(pw, pw)
        inv = inv + _bdot(inv, pw)
    while (1 << sh) < c:
        off = jnp.where(((row >> (sh + 1)) == (col >> (sh + 1))) & ((row >> sh) != (col >> sh)), nil, 0.0)
        inv = inv - _bdot(inv, _bdot(off, inv))
        sh += 1
    return inv


def _bdot(a, b):
    return lax.dot_general(a.astype(BF16), b.astype(BF16), (((2,), (1,)), ((0,), (0,))),
                           preferred_element_type=F32)


def _bdot_nt(a, b):
    return lax.dot_general(a.astype(BF16), b.astype(BF16), (((2,), (2,)), ((0,), (0,))),
                           preferred_element_type=F32)


def _bdot_tn(a, b):
    return lax.dot_general(a.astype(BF16), b.astype(BF16), (((1,), (1,)), ((0,), (0,))),
                           preferred_element_type=F32)


DN_STEP_CHUNKS = 2


def _dn_scan_kernel(xf_ref, xb_ref, gf_ref, gbk_ref, s0_ref, of_ref, ob_ref, sfin_ref, s_ref, *, nsub):
    t = pl.program_id(1)
    c = DN_CHUNK
    nh = DN_HEADS
    nb = 2 * nh

    @pl.when(t == 0)
    def _():
        s_ref[...] = s0_ref[0]

    row = lax.broadcasted_iota(jnp.int32, (c, c), 0)
    col = lax.broadcasted_iota(jnp.int32, (c, c), 1)
    eye = (row == col).astype(F32)
    incl = jnp.stack(([row >= col] * nh + [row <= col] * nh) * nsub)
    strict = jnp.stack(([row > col] * nh + [row < col] * nh) * nsub)
    rowg = lax.broadcasted_iota(jnp.int32, (c, LANES), 0)

    def chunk_rows(d, sq):
        first = (sq if d == 0 else nsub - 1 - sq) * c
        return slice(first, first + c)

    qs, ks, vs, gcs, betas = [], [], [], [], []
    for sq in range(nsub):
        for d, (x_ref, g_ref) in enumerate(((xf_ref, gf_ref), (xb_ref, gbk_ref))):
            rs = chunk_rows(d, sq)
            gb = g_ref[0, rs, :]
            cum = gb
            step = 1
            while step < c:
                if d == 0:
                    cum = cum + jnp.where(rowg >= step, pltpu.roll(cum, step, 0), 0.0)
                else:
                    cum = cum + jnp.where(rowg < c - step, pltpu.roll(cum, c - step, 0), 0.0)
                step *= 2
            for h in range(nh):
                qs.append(x_ref[0, rs, h * DN_DK:(h + 1) * DN_DK].astype(F32))
                ks.append(x_ref[0, rs, DN_W + h * DN_DK:DN_W + (h + 1) * DN_DK].astype(F32))
                vs.append(x_ref[0, rs, 2 * DN_W + h * DN_DK:2 * DN_W + (h + 1) * DN_DK].astype(F32))
                ci = d * nh + h
                gcs.append(jnp.broadcast_to(cum[:, ci:ci + 1], (c, c)))
                betas.append(gb[:, 2 * nh + ci:2 * nh + ci + 1])
    q, k, v, gc_rows, beta = (jnp.stack(a) for a in (qs, ks, vs, gcs, betas))
    gc_cols = jnp.swapaxes(gc_rows, 1, 2)
    decay = jnp.where(incl, jnp.exp(jnp.where(incl, gc_rows - gc_cols, 0.0)), 0.0)
    kb = k * beta
    vb = v * beta
    kq_k = _bdot_nt(jnp.concatenate([kb, q], axis=1), k)
    nil = jnp.where(strict, kq_k[:, :c] * decay, 0.0)
    aqk = kq_k[:, c:] * decay
    egc = jnp.exp(gc_rows)
    inv = _unit_triangular_inverse(nil, eye, row, col)
    sol = _bdot(inv, jnp.concatenate([vb, kb * egc], axis=2))
    u, w = sol[:, :, :DN_DK], sol[:, :, DN_DK:]
    wq = jnp.concatenate([w, q * egc], axis=1)
    s = s_ref[...]
    for sq in range(nsub):
        e = slice(sq * nb, (sq + 1) * nb)
        gc = gc_rows[e]
        wq_s = _bdot(wq[e], s)
        v_new = u[e] - wq_s[:, :c]
        o = wq_s[:, c:] + _bdot(aqk[e], v_new)
        g_last = jnp.concatenate([gc[:nh, c - 1:c, :], gc[nh:, 0:1, :]], axis=0)
        s = s * jnp.exp(g_last) + _bdot_tn(k[e] * jnp.exp(g_last - gc), v_new)
        for h in range(nh):
            of_ref[0, chunk_rows(0, sq), h * DN_DK:(h + 1) * DN_DK] = o[h].astype(of_ref.dtype)
            ob_ref[0, chunk_rows(1, sq), h * DN_DK:(h + 1) * DN_DK] = o[nh + h].astype(ob_ref.dtype)
    s_ref[...] = s

    @pl.when(t == pl.num_programs(1) - 1)
    def _():
        sfin_ref[0] = s_ref[...]


def _dn_scan(qkv, gb, s0):
    b, l, w = qkv.shape
    nsub = DN_STEP_CHUNKS if l % (DN_STEP_CHUNKS * DN_CHUNK) == 0 else 1
    rows = nsub * DN_CHUNK
    n = l // rows
    fwd = lambda bi, t: (bi, t, 0)
    bwd = lambda bi, t: (bi, n - 1 - t, 0)
    state = pl.BlockSpec((1, 2 * DN_HEADS, DN_DK, DN_DK), lambda bi, t: (bi, 0, 0, 0))
    return pl.pallas_call(
        functools.partial(_dn_scan_kernel, nsub=nsub),
        grid=(b, n),
        in_specs=[pl.BlockSpec((1, rows, w), fwd), pl.BlockSpec((1, rows, w), bwd),
                  pl.BlockSpec((1, rows, LANES), fwd), pl.BlockSpec((1, rows, LANES), bwd),
                  state],
        out_specs=[pl.BlockSpec((1, rows, DN_W), fwd), pl.BlockSpec((1, rows, DN_W), bwd),
                   state],
        out_shape=[jax.ShapeDtypeStruct((b, l, DN_W), BF16), jax.ShapeDtypeStruct((b, l, DN_W), BF16),
                   jax.ShapeDtypeStruct((b, 2 * DN_HEADS, DN_DK, DN_DK), F32)],
        scratch_shapes=[pltpu.VMEM((2 * DN_HEADS, DN_DK, DN_DK), F32)],
        compiler_params=_params("parallel", "arbitrary"),
        name="dn_scan",
    )(qkv, qkv, gb, gb, s0)


def _out0_kernel(x_ref, na_ref, of_ref, ob_ref, z_ref, gate_ref, ng_ref, w_ref, o_ref):
    o = of_ref[0].astype(F32) + ob_ref[0].astype(F32)
    segs = [na_ref[0].astype(F32)]
    for h in range(DN_HEADS):
        seg = o[:, h * DN_DK:(h + 1) * DN_DK]
        ms = jnp.mean(seg * seg, axis=-1, keepdims=True)
        segs.append(seg * lax.rsqrt(ms + EPS) * ng_ref[...])
    mix = jnp.concatenate(segs, axis=1) * _silu(z_ref[0].astype(F32))
    o_ref[0] = x_ref[0] + gate_ref[0] * _dot(mix, w_ref[...])


def _out0(x, na, o_f, o_b, z, gate, norm_g, w_out, tm):
    b, l, d = x.shape
    tile = lambda n: pl.BlockSpec((1, tm, n), lambda bi, i: (bi, i, 0))
    return pl.pallas_call(
        _out0_kernel,
        grid=(b, l // tm),
        in_specs=[tile(d), tile(NA_W), tile(DN_W), tile(DN_W), tile(d),
                  pl.BlockSpec((1, 1, d), lambda bi, i: (bi, 0, 0)),
                  pl.BlockSpec((1, DN_DK), lambda bi, i: (0, 0)),
                  pl.BlockSpec(w_out.shape, lambda bi, i: (0, 0))],
        out_specs=tile(d),
        out_shape=jax.ShapeDtypeStruct((b, l, d), F32),
        compiler_params=_params("parallel", "parallel"),
        name="out_proj0",
    )(x, na, o_f, o_b, z, gate, norm_g.reshape(1, DN_DK), w_out.astype(BF16))


def _filter_kernel(feat_ref, env_ref, w1_ref, b1_ref, f1_ref, w2_ref, b2_ref, f2_ref, w3_ref,
                   h_ref, ss_ref):
    hid = jnp.sin(f1_ref[...] * (jnp.dot(feat_ref[...], w1_ref[...], precision=HI,
                                         preferred_element_type=F32) + b1_ref[...]))
    hid = jnp.sin(f2_ref[...] * (jnp.dot(hid, w2_ref[...], precision=HI,
                                         preferred_element_type=F32) + b2_ref[...]))
    h = jnp.dot(hid, w3_ref[...], precision=HI, preferred_element_type=F32)
    h = h * jnp.concatenate([env_ref[...]] * 4, axis=1)
    h_ref[...] = h.astype(h_ref.dtype)

    @pl.when(pl.program_id(0) == 0)
    def _():
        ss_ref[...] = jnp.zeros_like(ss_ref)

    ss_ref[...] += jnp.sum(h * h, axis=0, keepdims=True)


def _hyena_filters_raw(length, w1, b1, f1, w2, b2, f2, w3, tl):
    t = jnp.linspace(0.0, 1.0, length, dtype=F32)[:, None]
    wv = 2.0 * math.pi * jnp.arange(length, dtype=F32)[:, None] / length
    f = jnp.linspace(1e-4, HY_BANDS - 1, HY_BANDS, dtype=F32)[None, :]
    feats = jnp.concatenate([t, jnp.cos(f * wv), -jnp.sin(f * wv)], axis=-1)
    decay = jnp.abs(jnp.linspace(HY_DECAY_MIN, HY_DECAY_MAX, HY_W, dtype=F32))
    env = jnp.exp(-t * decay)
    feats, env = _time_major_to_pq(feats[None])[0], _time_major_to_pq(env[None])[0]
    feats, env = feats.reshape(length, -1), env.reshape(length, -1)
    ffn = w1.shape[1]
    pe, pf = LANES - HY_EMB, LANES - ffn
    feats = jnp.pad(feats, ((0, 0), (0, pe)))
    w1p = jnp.pad(w1, ((0, pe), (0, pf)))
    w2p = jnp.pad(w2, ((0, pf), (0, pf)))
    w3p = jnp.pad(w3, ((0, pf), (0, 0)))
    vec = lambda a: jnp.pad(a.reshape(1, ffn), ((0, 0), (0, pf)))
    n_out = w3.shape[1]
    const = lambda shape: pl.BlockSpec(shape, lambda i: (0, 0))
    return pl.pallas_call(
        _filter_kernel,
        grid=(length // tl,),
        in_specs=[pl.BlockSpec((tl, LANES), lambda i: (i, 0)), pl.BlockSpec((tl, HY_W), lambda i: (i, 0)),
                  const((LANES, LANES)), const((1, LANES)), const((1, LANES)),
                  const((LANES, LANES)), const((1, LANES)), const((1, LANES)), const((LANES, n_out))],
        out_specs=[pl.BlockSpec((tl, n_out), lambda i: (i, 0)), const((1, n_out))],
        out_shape=[jax.ShapeDtypeStruct((length, n_out), BF16), jax.ShapeDtypeStruct((1, n_out), F32)],
        compiler_params=_params("arbitrary"),
        name="hyena_filter_ffn",
    )(feats, env, w1p, vec(b1), vec(f1), w2p, vec(b2), vec(f2), w3p)


HY_P = 64
HY_KQ_PAD = 8
HY_ROW_PAD = 8


def _time_major_to_pq(a):
    b, l, c = a.shape
    return a.reshape(b, l // HY_P, HY_P, c).transpose(0, 2, 1, 3)


def _pq_to_time_major(a):
    b, p, q, c = a.shape
    return a.transpose(0, 2, 1, 3).reshape(b, p * q, c)


def _fft_tables(length):
    p_n = HY_P
    q_n = length // p_n
    qn2 = 2 * q_n
    n = 2 * length
    kq_n = q_n + HY_KQ_PAD
    kq = np.arange(kq_n)
    live = (kq <= q_n)[:, None]
    a1 = 2.0 * np.pi * ((kq[:, None] * np.arange(q_n)[None, :]) % qn2) / qn2
    c1, s1 = np.cos(a1) * live, np.sin(a1) * live
    g1 = np.concatenate([c1, -s1], axis=0)
    g1i = np.concatenate([c1.T, -s1.T], axis=1)
    bf = lambda m: jnp.asarray(m, F32).astype(BF16)
    pp = jnp.arange(p_n, dtype=jnp.int32)
    k = jnp.arange(kq_n, dtype=jnp.int32)[:, None, None] + qn2 * pp[None, :, None]
    ang = ((k * pp[None, None, :]) % n).astype(F32) * (2.0 * math.pi / n)
    c2, s2 = jnp.cos(ang), jnp.sin(ang)
    f2t = jnp.concatenate([jnp.concatenate([c2, s2], axis=2), jnp.concatenate([-s2, c2], axis=2)], axis=1)
    c2t, s2t = jnp.swapaxes(c2, 1, 2), jnp.swapaxes(s2, 1, 2)
    f2it = jnp.concatenate([jnp.concatenate([c2t, -s2t], axis=2), jnp.concatenate([s2t, c2t], axis=2)], axis=1)
    return bf(g1), bf(g1i), f2t.astype(BF16), f2it.astype(BF16)


def _conv3_slabs(ref, w_ref):
    p_n, q_n = ref.shape[1], ref.shape[2]
    w = w_ref[...]
    zero = jnp.zeros((1, LANES), F32)

    def raw(p):
        return ref[0, p].astype(F32)

    def slab(p):
        prev = raw(p - 1) if p > 0 else jnp.concatenate([zero, raw(p_n - 1)[:q_n - 1]], axis=0)
        nxt = raw(p + 1) if p < p_n - 1 else jnp.concatenate([raw(0)[1:], zero], axis=0)
        return prev * w[0:1, :] + raw(p) * w[1:2, :] + nxt * w[2:3, :]

    return slab


def _fft_forward(slab, p_n, g1_ref, sre_ref, sim_ref):
    kq_n = g1_ref.shape[0] // 2
    pitch = p_n + HY_ROW_PAD
    for p in range(0, p_n, 2):
        pair = jnp.concatenate([slab(p), slab(p + 1)], axis=1).astype(BF16)
        a = jnp.dot(g1_ref[...], pair, preferred_element_type=F32)
        for off in range(2):
            lanes = slice(off * LANES, (off + 1) * LANES)
            sre_ref[pl.ds(p + off, kq_n, stride=pitch), :] = a[:kq_n, lanes]
            sim_ref[pl.ds(p + off, kq_n, stride=pitch), :] = a[kq_n:, lanes]


def _fft_slab_spectrum(kq, f2t_ref, sre_ref, sim_ref, p_n):
    base = kq * (p_n + HY_ROW_PAD)
    slab = jnp.concatenate([sre_ref[base:base + p_n, :], sim_ref[base:base + p_n, :]], axis=0)
    return base, jnp.dot(f2t_ref[kq], slab.astype(BF16), preferred_element_type=F32)


def _fftconv_kernel(x_ref, gate_ref, xw_ref, gw_ref, h_ref, g1_ref, g1i_ref, f2t_ref, f2it_ref, o_ref,
                    sre_ref, sim_ref, *, conv_x):
    p_n, q_n = x_ref.shape[1], x_ref.shape[2]
    kq_n = g1_ref.shape[0] // 2
    pitch = p_n + HY_ROW_PAD
    x_slab = _conv3_slabs(x_ref, xw_ref) if conv_x else (lambda p: x_ref[0, p])
    gate_slab = _conv3_slabs(gate_ref, gw_ref)
    _fft_forward(x_slab, p_n, g1_ref, sre_ref, sim_ref)

    for kq in range(q_n + 1):
        base, x = _fft_slab_spectrum(kq, f2t_ref, sre_ref, sim_ref, p_n)
        xr, xi = x[:p_n], x[p_n:]
        h = h_ref[0, kq].astype(F32)
        hr, hi = h[:p_n], h[p_n:]
        y = jnp.concatenate([xr * hr - xi * hi, xr * hi + xi * hr], axis=0)
        z = jnp.dot(f2it_ref[kq], y.astype(BF16), preferred_element_type=F32)
        sre_ref[base:base + p_n, :] = z[:p_n]
        sim_ref[base:base + p_n, :] = z[p_n:]

    for p in range(0, p_n, 2):
        pair = []
        for off in range(2):
            zr = sre_ref[pl.ds(p + off, kq_n, stride=pitch), :]
            zi = sim_ref[pl.ds(p + off, kq_n, stride=pitch), :]
            pair.append(jnp.concatenate([zr, zi], axis=0))
        y = jnp.dot(g1i_ref[...], jnp.concatenate(pair, axis=1).astype(BF16), preferred_element_type=F32)
        for off in range(2):
            o_ref[0, p + off] = (gate_slab(p + off) * y[:, off * LANES:(off + 1) * LANES]).astype(o_ref.dtype)


def _filter_response_kernel(hf_ref, hb_ref, ssf_ref, ssb_ref, skip_ref, g1_ref, f2t_ref, o_ref,
                            sre_ref, sim_ref, spec_ref, *, n):
    p_n, q_n = hf_ref.shape[1], hf_ref.shape[2]
    nf = lax.rsqrt(ssf_ref[...] + EPS)
    nb = lax.rsqrt(ssb_ref[...] + EPS)
    _fft_forward(lambda p: hf_ref[0, p], p_n, g1_ref, sre_ref, sim_ref)
    for kq in range(q_n + 1):
        spec_ref[kq] = _fft_slab_spectrum(kq, f2t_ref, sre_ref, sim_ref, p_n)[1]
    _fft_forward(lambda p: hb_ref[0, p], p_n, g1_ref, sre_ref, sim_ref)
    for kq in range(q_n + 1):
        xb = _fft_slab_spectrum(kq, f2t_ref, sre_ref, sim_ref, p_n)[1]
        xf = spec_ref[kq]
        coef = (1.0 if kq in (0, q_n) else 2.0) / n
        hr = (xf[:p_n] * nf + xb[:p_n] * nb + skip_ref[0]) * coef
        hi = (xf[p_n:] * nf - xb[p_n:] * nb) * coef
        o_ref[0, kq] = jnp.concatenate([hr, hi], axis=0).astype(o_ref.dtype)


def _fft_scratch(kq_n, p_n):
    rows = kq_n * (p_n + HY_ROW_PAD)
    return [pltpu.VMEM((rows, LANES), F32), pltpu.VMEM((rows, LANES), F32)]


def _const_spec(a):
    return pl.BlockSpec(a.shape, lambda *_: (0,) * a.ndim)


def _filter_response(h_pq, ss, skip, tables):
    g1, _, f2t, _ = tables
    _, p_n, q_n, _ = h_pq.shape
    kq_n = g1.shape[0] // 2
    nb = HY_W // LANES
    taps = lambda d: pl.BlockSpec((1, p_n, q_n, LANES), lambda o, j: (0, 0, 0, (2 * o + d) * nb + j))
    sumsq = lambda d: pl.BlockSpec((1, LANES), lambda o, j: (0, (2 * o + d) * nb + j))
    return pl.pallas_call(
        functools.partial(_filter_response_kernel, n=2 * p_n * q_n),
        grid=(2, nb),
        in_specs=[taps(0), taps(1), sumsq(0), sumsq(1), pl.BlockSpec((1, 1, LANES), lambda o, j: (o, 0, j)),
                  _const_spec(g1), _const_spec(f2t)],
        out_specs=pl.BlockSpec((1, q_n + 1, 2 * p_n, LANES), lambda o, j: (o, 0, 0, j)),
        out_shape=jax.ShapeDtypeStruct((2, q_n + 1, 2 * p_n, HY_W), BF16),
        scratch_shapes=_fft_scratch(kq_n, p_n) + [pltpu.VMEM((q_n + 1, 2 * p_n, LANES), F32)],
        compiler_params=_params("parallel", "parallel"),
        name="hyena_filter_response",
    )(h_pq, h_pq, ss, ss, skip.reshape(2, 1, HY_W), g1, f2t)


def _long_conv_gated(u, u_block0, conv_x, p_u, gate_block0, conv_w, resp, order, tables):
    g1, g1i, f2t, f2it = tables
    b, p_n, q_n, _ = u.shape
    kq_n = g1.shape[0] // 2
    blk = lambda off: pl.BlockSpec((1, p_n, q_n, LANES), lambda j, bi: (bi, 0, 0, off + j))
    taps = lambda off: pl.BlockSpec((3, LANES), lambda j, bi: (0, off + j))
    return pl.pallas_call(
        functools.partial(_fftconv_kernel, conv_x=conv_x),
        grid=(HY_W // LANES, b),
        in_specs=[blk(u_block0), blk(gate_block0), taps(u_block0 if conv_x else 0), taps(gate_block0),
                  pl.BlockSpec((1, q_n + 1, 2 * p_n, LANES), lambda j, bi: (order, 0, 0, j)),
                  _const_spec(g1), _const_spec(g1i), _const_spec(f2t), _const_spec(f2it)],
        out_specs=blk(0),
        out_shape=jax.ShapeDtypeStruct((b, p_n, q_n, HY_W), BF16),
        scratch_shapes=_fft_scratch(kq_n, p_n),
        compiler_params=_params("parallel", "parallel"),
        name="hyena_fftconv",
    )(u, p_u, conv_w, conv_w, resp, g1, g1i, f2t, f2it)


def _out1_kernel(x_ref, y_ref, z_ref, gate_ref, w_ref, fg_ref, o_ref):
    r = x_ref[0] + gate_ref[0] * _dot(y_ref[0].astype(F32) * _silu(z_ref[0].astype(F32)), w_ref[...])
    ms = jnp.mean(r * r, axis=-1, keepdims=True)
    o_ref[0] = r * lax.rsqrt(ms + EPS) * fg_ref[...]


def _out1(x, y, z, gate, w_out, final_g, tm):
    b, l, d = x.shape
    tile = pl.BlockSpec((1, tm, d), lambda bi, i: (bi, i, 0))
    return pl.pallas_call(
        _out1_kernel,
        grid=(b, l // tm),
        in_specs=[tile, tile, tile, pl.BlockSpec((1, 1, d), lambda bi, i: (bi, 0, 0)),
                  pl.BlockSpec(w_out.shape, lambda bi, i: (0, 0)), pl.BlockSpec((1, d), lambda bi, i: (0, 0))],
        out_specs=tile,
        out_shape=jax.ShapeDtypeStruct((b, l, d), F32),
        compiler_params=_params("parallel", "parallel"),
        name="out_proj1",
    )(x, y, z, gate, w_out.astype(BF16), final_g.reshape(1, d))


def _row_tile(l, want):
    return want if l % want == 0 else l


def _even_layer(x, ctx, c, c_ctx, norm_g, mod_w, mod_b, w_in, rpb, dn_conv, a_log, dt_bias, dn_norm_g, w_out):
    b, l, d = x.shape
    lc = ctx.shape[1]
    off_dn = 3 * NA_W
    off_ab = off_dn + DN_CONV_W
    off_z = off_ab + 4 * DN_HEADS
    rows = b + 1
    pad = (-rows) % 8
    cvecs = jnp.concatenate([c, c_ctx[None, :], jnp.zeros((pad, d), F32)], axis=0)
    m = _modulation(cvecs, mod_w, mod_b)
    shift, scale, gate = (m[:, i * d:(i + 1) * d] for i in range(3))
    lat = lambda a: a[:b, None, :]
    cx = lambda a: jnp.broadcast_to(a[b:b + 1, None, :], (b, 1, d))

    w_na = jnp.concatenate([w_in[:, :NA_W] * (NA_DH ** -0.5 * LOG2E), w_in[:, NA_W:off_dn]], axis=1)
    w_dn = w_in[:, off_dn:off_ab]
    w_ab = jnp.pad(w_in[:, off_ab:off_z], ((0, 0), (0, LANES - 4 * DN_HEADS)))
    w_z = w_in[:, off_z:]
    qkv_x, dnx, abx, z_x = _norm_proj(
        x, norm_g, lat(shift), lat(scale),
        [(w_na, "bf16", BF16), (w_dn, "bf16", BF16), (w_ab, "f32", F32), (w_z, "bf16", BF16)],
        _row_tile(l, 1024))
    qkv_c, dnc, abc = _norm_proj(
        ctx, norm_g, cx(shift), cx(scale),
        [(w_na, "bf16", BF16), (w_dn, "bf16", BF16), (w_ab, "f32", F32)], _row_tile(lc, 256))

    na_x = _neighbourhood_attention(qkv_x, qkv_c, rpb)

    dn_c, gb_c = _dn_prepare(dnc, abc, dn_conv, a_log, dt_bias, _row_tile(lc, 256))
    dn_x, gb_x = _dn_prepare(dnx, abx, dn_conv, a_log, dt_bias, _row_tile(l, 512))
    s0 = jnp.zeros((b, 2 * DN_HEADS, DN_DK, DN_DK), F32)
    _, _, s_ctx = _dn_scan(dn_c, gb_c, s0)
    o_f, o_b, _ = _dn_scan(dn_x, gb_x, s_ctx)

    return _out0(x, na_x, o_f, o_b, z_x, lat(gate), dn_norm_g, w_out, _row_tile(l, 512))


def _hyena_layer(x, c, norm_g, mod_w, mod_b, w_in, conv_w, fw1, fb1, ff1, fw2, fb2, ff2, fw3, skip, w_out,
                 final_g):
    b, l, d = x.shape
    pad = (-b) % 8
    cvecs = jnp.concatenate([c, jnp.zeros((pad, d), F32)], axis=0)
    m = _modulation(cvecs, mod_w, mod_b)
    shift, scale, gate = (m[:b, None, i * d:(i + 1) * d] for i in range(3))
    q_n = l // HY_P
    xt = _time_major_to_pq(x).reshape(b, l, d)
    p_u, gz = _norm_proj(xt, norm_g, shift, scale,
                         [(w_in[:, :3 * HY_W], "bf16", BF16), (w_in[:, 3 * HY_W:], "bf16", BF16)],
                         _row_tile(l, 1024))
    p_u = p_u.reshape(b, HY_P, q_n, 3 * HY_W)

    h_raw, ss = _hyena_filters_raw(l, fw1, fb1, ff1, fw2, fb2, ff2, fw3, _row_tile(l, 256))
    tables = _fft_tables(l)
    resp = _filter_response(h_raw.reshape(1, HY_P, q_n, h_raw.shape[1]), ss, skip, tables)

    nb = HY_W // LANES
    z = _long_conv_gated(p_u, 0, True, p_u, nb, conv_w, resp, 0, tables)
    y = _long_conv_gated(z, 0, False, p_u, 2 * nb, conv_w, resp, 1, tables)
    out = _out1(xt, y.reshape(b, l, HY_W), gz, gate, w_out, final_g, _row_tile(l, 512))
    return _pq_to_time_major(out.reshape(b, HY_P, q_n, d))


def kernel(x, c, ctx, c_ctx, e_norm_g, e_mod_w, e_mod_b, e_w_in, e_na_rpb, e_dn_conv, e_dn_a_log, e_dn_dt_bias, e_dn_norm_g, e_w_out, o_norm_g, o_mod_w, o_mod_b, o_w_in, o_hy_conv, o_ffn_w1, o_ffn_b1, o_ffn_f1, o_ffn_w2, o_ffn_b2, o_ffn_f2, o_ffn_w3, o_hy_skip, o_w_out, final_norm_g):
    x = _even_layer(x, ctx, c, c_ctx, e_norm_g[0], e_mod_w[0], e_mod_b[0], e_w_in[0], e_na_rpb[0],
                    e_dn_conv[0], e_dn_a_log[0], e_dn_dt_bias[0], e_dn_norm_g[0], e_w_out[0])
    return _hyena_layer(x, c, o_norm_g[0], o_mod_w[0], o_mod_b[0], o_w_in[0], o_hy_conv[0],
                        o_ffn_w1[0], o_ffn_b1[0], o_ffn_f1[0], o_ffn_w2[0], o_ffn_b2[0], o_ffn_f2[0],
                        o_ffn_w3[0], o_hy_skip[0], o_w_out[0], final_norm_g)
```

```python
import functools
import math

import numpy as np
import jax
import jax.numpy as jnp
from jax import lax
from jax.experimental import pallas as pl
from jax.experimental.pallas import tpu as pltpu

F32 = jnp.float32
BF16 = jnp.bfloat16
HI = lax.Precision.HIGHEST
EPS = 1e-6
NEG = -1e30
LOG2E = math.log2(math.e)

LANES = 128
VMEM_LIMIT_BYTES = 56 * 1024 * 1024

GRID_W = 64
NA_HEADS = 8
NA_DH = 64
NA_W = NA_HEADS * NA_DH
NA_WIN_R = 8
NA_WIN_C = 16
NA_QROWS = 4
NA_KROWS = NA_QROWS + NA_WIN_R

DN_HEADS = 4
DN_DK = 128
DN_W = DN_HEADS * DN_DK
DN_CONV_W = 3 * DN_W
DN_CONV = 5
DN_CHUNK = 128

HY_W = 1024
HY_EMB = 33
HY_BANDS = (HY_EMB - 1) // 2
HY_DECAY_MIN = math.log(1e-2) / 1.5
HY_DECAY_MAX = math.log(1e-2) / 0.3


def _params(*sem):
    return pltpu.CompilerParams(dimension_semantics=sem, vmem_limit_bytes=VMEM_LIMIT_BYTES)


def _silu(v):
    return v * jax.nn.sigmoid(v)


def _dot(a, b):
    return jnp.dot(a.astype(BF16), b.astype(BF16), preferred_element_type=F32)


def _dot_nt(a, b):
    return lax.dot_general(a.astype(BF16), b.astype(BF16), (((1,), (1,)), ((), ())),
                           preferred_element_type=F32)


def _dot_tn(a, b):
    return lax.dot_general(a.astype(BF16), b.astype(BF16), (((0,), (0,)), ((), ())),
                           preferred_element_type=F32)


def _mod_kernel(c_ref, w_ref, b_ref, o_ref):
    o_ref[...] = jnp.dot(_silu(c_ref[...]), w_ref[...], precision=HI,
                         preferred_element_type=F32) + b_ref[...]


def _modulation(cvecs, w, b):
    r, d = cvecs.shape
    n = w.shape[1]
    tn = 512
    return pl.pallas_call(
        _mod_kernel,
        grid=(n // tn,),
        in_specs=[pl.BlockSpec((r, d), lambda j: (0, 0)),
                  pl.BlockSpec((d, tn), lambda j: (0, j)),
                  pl.BlockSpec((1, tn), lambda j: (0, j))],
        out_specs=pl.BlockSpec((r, tn), lambda j: (0, j)),
        out_shape=jax.ShapeDtypeStruct((r, n), F32),
        compiler_params=_params("parallel"),
        name="adaln_mod",
    )(cvecs, w, b.reshape(1, n))


def _proj_kernel(x_ref, g_ref, sh_ref, sc_ref, *refs, precs):
    n = len(precs)
    w_refs, o_refs = refs[:n], refs[n:]
    x = x_ref[0]
    ms = jnp.mean(x * x, axis=-1, keepdims=True)
    h = x * lax.rsqrt(ms + EPS) * g_ref[...] * (1.0 + sc_ref[0]) + sh_ref[0]
    hb = h.astype(BF16)
    for w_ref, o_ref, prec in zip(w_refs, o_refs, precs):
        if prec == "bf16":
            r = jnp.dot(hb, w_ref[...], preferred_element_type=F32)
        else:
            r = jnp.dot(h, w_ref[...], precision=HI, preferred_element_type=F32)
        o_ref[0] = r.astype(o_ref.dtype)


def _norm_proj(x, norm_g, shift, scale, sections, tm):
    b, l, d = x.shape
    ws, precs, out_shapes, out_specs, w_specs = [], [], [], [], []
    for w, prec, odt in sections:
        ws.append(w.astype(BF16) if prec == "bf16" else w)
        precs.append(prec)
        n = w.shape[1]
        w_specs.append(pl.BlockSpec((d, n), lambda bi, i: (0, 0)))
        out_specs.append(pl.BlockSpec((1, tm, n), lambda bi, i: (bi, i, 0)))
        out_shapes.append(jax.ShapeDtypeStruct((b, l, n), odt))
    vec = pl.BlockSpec((1, 1, d), lambda bi, i: (bi, 0, 0))
    return pl.pallas_call(
        functools.partial(_proj_kernel, precs=tuple(precs)),
        grid=(b, l // tm),
        in_specs=[pl.BlockSpec((1, tm, d), lambda bi, i: (bi, i, 0)),
                  pl.BlockSpec((1, d), lambda bi, i: (0, 0)), vec, vec] + w_specs,
        out_specs=out_specs,
        out_shape=out_shapes,
        compiler_params=_params("parallel", "parallel"),
        name="norm_proj",
    )(x, norm_g.reshape(1, d), shift, scale, *ws)


def _na_block_geometry(rows):
    return ((0, 0), (NA_QROWS, 0), (rows - NA_QROWS, rows - NA_KROWS))


def _na_bias_kernel(rc_ref, o_ref, *, rows):
    wr = min(NA_WIN_R, rows)
    qc = lax.broadcasted_iota(jnp.int32, (GRID_W, LANES), 0)
    lane = lax.broadcasted_iota(jnp.int32, (GRID_W, LANES), 1)
    kc = lane & (GRID_W - 1)
    c0 = jnp.clip(qc - NA_WIN_C // 2, 0, GRID_W - NA_WIN_C)
    col_ok = (kc >= c0) & (kc < c0 + NA_WIN_C)
    neg = jnp.full((GRID_W, LANES), NEG, F32)
    tiles = []
    for dr in range(2 * NA_WIN_R - 1):
        base = jnp.broadcast_to(rc_ref[0, dr:dr + 1, :], (GRID_W, LANES))
        tiles.append(jnp.where(col_ok, pltpu.roll(base, 0, 1, stride=1, stride_axis=0) * LOG2E, neg))
    for g, (r_first, k_first) in enumerate(_na_block_geometry(rows)):
        for i in range(NA_QROWS):
            qr = r_first + i
            r0 = min(max(qr - wr // 2, 0), rows - wr)
            for jp in range(NA_KROWS // 2):
                halves = []
                for j in (2 * jp, 2 * jp + 1):
                    kr = k_first + j
                    halves.append(tiles[kr - qr + NA_WIN_R - 1] if r0 <= kr < r0 + wr else neg)
                o_ref[g, 0, i * GRID_W:(i + 1) * GRID_W, jp * LANES:(jp + 1) * LANES] = jnp.where(
                    lane < GRID_W, halves[0], halves[1])


def _na_bias_table(rpb, rows):
    wc = NA_WIN_C
    fill = jnp.full(rpb.shape[:2] + (GRID_W - (2 * wc - 1),), NEG, F32)
    ring = jnp.concatenate([rpb[..., wc - 1:], fill, rpb[..., :wc - 1]], axis=-1)
    ring = jnp.concatenate([ring, ring], axis=-1)
    nq, nk = NA_QROWS * GRID_W, NA_KROWS * GRID_W
    return pl.pallas_call(
        functools.partial(_na_bias_kernel, rows=rows),
        grid=(NA_HEADS,),
        in_specs=[pl.BlockSpec((1, 2 * NA_WIN_R - 1, LANES), lambda h: (h, 0, 0))],
        out_specs=pl.BlockSpec((3, 1, nq, nk), lambda h: (0, h, 0, 0)),
        out_shape=jax.ShapeDtypeStruct((3, NA_HEADS, nq, nk), F32),
        compiler_params=_params("parallel"),
        name="na_bias_table",
    )(ring)


def _na_kernel(q_ref, k_ref, v_ref, kc_ref, vc_ref, bias_ref, o_ref, *, rows):
    blk = pl.program_id(1)
    nq = NA_QROWS * GRID_W
    nk = NA_KROWS * GRID_W
    k_first = jnp.clip(blk * NA_QROWS - NA_WIN_R // 2, 0, rows - NA_KROWS)
    start = pl.multiple_of(k_first * GRID_W, GRID_W)
    lane = lax.broadcasted_iota(jnp.int32, (nq, LANES), 1)
    low = lane < NA_DH
    for p in range(NA_W // LANES):
        cs = slice(p * LANES, (p + 1) * LANES)
        q2 = q_ref[0, :, cs]
        k2 = k_ref[0, pl.ds(start, nk), cs]
        v2 = v_ref[0, pl.ds(start, nk), cs]
        kc2 = kc_ref[0, :, cs]
        vc2 = vc_ref[0, :, cs]
        zero = jnp.zeros_like(q2)
        qq = jnp.concatenate([jnp.where(low, q2, zero), jnp.where(low, zero, q2)], axis=0)
        bias = jnp.concatenate([bias_ref[0, 2 * p], bias_ref[0, 2 * p + 1]], axis=0)
        s_win = _dot_nt(qq, k2) + bias
        s_ctx = _dot_nt(qq, kc2)
        m = jnp.maximum(jnp.max(s_win, axis=-1, keepdims=True), jnp.max(s_ctx, axis=-1, keepdims=True))
        p_win = jnp.exp2(s_win - m)
        p_ctx = jnp.exp2(s_ctx - m)
        den = jnp.sum(p_win, axis=-1, keepdims=True) + jnp.sum(p_ctx, axis=-1, keepdims=True)
        o = (_dot(p_win, v2) + _dot(p_ctx, vc2)) / den
        o_ref[0, :, cs] = jnp.where(low, o[:nq], o[nq:]).astype(o_ref.dtype)


def _neighbourhood_attention(qkv, qkv_c, rpb):
    b, l, _ = qkv.shape
    lc = qkv_c.shape[1]
    rows = l // GRID_W
    assert rows % NA_QROWS == 0 and rows >= NA_KROWS + 1
    nq = NA_QROWS * GRID_W
    nblk = rows // NA_QROWS
    bias = _na_bias_table(rpb, rows)

    def cfg(bi, i):
        return (jnp.where(i == 0, 0, jnp.where(i == nblk - 1, 2, 1)), 0, 0, 0)

    return pl.pallas_call(
        functools.partial(_na_kernel, rows=rows),
        grid=(b, nblk),
        in_specs=[pl.BlockSpec((1, nq, NA_W), lambda bi, i: (bi, i, 0)),
                  pl.BlockSpec((1, l, NA_W), lambda bi, i: (bi, 0, 1)),
                  pl.BlockSpec((1, l, NA_W), lambda bi, i: (bi, 0, 2)),
                  pl.BlockSpec((1, lc, NA_W), lambda bi, i: (bi, 0, 1)),
                  pl.BlockSpec((1, lc, NA_W), lambda bi, i: (bi, 0, 2)),
                  pl.BlockSpec((1, NA_HEADS, nq, NA_KROWS * GRID_W), cfg)],
        out_specs=pl.BlockSpec((1, nq, NA_W), lambda bi, i: (bi, i, 0)),
        out_shape=jax.ShapeDtypeStruct((b, l, NA_W), BF16),
        compiler_params=_params("parallel", "arbitrary"),
        name="neighbourhood_attention",
    )(qkv, qkv, qkv, qkv_c, qkv_c, bias)


HALO_ROWS = 16


def _halo_specs(tl, width, l):
    nbh = tl // HALO_ROWS
    last = l // HALO_ROWS - 1
    return [pl.BlockSpec((1, tl, width), lambda bi, i: (bi, i, 0)),
            pl.BlockSpec((1, HALO_ROWS, width), lambda bi, i: (bi, jnp.maximum(i * nbh - 1, 0), 0)),
            pl.BlockSpec((1, HALO_ROWS, width), lambda bi, i: (bi, jnp.minimum((i + 1) * nbh, last), 0))]


def _norm_modulate(x, g_ref, sh_ref, sc_ref):
    ms = jnp.mean(x * x, axis=-1, keepdims=True)
    return x * lax.rsqrt(ms + EPS) * g_ref[...] * (1.0 + sc_ref[0]) + sh_ref[0]


def _proj0_kernel(x_ref, xp_ref, xn_ref, g_ref, sh_ref, sc_ref, wna_ref, wdn_ref, wab_ref, wz_ref,
                  cw_ref, al_ref, dtb_ref, qkv_ref, dn_ref, gb_ref, z_ref):
    i = pl.program_id(1)
    tm = x_ref.shape[1]
    h = _norm_modulate(x_ref[0], g_ref, sh_ref, sc_ref)
    hb = h.astype(BF16)
    qkv_ref[0] = jnp.dot(hb, wna_ref[...], preferred_element_type=F32).astype(qkv_ref.dtype)
    z_ref[0] = jnp.dot(hb, wz_ref[...], preferred_element_type=F32).astype(z_ref.dtype)
    ab = jnp.dot(h, wab_ref[...], precision=HI, preferred_element_type=F32)

    h_prev = jnp.where(i > 0, _norm_modulate(xp_ref[0], g_ref, sh_ref, sc_ref), 0.0)
    h_next = jnp.where(i < pl.num_programs(1) - 1, _norm_modulate(xn_ref[0], g_ref, sh_ref, sc_ref), 0.0)
    h_ext = jnp.concatenate([h_prev.astype(BF16), hb, h_next.astype(BF16)], axis=0)
    xe = jnp.dot(h_ext, wdn_ref[...], preferred_element_type=F32)
    half = DN_CONV // 2
    acc = jnp.zeros((tm, xe.shape[1]), F32)
    for j in range(DN_CONV):
        off = HALO_ROWS - half + j
        acc = acc + xe[off:off + tm] * cw_ref[j:j + 1, :]
    t = _silu(acc)
    segs = []
    for hh in range(3 * DN_HEADS):
        seg = t[:, hh * DN_DK:(hh + 1) * DN_DK]
        if hh < 2 * DN_HEADS:
            inv_norm = lax.rsqrt(jnp.sum(seg * seg, axis=-1, keepdims=True) + EPS)
            seg = seg * (inv_norm * (DN_DK ** -0.5) if hh < DN_HEADS else inv_norm)
        segs.append(seg)
    dn_ref[0] = jnp.concatenate(segs, axis=1).astype(dn_ref.dtype)
    zg = ab + dtb_ref[...]
    softplus = jnp.maximum(zg, 0.0) + jnp.log(1.0 + jnp.exp(-jnp.abs(zg)))
    g = -jnp.exp(al_ref[...]) * softplus
    lane = lax.broadcasted_iota(jnp.int32, ab.shape, 1)
    gb_ref[0] = jnp.where(lane < 2 * DN_HEADS, g, jax.nn.sigmoid(ab))


def _norm_proj0(x, norm_g, shift, scale, w_na, w_dn, w_ab, w_z, conv_w, a_log, dt_bias, tm):
    b, l, d = x.shape
    pad = LANES - 2 * DN_HEADS
    al = jnp.pad(a_log.reshape(1, 2 * DN_HEADS), ((0, 0), (0, pad)))
    dtb = jnp.pad(dt_bias.reshape(1, 2 * DN_HEADS), ((0, 0), (0, pad)))
    const = lambda a: pl.BlockSpec(a.shape, lambda bi, i: (0,) * a.ndim)
    vec = pl.BlockSpec((1, 1, d), lambda bi, i: (bi, 0, 0))
    tile = lambda n: pl.BlockSpec((1, tm, n), lambda bi, i: (bi, i, 0))
    consts = [norm_g.reshape(1, d)]
    weights = [w_na.astype(BF16), w_dn.astype(BF16), w_ab, w_z.astype(BF16), conv_w, al, dtb]
    widths = [w_na.shape[1], w_dn.shape[1], LANES, w_z.shape[1]]
    dtypes = [BF16, BF16, F32, BF16]
    return pl.pallas_call(
        _proj0_kernel,
        grid=(b, l // tm),
        in_specs=_halo_specs(tm, d, l) + [const(consts[0]), vec, vec] + [const(w) for w in weights],
        out_specs=[tile(n) for n in widths],
        out_shape=[jax.ShapeDtypeStruct((b, l, n), dt) for n, dt in zip(widths, dtypes)],
        compiler_params=_params("parallel", "parallel"),
        name="norm_proj0",
    )(x, x, x, consts[0], shift, scale, *weights)


DN_INV_BASE = 16


def _unit_triangular_inverse(nil, eye, row, col):
    c = nil.shape[-1]
    sh = int(math.log2(DN_INV_BASE))
    diag = jnp.where((row >> sh) == (col >> sh), nil, 0.0)
    inv = eye - diag
    pw = diag
    for _ in range(sh - 1):
        pw = _bdot(pw, pw)
        inv = inv + _bdot(inv, pw)
    while (1 << sh) < c:
        off = jnp.where(((row >> (sh + 1)) == (col >> (sh + 1))) & ((row >> sh) != (col >> sh)), nil, 0.0)
        inv = inv - _bdot(inv, _bdot(off, inv))
        sh += 1
    return inv


def _bdot(a, b):
    return lax.dot_general(a.astype(BF16), b.astype(BF16), (((2,), (1,)), ((0,), (0,))),
                           preferred_element_type=F32)


def _bdot_nt(a, b):
    return lax.dot_general(a.astype(BF16), b.astype(BF16), (((2,), (2,)), ((0,), (0,))),
                           preferred_element_type=F32)


def _bdot_tn(a, b):
    return lax.dot_general(a.astype(BF16), b.astype(BF16), (((1,), (1,)), ((0,), (0,))),
                           preferred_element_type=F32)


DN_STEP_CHUNKS = 2


def _dn_scan_kernel(xf_ref, xb_ref, gf_ref, gbk_ref, s0_ref, of_ref, ob_ref, sfin_ref, s_ref, *, nsub):
    t = pl.program_id(1)
    c = DN_CHUNK
    nh = DN_HEADS
    nb = 2 * nh

    @pl.when(t == 0)
    def _():
        s_ref[...] = s0_ref[0]

    row = lax.broadcasted_iota(jnp.int32, (c, c), 0)
    col = lax.broadcasted_iota(jnp.int32, (c, c), 1)
    eye = (row == col).astype(F32)
    incl = jnp.stack(([row >= col] * nh + [row <= col] * nh) * nsub)
    strict = jnp.stack(([row > col] * nh + [row < col] * nh) * nsub)
    rowg = lax.broadcasted_iota(jnp.int32, (c, LANES), 0)

    def chunk_rows(d, sq):
        first = (sq if d == 0 else nsub - 1 - sq) * c
        return slice(first, first + c)

    qs, ks, vs, gcs, betas = [], [], [], [], []
    for sq in range(nsub):
        for d, (x_ref, g_ref) in enumerate(((xf_ref, gf_ref), (xb_ref, gbk_ref))):
            rs = chunk_rows(d, sq)
            gb = g_ref[0, rs, :]
            cum = gb
            step = 1
            while step < c:
                if d == 0:
                    cum = cum + jnp.where(rowg >= step, pltpu.roll(cum, step, 0), 0.0)
                else:
                    cum = cum + jnp.where(rowg < c - step, pltpu.roll(cum, c - step, 0), 0.0)
                step *= 2
            for h in range(nh):
                qs.append(x_ref[0, rs, h * DN_DK:(h + 1) * DN_DK].astype(F32))
                ks.append(x_ref[0, rs, DN_W + h * DN_DK:DN_W + (h + 1) * DN_DK].astype(F32))
                vs.append(x_ref[0, rs, 2 * DN_W + h * DN_DK:2 * DN_W + (h + 1) * DN_DK].astype(F32))
                ci = d * nh + h
                gcs.append(jnp.broadcast_to(cum[:, ci:ci + 1], (c, c)))
                betas.append(gb[:, 2 * nh + ci:2 * nh + ci + 1])
    q, k, v, gc_rows, beta = (jnp.stack(a) for a in (qs, ks, vs, gcs, betas))
    gc_cols = jnp.swapaxes(gc_rows, 1, 2)
    decay = jnp.where(incl, jnp.exp(jnp.where(incl, gc_rows - gc_cols, 0.0)), 0.0)
    kb = k * beta
    vb = v * beta
    kq_k = _bdot_nt(jnp.concatenate([kb, q], axis=1), k)
    nil = jnp.where(strict, kq_k[:, :c] * decay, 0.0)
    aqk = kq_k[:, c:] * decay
    egc = jnp.exp(gc_rows)
    inv = _unit_triangular_inverse(nil, eye, row, col)
    sol = _bdot(inv, jnp.concatenate([vb, kb * egc], axis=2))
    u, w = sol[:, :, :DN_DK], sol[:, :, DN_DK:]
    wq = jnp.concatenate([w, q * egc], axis=1)
    s = s_ref[...]
    for sq in range(nsub):
        e = slice(sq * nb, (sq + 1) * nb)
        gc = gc_rows[e]
        wq_s = _bdot(wq[e], s)
        v_new = u[e] - wq_s[:, :c]
        o = wq_s[:, c:] + _bdot(aqk[e], v_new)
        g_last = jnp.concatenate([gc[:nh, c - 1:c, :], gc[nh:, 0:1, :]], axis=0)
        s = s * jnp.exp(g_last) + _bdot_tn(k[e] * jnp.exp(g_last - gc), v_new)
        for h in range(nh):
            of_ref[0, chunk_rows(0, sq), h * DN_DK:(h + 1) * DN_DK] = o[h].astype(of_ref.dtype)
            ob_ref[0, chunk_rows(1, sq), h * DN_DK:(h + 1) * DN_DK] = o[nh + h].astype(ob_ref.dtype)
    s_ref[...] = s

    @pl.when(t == pl.num_programs(1) - 1)
    def _():
        sfin_ref[0] = s_ref[...]


def _dn_scan(qkv, gb, s0):
    b, l, w = qkv.shape
    nsub = DN_STEP_CHUNKS if l % (DN_STEP_CHUNKS * DN_CHUNK) == 0 else 1
    rows = nsub * DN_CHUNK
    n = l // rows
    fwd = lambda bi, t: (bi, t, 0)
    bwd = lambda bi, t: (bi, n - 1 - t, 0)
    state = pl.BlockSpec((1, 2 * DN_HEADS, DN_DK, DN_DK), lambda bi, t: (bi, 0, 0, 0))
    return pl.pallas_call(
        functools.partial(_dn_scan_kernel, nsub=nsub),
        grid=(b, n),
        in_specs=[pl.BlockSpec((1, rows, w), fwd), pl.BlockSpec((1, rows, w), bwd),
                  pl.BlockSpec((1, rows, LANES), fwd), pl.BlockSpec((1, rows, LANES), bwd),
                  state],
        out_specs=[pl.BlockSpec((1, rows, DN_W), fwd), pl.BlockSpec((1, rows, DN_W), bwd),
                   state],
        out_shape=[jax.ShapeDtypeStruct((b, l, DN_W), BF16), jax.ShapeDtypeStruct((b, l, DN_W), BF16),
                   jax.ShapeDtypeStruct((b, 2 * DN_HEADS, DN_DK, DN_DK), F32)],
        scratch_shapes=[pltpu.VMEM((2 * DN_HEADS, DN_DK, DN_DK), F32)],
        compiler_params=_params("parallel", "arbitrary"),
        name="dn_scan",
    )(qkv, qkv, gb, gb, s0)


def _out0_kernel(x_ref, na_ref, of_ref, ob_ref, z_ref, gate_ref, ng_ref, w_ref, o_ref):
    o = of_ref[0].astype(F32) + ob_ref[0].astype(F32)
    segs = [na_ref[0].astype(F32)]
    for h in range(DN_HEADS):
        seg = o[:, h * DN_DK:(h + 1) * DN_DK]
        ms = jnp.mean(seg * seg, axis=-1, keepdims=True)
        segs.append(seg * lax.rsqrt(ms + EPS) * ng_ref[...])
    mix = jnp.concatenate(segs, axis=1) * _silu(z_ref[0].astype(F32))
    o_ref[0] = x_ref[0] + gate_ref[0] * _dot(mix, w_ref[...])


def _out0(x, na, o_f, o_b, z, gate, norm_g, w_out, tm):
    b, l, d = x.shape
    tile = lambda n: pl.BlockSpec((1, tm, n), lambda bi, i: (bi, i, 0))
    return pl.pallas_call(
        _out0_kernel,
        grid=(b, l // tm),
        in_specs=[tile(d), tile(NA_W), tile(DN_W), tile(DN_W), tile(d),
                  pl.BlockSpec((1, 1, d), lambda bi, i: (bi, 0, 0)),
                  pl.BlockSpec((1, DN_DK), lambda bi, i: (0, 0)),
                  pl.BlockSpec(w_out.shape, lambda bi, i: (0, 0))],
        out_specs=tile(d),
        out_shape=jax.ShapeDtypeStruct((b, l, d), F32),
        compiler_params=_params("parallel", "parallel"),
        name="out_proj0",
    )(x, na, o_f, o_b, z, gate, norm_g.reshape(1, DN_DK), w_out.astype(BF16))


def _filter_kernel(feat_ref, decay_ref, w1_ref, b1_ref, f1_ref, w2_ref, b2_ref, f2_ref, w3_ref,
                   h_ref, ss_ref):
    hid = jnp.sin(f1_ref[...] * (jnp.dot(feat_ref[...], w1_ref[...], precision=HI,
                                         preferred_element_type=F32) + b1_ref[...]))
    hid = jnp.sin(f2_ref[...] * (jnp.dot(hid, w2_ref[...], precision=HI,
                                         preferred_element_type=F32) + b2_ref[...]))
    h = jnp.dot(hid, w3_ref[...], precision=HI, preferred_element_type=F32)
    env = jnp.exp(-feat_ref[:, 0:1] * decay_ref[...])
    h = h * jnp.concatenate([env] * 4, axis=1)
    h_ref[...] = h.astype(h_ref.dtype)

    @pl.when(pl.program_id(0) == 0)
    def _():
        ss_ref[...] = jnp.zeros_like(ss_ref)

    ss_ref[...] += jnp.sum(h * h, axis=0, keepdims=True)


def _hyena_filters_raw(length, w1, b1, f1, w2, b2, f2, w3, tl):
    t = jnp.linspace(0.0, 1.0, length, dtype=F32)[:, None]
    wv = 2.0 * math.pi * jnp.arange(length, dtype=F32)[:, None] / length
    f = jnp.linspace(1e-4, HY_BANDS - 1, HY_BANDS, dtype=F32)[None, :]
    feats = jnp.concatenate([t, jnp.cos(f * wv), -jnp.sin(f * wv)], axis=-1)
    decay = jnp.abs(jnp.linspace(HY_DECAY_MIN, HY_DECAY_MAX, HY_W, dtype=F32))
    feats = _time_major_to_pq(feats[None])[0].reshape(length, -1)
    ffn = w1.shape[1]
    pe, pf = LANES - HY_EMB, LANES - ffn
    feats = jnp.pad(feats, ((0, 0), (0, pe)))
    w1p = jnp.pad(w1, ((0, pe), (0, pf)))
    w2p = jnp.pad(w2, ((0, pf), (0, pf)))
    w3p = jnp.pad(w3, ((0, pf), (0, 0)))
    vec = lambda a: jnp.pad(a.reshape(1, ffn), ((0, 0), (0, pf)))
    n_out = w3.shape[1]
    const = lambda shape: pl.BlockSpec(shape, lambda i: (0, 0))
    return pl.pallas_call(
        _filter_kernel,
        grid=(length // tl,),
        in_specs=[pl.BlockSpec((tl, LANES), lambda i: (i, 0)), const((1, HY_W)),
                  const((LANES, LANES)), const((1, LANES)), const((1, LANES)),
                  const((LANES, LANES)), const((1, LANES)), const((1, LANES)), const((LANES, n_out))],
        out_specs=[pl.BlockSpec((tl, n_out), lambda i: (i, 0)), const((1, n_out))],
        out_shape=[jax.ShapeDtypeStruct((length, n_out), BF16), jax.ShapeDtypeStruct((1, n_out), F32)],
        compiler_params=_params("arbitrary"),
        name="hyena_filter_ffn",
    )(feats, decay.reshape(1, HY_W), w1p, vec(b1), vec(f1), w2p, vec(b2), vec(f2), w3p)


HY_P = 64
HY_KQ_PAD = 8
HY_ROW_PAD = 8


def _time_major_to_pq(a):
    b, l, c = a.shape
    return a.reshape(b, l // HY_P, HY_P, c).transpose(0, 2, 1, 3)


def _pq_to_time_major(a):
    b, p, q, c = a.shape
    return a.transpose(0, 2, 1, 3).reshape(b, p * q, c)


def _fft_tables(length):
    p_n = HY_P
    q_n = length // p_n
    qn2 = 2 * q_n
    n = 2 * length
    kq_n = q_n + HY_KQ_PAD
    kq = np.arange(kq_n)
    live = (kq <= q_n)[:, None]
    a1 = 2.0 * np.pi * ((kq[:, None] * np.arange(q_n)[None, :]) % qn2) / qn2
    c1, s1 = np.cos(a1) * live, np.sin(a1) * live
    g1 = np.concatenate([c1, -s1], axis=0)
    g1i = np.concatenate([c1.T, -s1.T], axis=1)
    bf = lambda m: jnp.asarray(m, F32).astype(BF16)
    pp = jnp.arange(p_n, dtype=jnp.int32)
    k = jnp.arange(kq_n, dtype=jnp.int32)[:, None, None] + qn2 * pp[None, :, None]
    ang = ((k * pp[None, None, :]) % n).astype(F32) * (2.0 * math.pi / n)
    c2, s2 = jnp.cos(ang), jnp.sin(ang)
    f2t = jnp.concatenate([jnp.concatenate([c2, s2], axis=2), jnp.concatenate([-s2, c2], axis=2)], axis=1)
    c2t, s2t = jnp.swapaxes(c2, 1, 2), jnp.swapaxes(s2, 1, 2)
    f2it = jnp.concatenate([jnp.concatenate([c2t, -s2t], axis=2), jnp.concatenate([s2t, c2t], axis=2)], axis=1)
    return bf(g1), bf(g1i), f2t.astype(BF16), f2it.astype(BF16)


def _conv3_slabs(ref, w_ref):
    p_n, q_n = ref.shape[1], ref.shape[2]
    w = w_ref[...]
    zero = jnp.zeros((1, LANES), F32)

    def raw(p):
        return ref[0, p].astype(F32)

    def slab(p):
        prev = raw(p - 1) if p > 0 else jnp.concatenate([zero, raw(p_n - 1)[:q_n - 1]], axis=0)
        nxt = raw(p + 1) if p < p_n - 1 else jnp.concatenate([raw(0)[1:], zero], axis=0)
        return prev * w[0:1, :] + raw(p) * w[1:2, :] + nxt * w[2:3, :]

    return slab


def _fft_forward(slab, p_n, g1_ref, sre_ref, sim_ref):
    kq_n = g1_ref.shape[0] // 2
    pitch = p_n + HY_ROW_PAD
    for p in range(0, p_n, 2):
        pair = jnp.concatenate([slab(p), slab(p + 1)], axis=1).astype(BF16)
        a = jnp.dot(g1_ref[...], pair, preferred_element_type=F32)
        for off in range(2):
            lanes = slice(off * LANES, (off + 1) * LANES)
            sre_ref[pl.ds(p + off, kq_n, stride=pitch), :] = a[:kq_n, lanes]
            sim_ref[pl.ds(p + off, kq_n, stride=pitch), :] = a[kq_n:, lanes]


def _fft_slab_spectrum(kq, f2t_ref, sre_ref, sim_ref, p_n):
    base = kq * (p_n + HY_ROW_PAD)
    slab = jnp.concatenate([sre_ref[base:base + p_n, :], sim_ref[base:base + p_n, :]], axis=0)
    return base, jnp.dot(f2t_ref[kq], slab.astype(BF16), preferred_element_type=F32)


def _fftconv_kernel(x_ref, gate_ref, xw_ref, gw_ref, h_ref, g1_ref, g1i_ref, f2t_ref, f2it_ref, o_ref,
                    sre_ref, sim_ref, *, conv_x):
    p_n, q_n = x_ref.shape[1], x_ref.shape[2]
    kq_n = g1_ref.shape[0] // 2
    pitch = p_n + HY_ROW_PAD
    x_slab = _conv3_slabs(x_ref, xw_ref) if conv_x else (lambda p: x_ref[0, p])
    gate_slab = _conv3_slabs(gate_ref, gw_ref)
    _fft_forward(x_slab, p_n, g1_ref, sre_ref, sim_ref)

    for kq in range(q_n + 1):
        base, x = _fft_slab_spectrum(kq, f2t_ref, sre_ref, sim_ref, p_n)
        xr, xi = x[:p_n], x[p_n:]
        h = h_ref[0, kq].astype(F32)
        hr, hi = h[:p_n], h[p_n:]
        y = jnp.concatenate([xr * hr - xi * hi, xr * hi + xi * hr], axis=0)
        z = jnp.dot(f2it_ref[kq], y.astype(BF16), preferred_element_type=F32)
        sre_ref[base:base + p_n, :] = z[:p_n]
        sim_ref[base:base + p_n, :] = z[p_n:]

    for p in range(0, p_n, 2):
        pair = []
        for off in range(2):
            zr = sre_ref[pl.ds(p + off, kq_n, stride=pitch), :]
            zi = sim_ref[pl.ds(p + off, kq_n, stride=pitch), :]
            pair.append(jnp.concatenate([zr, zi], axis=0))
        y = jnp.dot(g1i_ref[...], jnp.concatenate(pair, axis=1).astype(BF16), preferred_element_type=F32)
        for off in range(2):
            o_ref[0, p + off] = (gate_slab(p + off) * y[:, off * LANES:(off + 1) * LANES]).astype(o_ref.dtype)


def _filter_response_kernel(hf_ref, hb_ref, ssf_ref, ssb_ref, skip_ref, g1_ref, f2t_ref, o_ref,
                            sre_ref, sim_ref, spec_ref, *, n):
    p_n, q_n = hf_ref.shape[1], hf_ref.shape[2]
    nf = lax.rsqrt(ssf_ref[...] + EPS)
    nb = lax.rsqrt(ssb_ref[...] + EPS)
    _fft_forward(lambda p: hf_ref[0, p], p_n, g1_ref, sre_ref, sim_ref)
    for kq in range(q_n + 1):
        spec_ref[kq] = _fft_slab_spectrum(kq, f2t_ref, sre_ref, sim_ref, p_n)[1]
    _fft_forward(lambda p: hb_ref[0, p], p_n, g1_ref, sre_ref, sim_ref)
    for kq in range(q_n + 1):
        xb = _fft_slab_spectrum(kq, f2t_ref, sre_ref, sim_ref, p_n)[1]
        xf = spec_ref[kq]
        coef = (1.0 if kq in (0, q_n) else 2.0) / n
        hr = (xf[:p_n] * nf + xb[:p_n] * nb + skip_ref[0]) * coef
        hi = (xf[p_n:] * nf - xb[p_n:] * nb) * coef
        o_ref[0, kq] = jnp.concatenate([hr, hi], axis=0).astype(o_ref.dtype)


def _fft_scratch(kq_n, p_n):
    rows = kq_n * (p_n + HY_ROW_PAD)
    return [pltpu.VMEM((rows, LANES), F32), pltpu.VMEM((rows, LANES), F32)]


def _const_spec(a):
    return pl.BlockSpec(a.shape, lambda *_: (0,) * a.ndim)


def _filter_response(h_pq, ss, skip, tables):
    g1, _, f2t, _ = tables
    _, p_n, q_n, _ = h_pq.shape
    kq_n = g1.shape[0] // 2
    nb = HY_W // LANES
    taps = lambda d: pl.BlockSpec((1, p_n, q_n, LANES), lambda o, j: (0, 0, 0, (2 * o + d) * nb + j))
    sumsq = lambda d: pl.BlockSpec((1, LANES), lambda o, j: (0, (2 * o + d) * nb + j))
    return pl.pallas_call(
        functools.partial(_filter_response_kernel, n=2 * p_n * q_n),
        grid=(2, nb),
        in_specs=[taps(0), taps(1), sumsq(0), sumsq(1), pl.BlockSpec((1, 1, LANES), lambda o, j: (o, 0, j)),
                  _const_spec(g1), _const_spec(f2t)],
        out_specs=pl.BlockSpec((1, q_n + 1, 2 * p_n, LANES), lambda o, j: (o, 0, 0, j)),
        out_shape=jax.ShapeDtypeStruct((2, q_n + 1, 2 * p_n, HY_W), BF16),
        scratch_shapes=_fft_scratch(kq_n, p_n) + [pltpu.VMEM((q_n + 1, 2 * p_n, LANES), F32)],
        compiler_params=_params("parallel", "parallel"),
        name="hyena_filter_response",
    )(h_pq, h_pq, ss, ss, skip.reshape(2, 1, HY_W), g1, f2t)


def _long_conv_gated(u, u_block0, conv_x, p_u, gate_block0, conv_w, resp, order, tables):
    g1, g1i, f2t, f2it = tables
    b, p_n, q_n, _ = u.shape
    kq_n = g1.shape[0] // 2
    blk = lambda off: pl.BlockSpec((1, p_n, q_n, LANES), lambda j, bi: (bi, 0, 0, off + j))
    taps = lambda off: pl.BlockSpec((3, LANES), lambda j, bi: (0, off + j))
    return pl.pallas_call(
        functools.partial(_fftconv_kernel, conv_x=conv_x),
        grid=(HY_W // LANES, b),
        in_specs=[blk(u_block0), blk(gate_block0), taps(u_block0 if conv_x else 0), taps(gate_block0),
                  pl.BlockSpec((1, q_n + 1, 2 * p_n, LANES), lambda j, bi: (order, 0, 0, j)),
                  _const_spec(g1), _const_spec(g1i), _const_spec(f2t), _const_spec(f2it)],
        out_specs=blk(0),
        out_shape=jax.ShapeDtypeStruct((b, p_n, q_n, HY_W), BF16),
        scratch_shapes=_fft_scratch(kq_n, p_n),
        compiler_params=_params("parallel", "parallel"),
        name="hyena_fftconv",
    )(u, p_u, conv_w, conv_w, resp, g1, g1i, f2t, f2it)


def _out1_kernel(x_ref, y_ref, z_ref, gate_ref, w_ref, fg_ref, o_ref):
    r = x_ref[0] + gate_ref[0] * _dot(y_ref[0].astype(F32) * _silu(z_ref[0].astype(F32)), w_ref[...])
    ms = jnp.mean(r * r, axis=-1, keepdims=True)
    o_ref[0] = r * lax.rsqrt(ms + EPS) * fg_ref[...]


def _out1(x, y, z, gate, w_out, final_g, tm):
    b, l, d = x.shape
    tile = pl.BlockSpec((1, tm, d), lambda bi, i: (bi, i, 0))
    return pl.pallas_call(
        _out1_kernel,
        grid=(b, l // tm),
        in_specs=[tile, tile, tile, pl.BlockSpec((1, 1, d), lambda bi, i: (bi, 0, 0)),
                  pl.BlockSpec(w_out.shape, lambda bi, i: (0, 0)), pl.BlockSpec((1, d), lambda bi, i: (0, 0))],
        out_specs=tile,
        out_shape=jax.ShapeDtypeStruct((b, l, d), F32),
        compiler_params=_params("parallel", "parallel"),
        name="out_proj1",
    )(x, y, z, gate, w_out.astype(BF16), final_g.reshape(1, d))


def _row_tile(l, want):
    return want if l % want == 0 else l


def _even_layer(x, ctx, c, c_ctx, norm_g, mod_w, mod_b, w_in, rpb, dn_conv, a_log, dt_bias, dn_norm_g, w_out):
    b, l, d = x.shape
    lc = ctx.shape[1]
    off_dn = 3 * NA_W
    off_ab = off_dn + DN_CONV_W
    off_z = off_ab + 4 * DN_HEADS
    rows = b + 1
    pad = (-rows) % 8
    cvecs = jnp.concatenate([c, c_ctx[None, :], jnp.zeros((pad, d), F32)], axis=0)
    m = _modulation(cvecs, mod_w, mod_b)
    shift, scale, gate = (m[:, i * d:(i + 1) * d] for i in range(3))
    lat = lambda a: a[:b, None, :]
    cx = lambda a: jnp.broadcast_to(a[b:b + 1, None, :], (b, 1, d))

    w_na = jnp.concatenate([w_in[:, :NA_W] * (NA_DH ** -0.5 * LOG2E), w_in[:, NA_W:off_dn]], axis=1)
    w_dn = w_in[:, off_dn:off_ab]
    w_ab = jnp.pad(w_in[:, off_ab:off_z], ((0, 0), (0, LANES - 4 * DN_HEADS)))
    w_z = w_in[:, off_z:]
    qkv_x, dn_x, gb_x, z_x = _norm_proj0(x, norm_g, lat(shift), lat(scale), w_na, w_dn, w_ab, w_z,
                                         dn_conv, a_log, dt_bias, _row_tile(l, 512))
    qkv_c, dn_c, gb_c, _ = _norm_proj0(ctx, norm_g, cx(shift), cx(scale), w_na, w_dn, w_ab, w_z,
                                       dn_conv, a_log, dt_bias, _row_tile(lc, 256))

    na_x = _neighbourhood_attention(qkv_x, qkv_c, rpb)

    s0 = jnp.zeros((b, 2 * DN_HEADS, DN_DK, DN_DK), F32)
    _, _, s_ctx = _dn_scan(dn_c, gb_c, s0)
    o_f, o_b, _ = _dn_scan(dn_x, gb_x, s_ctx)

    return _out0(x, na_x, o_f, o_b, z_x, lat(gate), dn_norm_g, w_out, _row_tile(l, 512))


def _hyena_layer(x, c, norm_g, mod_w, mod_b, w_in, conv_w, fw1, fb1, ff1, fw2, fb2, ff2, fw3, skip, w_out,
                 final_g):
    b, l, d = x.shape
    pad = (-b) % 8
    cvecs = jnp.concatenate([c, jnp.zeros((pad, d), F32)], axis=0)
    m = _modulation(cvecs, mod_w, mod_b)
    shift, scale, gate = (m[:b, None, i * d:(i + 1) * d] for i in range(3))
    q_n = l // HY_P
    xt = _time_major_to_pq(x).reshape(b, l, d)
    p_u, gz = _norm_proj(xt, norm_g, shift, scale,
                         [(w_in[:, :3 * HY_W], "bf16", BF16), (w_in[:, 3 * HY_W:], "bf16", BF16)],
                         _row_tile(l, 1024))
    p_u = p_u.reshape(b, HY_P, q_n, 3 * HY_W)

    h_raw, ss = _hyena_filters_raw(l, fw1, fb1, ff1, fw2, fb2, ff2, fw3, _row_tile(l, 256))
    tables = _fft_tables(l)
    resp = _filter_response(h_raw.reshape(1, HY_P, q_n, h_raw.shape[1]), ss, skip, tables)

    nb = HY_W // LANES
    z = _long_conv_gated(p_u, 0, True, p_u, nb, conv_w, resp, 0, tables)
    y = _long_conv_gated(z, 0, False, p_u, 2 * nb, conv_w, resp, 1, tables)
    out = _out1(xt, y.reshape(b, l, HY_W), gz, gate, w_out, final_g, _row_tile(l, 512))
    return _pq_to_time_major(out.reshape(b, HY_P, q_n, d))


def kernel(x, c, ctx, c_ctx, e_norm_g, e_mod_w, e_mod_b, e_w_in, e_na_rpb, e_dn_conv, e_dn_a_log, e_dn_dt_bias, e_dn_norm_g, e_w_out, o_norm_g, o_mod_w, o_mod_b, o_w_in, o_hy_conv, o_ffn_w1, o_ffn_b1, o_ffn_f1, o_ffn_w2, o_ffn_b2, o_ffn_f2, o_ffn_w3, o_hy_skip, o_w_out, final_norm_g):
    x = _even_layer(x, ctx, c, c_ctx, e_norm_g[0], e_mod_w[0], e_mod_b[0], e_w_in[0], e_na_rpb[0],
                    e_dn_conv[0], e_dn_a_log[0], e_dn_dt_bias[0], e_dn_norm_g[0], e_w_out[0])
    return _hyena_layer(x, c, o_norm_g[0], o_mod_w[0], o_mod_b[0], o_w_in[0], o_hy_conv[0],
                        o_ffn_w1[0], o_ffn_b1[0], o_ffn_f1[0], o_ffn_w2[0], o_ffn_b2[0], o_ffn_f2[0],
                        o_ffn_w3[0], o_hy_skip[0], o_w_out[0], final_norm_g)
```

```python
import functools
import math

import numpy as np
import jax
import jax.numpy as jnp
from jax import lax
from jax.experimental import pallas as pl
from jax.experimental.pallas import tpu as pltpu

F32 = jnp.float32
BF16 = jnp.bfloat16
HI = lax.Precision.HIGHEST
EPS = 1e-6
NEG = -1e30
LOG2E = math.log2(math.e)

LANES = 128
VMEM_LIMIT_BYTES = 56 * 1024 * 1024

GRID_W = 64
NA_HEADS = 8
NA_DH = 64
NA_W = NA_HEADS * NA_DH
NA_WIN_R = 8
NA_WIN_C = 16
NA_QROWS = 4
NA_KROWS = NA_QROWS + NA_WIN_R

DN_HEADS = 4
DN_DK = 128
DN_W = DN_HEADS * DN_DK
DN_CONV_W = 3 * DN_W
DN_CONV = 5
DN_CHUNK = 128

HY_W = 1024
HY_EMB = 33
HY_BANDS = (HY_EMB - 1) // 2
HY_DECAY_MIN = math.log(1e-2) / 1.5
HY_DECAY_MAX = math.log(1e-2) / 0.3


def _params(*sem):
    return pltpu.CompilerParams(dimension_semantics=sem, vmem_limit_bytes=VMEM_LIMIT_BYTES)


def _silu(v):
    return v * jax.nn.sigmoid(v)


def _dot(a, b):
    return jnp.dot(a.astype(BF16), b.astype(BF16), preferred_element_type=F32)


def _dot_nt(a, b):
    return lax.dot_general(a.astype(BF16), b.astype(BF16), (((1,), (1,)), ((), ())),
                           preferred_element_type=F32)


def _dot_tn(a, b):
    return lax.dot_general(a.astype(BF16), b.astype(BF16), (((0,), (0,)), ((), ())),
                           preferred_element_type=F32)


def _mod_kernel(c_ref, w_ref, b_ref, o_ref):
    o_ref[...] = jnp.dot(_silu(c_ref[...]), w_ref[...], precision=HI,
                         preferred_element_type=F32) + b_ref[...]


def _modulation(cvecs, w, b):
    r, d = cvecs.shape
    n = w.shape[1]
    tn = 512
    return pl.pallas_call(
        _mod_kernel,
        grid=(n // tn,),
        in_specs=[pl.BlockSpec((r, d), lambda j: (0, 0)),
                  pl.BlockSpec((d, tn), lambda j: (0, j)),
                  pl.BlockSpec((1, tn), lambda j: (0, j))],
        out_specs=pl.BlockSpec((r, tn), lambda j: (0, j)),
        out_shape=jax.ShapeDtypeStruct((r, n), F32),
        compiler_params=_params("parallel"),
        name="adaln_mod",
    )(cvecs, w, b.reshape(1, n))


def _proj_kernel(x_ref, g_ref, sh_ref, sc_ref, *refs, precs):
    n = len(precs)
    w_refs, o_refs = refs[:n], refs[n:]
    x = x_ref[0]
    ms = jnp.mean(x * x, axis=-1, keepdims=True)
    h = x * lax.rsqrt(ms + EPS) * g_ref[...] * (1.0 + sc_ref[0]) + sh_ref[0]
    hb = h.astype(BF16)
    for w_ref, o_ref, prec in zip(w_refs, o_refs, precs):
        if prec == "bf16":
            r = jnp.dot(hb, w_ref[...], preferred_element_type=F32)
        else:
            r = jnp.dot(h, w_ref[...], precision=HI, preferred_element_type=F32)
        o_ref[0] = r.astype(o_ref.dtype)


def _norm_proj(x, norm_g, shift, scale, sections, tm):
    b, l, d = x.shape
    ws, precs, out_shapes, out_specs, w_specs = [], [], [], [], []
    for w, prec, odt in sections:
        ws.append(w.astype(BF16) if prec == "bf16" else w)
        precs.append(prec)
        n = w.shape[1]
        w_specs.append(pl.BlockSpec((d, n), lambda bi, i: (0, 0)))
        out_specs.append(pl.BlockSpec((1, tm, n), lambda bi, i: (bi, i, 0)))
        out_shapes.append(jax.ShapeDtypeStruct((b, l, n), odt))
    vec = pl.BlockSpec((1, 1, d), lambda bi, i: (bi, 0, 0))
    return pl.pallas_call(
        functools.partial(_proj_kernel, precs=tuple(precs)),
        grid=(b, l // tm),
        in_specs=[pl.BlockSpec((1, tm, d), lambda bi, i: (bi, i, 0)),
                  pl.BlockSpec((1, d), lambda bi, i: (0, 0)), vec, vec] + w_specs,
        out_specs=out_specs,
        out_shape=out_shapes,
        compiler_params=_params("parallel", "parallel"),
        name="norm_proj",
    )(x, norm_g.reshape(1, d), shift, scale, *ws)


def _na_block_geometry(rows):
    return ((0, 0), (NA_QROWS, 0), (rows - NA_QROWS, rows - NA_KROWS))


def _na_bias_kernel(rc_ref, o_ref, *, rows):
    wr = min(NA_WIN_R, rows)
    qc = lax.broadcasted_iota(jnp.int32, (GRID_W, LANES), 0)
    lane = lax.broadcasted_iota(jnp.int32, (GRID_W, LANES), 1)
    kc = lane & (GRID_W - 1)
    c0 = jnp.clip(qc - NA_WIN_C // 2, 0, GRID_W - NA_WIN_C)
    col_ok = (kc >= c0) & (kc < c0 + NA_WIN_C)
    neg = jnp.full((GRID_W, LANES), NEG, F32)
    tiles = []
    for dr in range(2 * NA_WIN_R - 1):
        base = jnp.broadcast_to(rc_ref[0, dr:dr + 1, :], (GRID_W, LANES))
        tiles.append(jnp.where(col_ok, pltpu.roll(base, 0, 1, stride=1, stride_axis=0) * LOG2E, neg))
    for g, (r_first, k_first) in enumerate(_na_block_geometry(rows)):
        for i in range(NA_QROWS):
            qr = r_first + i
            r0 = min(max(qr - wr // 2, 0), rows - wr)
            for jp in range(NA_KROWS // 2):
                halves = []
                for j in (2 * jp, 2 * jp + 1):
                    kr = k_first + j
                    halves.append(tiles[kr - qr + NA_WIN_R - 1] if r0 <= kr < r0 + wr else neg)
                o_ref[g, 0, i * GRID_W:(i + 1) * GRID_W, jp * LANES:(jp + 1) * LANES] = jnp.where(
                    lane < GRID_W, halves[0], halves[1])


def _na_bias_table(rpb, rows):
    wc = NA_WIN_C
    fill = jnp.full(rpb.shape[:2] + (GRID_W - (2 * wc - 1),), NEG, F32)
    ring = jnp.concatenate([rpb[..., wc - 1:], fill, rpb[..., :wc - 1]], axis=-1)
    ring = jnp.concatenate([ring, ring], axis=-1)
    nq, nk = NA_QROWS * GRID_W, NA_KROWS * GRID_W
    return pl.pallas_call(
        functools.partial(_na_bias_kernel, rows=rows),
        grid=(NA_HEADS,),
        in_specs=[pl.BlockSpec((1, 2 * NA_WIN_R - 1, LANES), lambda h: (h, 0, 0))],
        out_specs=pl.BlockSpec((3, 1, nq, nk), lambda h: (0, h, 0, 0)),
        out_shape=jax.ShapeDtypeStruct((3, NA_HEADS, nq, nk), F32),
        compiler_params=_params("parallel"),
        name="na_bias_table",
    )(ring)


def _na_kernel(q_ref, k_ref, v_ref, kc_ref, vc_ref, bias_ref, o_ref, *, rows):
    blk = pl.program_id(1)
    nq = NA_QROWS * GRID_W
    nk = NA_KROWS * GRID_W
    k_first = jnp.clip(blk * NA_QROWS - NA_WIN_R // 2, 0, rows - NA_KROWS)
    start = pl.multiple_of(k_first * GRID_W, GRID_W)
    lane = lax.broadcasted_iota(jnp.int32, (nq, LANES), 1)
    low = lane < NA_DH
    for p in range(NA_W // LANES):
        cs = slice(p * LANES, (p + 1) * LANES)
        q2 = q_ref[0, :, cs]
        k2 = k_ref[0, pl.ds(start, nk), cs]
        v2 = v_ref[0, pl.ds(start, nk), cs]
        kc2 = kc_ref[0, :, cs]
        vc2 = vc_ref[0, :, cs]
        zero = jnp.zeros_like(q2)
        qq = jnp.concatenate([jnp.where(low, q2, zero), jnp.where(low, zero, q2)], axis=0)
        bias = jnp.concatenate([bias_ref[0, 2 * p], bias_ref[0, 2 * p + 1]], axis=0)
        s_win = _dot_nt(qq, k2) + bias
        s_ctx = _dot_nt(qq, kc2)
        yield
        m = jnp.maximum(jnp.max(s_win, axis=-1, keepdims=True), jnp.max(s_ctx, axis=-1, keepdims=True))
        p_win = jnp.exp2(s_win - m)
        p_ctx = jnp.exp2(s_ctx - m)
        den = jnp.sum(p_win, axis=-1, keepdims=True) + jnp.sum(p_ctx, axis=-1, keepdims=True)
        yield
        o = (_dot(p_win, v2) + _dot(p_ctx, vc2)) / den
        o_ref[0, :, cs] = jnp.where(low, o[:nq], o[nq:]).astype(o_ref.dtype)
        yield


def _neighbourhood_attention(qkv, qkv_c, rpb):
    b, l, _ = qkv.shape
    lc = qkv_c.shape[1]
    rows = l // GRID_W
    assert rows % NA_QROWS == 0 and rows >= NA_KROWS + 1
    nq = NA_QROWS * GRID_W
    nblk = rows // NA_QROWS
    bias = _na_bias_table(rpb, rows)

    def cfg(bi, i):
        return (jnp.where(i == 0, 0, jnp.where(i == nblk - 1, 2, 1)), 0, 0, 0)

    return dict(
        kernel=functools.partial(_na_kernel, rows=rows),
        grid=(b, nblk),
        in_specs=[pl.BlockSpec((1, nq, NA_W), lambda bi, i: (bi, i, 0)),
                  pl.BlockSpec((1, l, NA_W), lambda bi, i: (bi, 0, 1)),
                  pl.BlockSpec((1, l, NA_W), lambda bi, i: (bi, 0, 2)),
                  pl.BlockSpec((1, lc, NA_W), lambda bi, i: (bi, 0, 1)),
                  pl.BlockSpec((1, lc, NA_W), lambda bi, i: (bi, 0, 2)),
                  pl.BlockSpec((1, NA_HEADS, nq, NA_KROWS * GRID_W), cfg)],
        out_specs=[pl.BlockSpec((1, nq, NA_W), lambda bi, i: (bi, i, 0))],
        out_shape=[jax.ShapeDtypeStruct((b, l, NA_W), BF16)],
        scratch_shapes=[],
        args=(qkv, qkv, qkv, qkv_c, qkv_c, bias))


HALO_ROWS = 16


def _halo_specs(tl, width, l):
    nbh = tl // HALO_ROWS
    last = l // HALO_ROWS - 1
    return [pl.BlockSpec((1, tl, width), lambda bi, i: (bi, i, 0)),
            pl.BlockSpec((1, HALO_ROWS, width), lambda bi, i: (bi, jnp.maximum(i * nbh - 1, 0), 0)),
            pl.BlockSpec((1, HALO_ROWS, width), lambda bi, i: (bi, jnp.minimum((i + 1) * nbh, last), 0))]


def _norm_modulate(x, g_ref, sh_ref, sc_ref):
    ms = jnp.mean(x * x, axis=-1, keepdims=True)
    return x * lax.rsqrt(ms + EPS) * g_ref[...] * (1.0 + sc_ref[0]) + sh_ref[0]


def _proj0_kernel(x_ref, xp_ref, xn_ref, g_ref, sh_ref, sc_ref, wna_ref, wdn_ref, wab_ref, wz_ref,
                  cw_ref, al_ref, dtb_ref, qkv_ref, dn_ref, gb_ref, z_ref):
    i = pl.program_id(1)
    tm = x_ref.shape[1]
    h = _norm_modulate(x_ref[0], g_ref, sh_ref, sc_ref)
    hb = h.astype(BF16)
    qkv_ref[0] = jnp.dot(hb, wna_ref[...], preferred_element_type=F32).astype(qkv_ref.dtype)
    z_ref[0] = jnp.dot(hb, wz_ref[...], preferred_element_type=F32).astype(z_ref.dtype)
    h_lo = (h - hb.astype(F32)).astype(BF16)
    ab2 = (jnp.dot(hb, wab_ref[...], preferred_element_type=F32)
           + jnp.dot(h_lo, wab_ref[...], preferred_element_type=F32))
    ab = ab2[:, :LANES] + ab2[:, LANES:]

    h_prev = jnp.where(i > 0, _norm_modulate(xp_ref[0], g_ref, sh_ref, sc_ref), 0.0)
    h_next = jnp.where(i < pl.num_programs(1) - 1, _norm_modulate(xn_ref[0], g_ref, sh_ref, sc_ref), 0.0)
    h_ext = jnp.concatenate([h_prev.astype(BF16), hb, h_next.astype(BF16)], axis=0)
    xe = jnp.dot(h_ext, wdn_ref[...], preferred_element_type=F32)
    half = DN_CONV // 2
    acc = jnp.zeros((tm, xe.shape[1]), F32)
    for j in range(DN_CONV):
        off = HALO_ROWS - half + j
        acc = acc + xe[off:off + tm] * cw_ref[j:j + 1, :]
    t = _silu(acc)
    segs = []
    for hh in range(3 * DN_HEADS):
        seg = t[:, hh * DN_DK:(hh + 1) * DN_DK]
        if hh < 2 * DN_HEADS:
            inv_norm = lax.rsqrt(jnp.sum(seg * seg, axis=-1, keepdims=True) + EPS)
            seg = seg * (inv_norm * (DN_DK ** -0.5) if hh < DN_HEADS else inv_norm)
        segs.append(seg)
    dn_ref[0] = jnp.concatenate(segs, axis=1).astype(dn_ref.dtype)
    zg = ab + dtb_ref[...]
    softplus = jnp.maximum(zg, 0.0) + jnp.log(1.0 + jnp.exp(-jnp.abs(zg)))
    g = -jnp.exp(al_ref[...]) * softplus
    lane = lax.broadcasted_iota(jnp.int32, ab.shape, 1)
    gb_ref[0] = jnp.where(lane < 2 * DN_HEADS, g, jax.nn.sigmoid(ab))


def _norm_proj0(x, norm_g, shift, scale, w_na, w_dn, w_ab, w_z, conv_w, a_log, dt_bias, tm):
    b, l, d = x.shape
    pad = LANES - 2 * DN_HEADS
    al = jnp.pad(a_log.reshape(1, 2 * DN_HEADS), ((0, 0), (0, pad)))
    dtb = jnp.pad(dt_bias.reshape(1, 2 * DN_HEADS), ((0, 0), (0, pad)))
    const = lambda a: pl.BlockSpec(a.shape, lambda bi, i: (0,) * a.ndim)
    vec = pl.BlockSpec((1, 1, d), lambda bi, i: (bi, 0, 0))
    tile = lambda n: pl.BlockSpec((1, tm, n), lambda bi, i: (bi, i, 0))
    consts = [norm_g.reshape(1, d)]
    w_ab_hi = w_ab.astype(BF16)
    w_ab2 = jnp.concatenate([w_ab_hi, (w_ab - w_ab_hi.astype(F32)).astype(BF16)], axis=1)
    weights = [w_na.astype(BF16), w_dn.astype(BF16), w_ab2, w_z.astype(BF16), conv_w, al, dtb]
    widths = [w_na.shape[1], w_dn.shape[1], LANES, w_z.shape[1]]
    dtypes = [BF16, BF16, F32, BF16]
    return pl.pallas_call(
        _proj0_kernel,
        grid=(b, l // tm),
        in_specs=_halo_specs(tm, d, l) + [const(consts[0]), vec, vec] + [const(w) for w in weights],
        out_specs=[tile(n) for n in widths],
        out_shape=[jax.ShapeDtypeStruct((b, l, n), dt) for n, dt in zip(widths, dtypes)],
        compiler_params=_params("parallel", "parallel"),
        name="norm_proj0",
    )(x, x, x, consts[0], shift, scale, *weights)


DN_INV_BASE = 16


def _unit_triangular_inverse(nil, eye, row, col):
    c = nil.shape[-1]
    sh = int(math.log2(DN_INV_BASE))
    diag = jnp.where((row >> sh) == (col >> sh), nil, 0.0)
    inv = eye - diag
    pw = diag
    for _ in range(sh - 1):
        pw = _bdot(pw, pw)
        yield
        inv = inv + _bdot(inv, pw)
        yield
    while (1 << sh) < c:
        off = jnp.where(((row >> (sh + 1)) == (col >> (sh + 1))) & ((row >> sh) != (col >> sh)), nil, 0.0)
        t = _bdot(off, inv)
        yield
        inv = inv - _bdot(inv, t)
        yield
        sh += 1
    return inv


def _bdot(a, b):
    return lax.dot_general(a.astype(BF16), b.astype(BF16), (((2,), (1,)), ((0,), (0,))),
                           preferred_element_type=F32)


def _bdot_nt(a, b):
    return lax.dot_general(a.astype(BF16), b.astype(BF16), (((2,), (2,)), ((0,), (0,))),
                           preferred_element_type=F32)


def _bdot_tn(a, b):
    return lax.dot_general(a.astype(BF16), b.astype(BF16), (((1,), (1,)), ((0,), (0,))),
                           preferred_element_type=F32)


DN_STEP_CHUNKS = 2


def _dn_scan_kernel(xf_ref, xb_ref, gf_ref, gbk_ref, s0_ref, of_ref, ob_ref, sfin_ref, s_ref, *, nsub):
    t = pl.program_id(1)
    c = DN_CHUNK
    nh = DN_HEADS
    nb = 2 * nh

    @pl.when(t == 0)
    def _():
        s_ref[...] = s0_ref[0]

    row = lax.broadcasted_iota(jnp.int32, (c, c), 0)
    col = lax.broadcasted_iota(jnp.int32, (c, c), 1)
    eye = (row == col).astype(F32)
    incl = jnp.stack(([row >= col] * nh + [row <= col] * nh) * nsub)
    strict = jnp.stack(([row > col] * nh + [row < col] * nh) * nsub)
    rowg = lax.broadcasted_iota(jnp.int32, (c, LANES), 0)

    def chunk_rows(d, sq):
        first = (sq if d == 0 else nsub - 1 - sq) * c
        return slice(first, first + c)

    qs, ks, vs, gcs, betas = [], [], [], [], []
    for sq in range(nsub):
        for d, (x_ref, g_ref) in enumerate(((xf_ref, gf_ref), (xb_ref, gbk_ref))):
            rs = chunk_rows(d, sq)
            gb = g_ref[0, rs, :]
            cum = gb
            step = 1
            while step < c:
                if d == 0:
                    cum = cum + jnp.where(rowg >= step, pltpu.roll(cum, step, 0), 0.0)
                else:
                    cum = cum + jnp.where(rowg < c - step, pltpu.roll(cum, c - step, 0), 0.0)
                step *= 2
            for h in range(nh):
                qs.append(x_ref[0, rs, h * DN_DK:(h + 1) * DN_DK].astype(F32))
                ks.append(x_ref[0, rs, DN_W + h * DN_DK:DN_W + (h + 1) * DN_DK].astype(F32))
                vs.append(x_ref[0, rs, 2 * DN_W + h * DN_DK:2 * DN_W + (h + 1) * DN_DK].astype(F32))
                ci = d * nh + h
                gcs.append(jnp.broadcast_to(cum[:, ci:ci + 1], (c, c)))
                betas.append(gb[:, 2 * nh + ci:2 * nh + ci + 1])
    q, k, v, gc_rows, beta = (jnp.stack(a) for a in (qs, ks, vs, gcs, betas))
    gc_cols = jnp.swapaxes(gc_rows, 1, 2)
    decay = jnp.where(incl, jnp.exp(jnp.where(incl, gc_rows - gc_cols, 0.0)), 0.0)
    kb = k * beta
    vb = v * beta
    yield
    kq_k = _bdot_nt(jnp.concatenate([kb, q], axis=1), k)
    yield
    nil = jnp.where(strict, kq_k[:, :c] * decay, 0.0)
    aqk = kq_k[:, c:] * decay
    egc = jnp.exp(gc_rows)
    inv = yield from _unit_triangular_inverse(nil, eye, row, col)
    sol = _bdot(inv, jnp.concatenate([vb, kb * egc], axis=2))
    yield
    u, w = sol[:, :, :DN_DK], sol[:, :, DN_DK:]
    wq = jnp.concatenate([w, q * egc], axis=1)
    s = s_ref[...]
    for sq in range(nsub):
        e = slice(sq * nb, (sq + 1) * nb)
        gc = gc_rows[e]
        wq_s = _bdot(wq[e], s)
        yield
        v_new = u[e] - wq_s[:, :c]
        o = wq_s[:, c:] + _bdot(aqk[e], v_new)
        g_last = jnp.concatenate([gc[:nh, c - 1:c, :], gc[nh:, 0:1, :]], axis=0)
        s = s * jnp.exp(g_last) + _bdot_tn(k[e] * jnp.exp(g_last - gc), v_new)
        yield
        for h in range(nh):
            of_ref[0, chunk_rows(0, sq), h * DN_DK:(h + 1) * DN_DK] = o[h].astype(of_ref.dtype)
            ob_ref[0, chunk_rows(1, sq), h * DN_DK:(h + 1) * DN_DK] = o[nh + h].astype(ob_ref.dtype)
    s_ref[...] = s
    sfin_ref[0] = s


def _dn_scan(qkv, gb, s0):
    b, l, w = qkv.shape
    nsub = DN_STEP_CHUNKS if l % (DN_STEP_CHUNKS * DN_CHUNK) == 0 else 1
    rows = nsub * DN_CHUNK
    n = l // rows
    fwd = lambda bi, t: (bi, t, 0)
    bwd = lambda bi, t: (bi, n - 1 - t, 0)
    state = pl.BlockSpec((1, 2 * DN_HEADS, DN_DK, DN_DK), lambda bi, t: (bi, 0, 0, 0))
    return dict(
        kernel=functools.partial(_dn_scan_kernel, nsub=nsub),
        grid=(b, n),
        in_specs=[pl.BlockSpec((1, rows, w), fwd), pl.BlockSpec((1, rows, w), bwd),
                  pl.BlockSpec((1, rows, LANES), fwd), pl.BlockSpec((1, rows, LANES), bwd),
                  state],
        out_specs=[pl.BlockSpec((1, rows, DN_W), fwd), pl.BlockSpec((1, rows, DN_W), bwd),
                   state],
        out_shape=[jax.ShapeDtypeStruct((b, l, DN_W), BF16), jax.ShapeDtypeStruct((b, l, DN_W), BF16),
                   jax.ShapeDtypeStruct((b, 2 * DN_HEADS, DN_DK, DN_DK), F32)],
        scratch_shapes=[pltpu.VMEM((2 * DN_HEADS, DN_DK, DN_DK), F32)],
        args=(qkv, qkv, gb, gb, s0))


def _run_parts(name, *parts):
    grid = parts[0]["grid"]
    assert all(p["grid"] == grid for p in parts)
    n_in = [len(p["in_specs"]) for p in parts]
    n_out = [len(p["out_specs"]) for p in parts]
    n_scr = [len(p["scratch_shapes"]) for p in parts]

    def body(*refs):
        ins, outs, scr = refs[:sum(n_in)], refs[sum(n_in):sum(n_in) + sum(n_out)], refs[sum(n_in) + sum(n_out):]
        live = []
        for p, ni, no, ns in zip(parts, n_in, n_out, n_scr):
            live.append(p["kernel"](*ins[:ni], *outs[:no], *scr[:ns]))
            ins, outs, scr = ins[ni:], outs[no:], scr[ns:]
        while live:
            for g in list(live):
                if next(g, live) is live:
                    live.remove(g)

    res = pl.pallas_call(
        body,
        grid=grid,
        in_specs=[s for p in parts for s in p["in_specs"]],
        out_specs=[s for p in parts for s in p["out_specs"]],
        out_shape=[s for p in parts for s in p["out_shape"]],
        scratch_shapes=[s for p in parts for s in p["scratch_shapes"]],
        compiler_params=_params("parallel", "arbitrary"),
        name=name,
    )(*[a for p in parts for a in p["args"]])
    out, k = [], 0
    for no in n_out:
        out.append(res[k:k + no])
        k += no
    return out


def _out0_kernel(x_ref, na_ref, of_ref, ob_ref, z_ref, gate_ref, ng_ref, w_ref, o_ref):
    o = of_ref[0].astype(F32) + ob_ref[0].astype(F32)
    segs = [na_ref[0].astype(F32)]
    for h in range(DN_HEADS):
        seg = o[:, h * DN_DK:(h + 1) * DN_DK]
        ms = jnp.mean(seg * seg, axis=-1, keepdims=True)
        segs.append(seg * lax.rsqrt(ms + EPS) * ng_ref[...])
    mix = jnp.concatenate(segs, axis=1) * _silu(z_ref[0].astype(F32))
    o_ref[0] = x_ref[0] + gate_ref[0] * _dot(mix, w_ref[...])


def _out0(x, na, o_f, o_b, z, gate, norm_g, w_out, tm):
    b, l, d = x.shape
    tile = lambda n: pl.BlockSpec((1, tm, n), lambda bi, i: (bi, i, 0))
    return pl.pallas_call(
        _out0_kernel,
        grid=(b, l // tm),
        in_specs=[tile(d), tile(NA_W), tile(DN_W), tile(DN_W), tile(d),
                  pl.BlockSpec((1, 1, d), lambda bi, i: (bi, 0, 0)),
                  pl.BlockSpec((1, DN_DK), lambda bi, i: (0, 0)),
                  pl.BlockSpec(w_out.shape, lambda bi, i: (0, 0))],
        out_specs=tile(d),
        out_shape=jax.ShapeDtypeStruct((b, l, d), F32),
        compiler_params=_params("parallel", "parallel"),
        name="out_proj0",
    )(x, na, o_f, o_b, z, gate, norm_g.reshape(1, DN_DK), w_out.astype(BF16))


def _filter_kernel(feat_ref, decay_ref, w1_ref, b1_ref, f1_ref, w2_ref, b2_ref, f2_ref, w3_ref,
                   h_ref, ss_ref):
    hid = jnp.sin(f1_ref[...] * (jnp.dot(feat_ref[...], w1_ref[...], precision=HI,
                                         preferred_element_type=F32) + b1_ref[...]))
    hid = jnp.sin(f2_ref[...] * (jnp.dot(hid, w2_ref[...], precision=HI,
                                         preferred_element_type=F32) + b2_ref[...]))
    h = jnp.dot(hid, w3_ref[...], precision=HI, preferred_element_type=F32)
    env = jnp.exp(-feat_ref[:, 0:1] * decay_ref[...])
    h = h * jnp.concatenate([env] * 4, axis=1)
    h_ref[...] = h.astype(h_ref.dtype)

    @pl.when(pl.program_id(0) == 0)
    def _():
        ss_ref[...] = jnp.zeros_like(ss_ref)

    ss_ref[...] += jnp.sum(h * h, axis=0, keepdims=True)


def _hyena_filters_raw(length, w1, b1, f1, w2, b2, f2, w3, tl):
    t = jnp.linspace(0.0, 1.0, length, dtype=F32)[:, None]
    wv = 2.0 * math.pi * jnp.arange(length, dtype=F32)[:, None] / length
    f = jnp.linspace(1e-4, HY_BANDS - 1, HY_BANDS, dtype=F32)[None, :]
    feats = jnp.concatenate([t, jnp.cos(f * wv), -jnp.sin(f * wv)], axis=-1)
    decay = jnp.abs(jnp.linspace(HY_DECAY_MIN, HY_DECAY_MAX, HY_W, dtype=F32))
    feats = _time_major_to_pq(feats[None])[0].reshape(length, -1)
    ffn = w1.shape[1]
    pe, pf = LANES - HY_EMB, LANES - ffn
    feats = jnp.pad(feats, ((0, 0), (0, pe)))
    w1p = jnp.pad(w1, ((0, pe), (0, pf)))
    w2p = jnp.pad(w2, ((0, pf), (0, pf)))
    w3p = jnp.pad(w3, ((0, pf), (0, 0)))
    vec = lambda a: jnp.pad(a.reshape(1, ffn), ((0, 0), (0, pf)))
    n_out = w3.shape[1]
    const = lambda shape: pl.BlockSpec(shape, lambda i: (0, 0))
    return pl.pallas_call(
        _filter_kernel,
        grid=(length // tl,),
        in_specs=[pl.BlockSpec((tl, LANES), lambda i: (i, 0)), const((1, HY_W)),
                  const((LANES, LANES)), const((1, LANES)), const((1, LANES)),
                  const((LANES, LANES)), const((1, LANES)), const((1, LANES)), const((LANES, n_out))],
        out_specs=[pl.BlockSpec((tl, n_out), lambda i: (i, 0)), const((1, n_out))],
        out_shape=[jax.ShapeDtypeStruct((length, n_out), BF16), jax.ShapeDtypeStruct((1, n_out), F32)],
        compiler_params=_params("arbitrary"),
        name="hyena_filter_ffn",
    )(feats, decay.reshape(1, HY_W), w1p, vec(b1), vec(f1), w2p, vec(b2), vec(f2), w3p)


HY_P = 64
HY_KQ_PAD = 8
HY_ROW_PAD = 8


def _time_major_to_pq(a):
    b, l, c = a.shape
    return a.reshape(b, l // HY_P, HY_P, c).transpose(0, 2, 1, 3)


def _pq_to_time_major(a):
    b, p, q, c = a.shape
    return a.transpose(0, 2, 1, 3).reshape(b, p * q, c)


def _fft_tables(length):
    p_n = HY_P
    q_n = length // p_n
    qn2 = 2 * q_n
    n = 2 * length
    kq_n = q_n + HY_KQ_PAD
    kq = np.arange(kq_n)
    live = (kq <= q_n)[:, None]
    a1 = 2.0 * np.pi * ((kq[:, None] * np.arange(q_n)[None, :]) % qn2) / qn2
    c1, s1 = np.cos(a1) * live, np.sin(a1) * live
    g1 = np.concatenate([c1, -s1], axis=0)
    g1i = np.concatenate([c1.T, -s1.T], axis=1)
    bf = lambda m: jnp.asarray(m, F32).astype(BF16)
    pp = jnp.arange(p_n, dtype=jnp.int32)
    k = jnp.arange(kq_n, dtype=jnp.int32)[:, None, None] + qn2 * pp[None, :, None]
    ang = ((k * pp[None, None, :]) % n).astype(F32) * (2.0 * math.pi / n)
    c2, s2 = jnp.cos(ang), jnp.sin(ang)
    f2t = jnp.concatenate([jnp.concatenate([c2, s2], axis=2), jnp.concatenate([-s2, c2], axis=2)], axis=1)
    c2t, s2t = jnp.swapaxes(c2, 1, 2), jnp.swapaxes(s2, 1, 2)
    f2it = jnp.concatenate([jnp.concatenate([c2t, -s2t], axis=2), jnp.concatenate([s2t, c2t], axis=2)], axis=1)
    return bf(g1), bf(g1i), f2t.astype(BF16), f2it.astype(BF16)


def _conv3_slabs(ref, w_ref):
    p_n, q_n = ref.shape[1], ref.shape[2]
    w = w_ref[...]
    zero = jnp.zeros((1, LANES), F32)

    def raw(p):
        return ref[0, p].astype(F32)

    def slab(p):
        prev = raw(p - 1) if p > 0 else jnp.concatenate([zero, raw(p_n - 1)[:q_n - 1]], axis=0)
        nxt = raw(p + 1) if p < p_n - 1 else jnp.concatenate([raw(0)[1:], zero], axis=0)
        return prev * w[0:1, :] + raw(p) * w[1:2, :] + nxt * w[2:3, :]

    return slab


def _fft_forward(slab, p_n, g1_ref, sre_ref, sim_ref):
    kq_n = g1_ref.shape[0] // 2
    pitch = p_n + HY_ROW_PAD
    for p in range(0, p_n, 2):
        pair = jnp.concatenate([slab(p), slab(p + 1)], axis=1).astype(BF16)
        a = jnp.dot(g1_ref[...], pair, preferred_element_type=F32)
        for off in range(2):
            lanes = slice(off * LANES, (off + 1) * LANES)
            sre_ref[pl.ds(p + off, kq_n, stride=pitch), :] = a[:kq_n, lanes]
            sim_ref[pl.ds(p + off, kq_n, stride=pitch), :] = a[kq_n:, lanes]


def _fft_slab_spectrum(kq, f2t_ref, sre_ref, sim_ref, p_n):
    base = kq * (p_n + HY_ROW_PAD)
    slab = jnp.concatenate([sre_ref[base:base + p_n, :], sim_ref[base:base + p_n, :]], axis=0)
    return base, jnp.dot(f2t_ref[kq], slab.astype(BF16), preferred_element_type=F32)


def _fftconv_kernel(x_ref, gate_ref, xw_ref, gw_ref, h_ref, g1_ref, g1i_ref, f2t_ref, f2it_ref, o_ref,
                    sre_ref, sim_ref, *, conv_x):
    p_n, q_n = x_ref.shape[1], x_ref.shape[2]
    kq_n = g1_ref.shape[0] // 2
    pitch = p_n + HY_ROW_PAD
    x_slab = _conv3_slabs(x_ref, xw_ref) if conv_x else (lambda p: x_ref[0, p])
    gate_slab = _conv3_slabs(gate_ref, gw_ref)
    _fft_forward(x_slab, p_n, g1_ref, sre_ref, sim_ref)

    for kq in range(q_n + 1):
        base, x = _fft_slab_spectrum(kq, f2t_ref, sre_ref, sim_ref, p_n)
        xr, xi = x[:p_n], x[p_n:]
        h = h_ref[0, kq].astype(F32)
        hr, hi = h[:p_n], h[p_n:]
        y = jnp.concatenate([xr * hr - xi * hi, xr * hi + xi * hr], axis=0)
        z = jnp.dot(f2it_ref[kq], y.astype(BF16), preferred_element_type=F32)
        sre_ref[base:base + p_n, :] = z[:p_n]
        sim_ref[base:base + p_n, :] = z[p_n:]

    for p in range(0, p_n, 2):
        pair = []
        for off in range(2):
            zr = sre_ref[pl.ds(p + off, kq_n, stride=pitch), :]
            zi = sim_ref[pl.ds(p + off, kq_n, stride=pitch), :]
            pair.append(jnp.concatenate([zr, zi], axis=0))
        y = jnp.dot(g1i_ref[...], jnp.concatenate(pair, axis=1).astype(BF16), preferred_element_type=F32)
        for off in range(2):
            o_ref[0, p + off] = (gate_slab(p + off) * y[:, off * LANES:(off + 1) * LANES]).astype(o_ref.dtype)


def _filter_response_kernel(hf_ref, hb_ref, ssf_ref, ssb_ref, skip_ref, g1_ref, f2t_ref, o_ref,
                            sre_ref, sim_ref, spec_ref, *, n):
    p_n, q_n = hf_ref.shape[1], hf_ref.shape[2]
    nf = lax.rsqrt(ssf_ref[...] + EPS)
    nb = lax.rsqrt(ssb_ref[...] + EPS)
    _fft_forward(lambda p: hf_ref[0, p], p_n, g1_ref, sre_ref, sim_ref)
    for kq in range(q_n + 1):
        spec_ref[kq] = _fft_slab_spectrum(kq, f2t_ref, sre_ref, sim_ref, p_n)[1]
    _fft_forward(lambda p: hb_ref[0, p], p_n, g1_ref, sre_ref, sim_ref)
    for kq in range(q_n + 1):
        xb = _fft_slab_spectrum(kq, f2t_ref, sre_ref, sim_ref, p_n)[1]
        xf = spec_ref[kq]
        coef = (1.0 if kq in (0, q_n) else 2.0) / n
        hr = (xf[:p_n] * nf + xb[:p_n] * nb + skip_ref[0]) * coef
        hi = (xf[p_n:] * nf - xb[p_n:] * nb) * coef
        o_ref[0, kq] = jnp.concatenate([hr, hi], axis=0).astype(o_ref.dtype)


def _fft_scratch(kq_n, p_n):
    rows = kq_n * (p_n + HY_ROW_PAD)
    return [pltpu.VMEM((rows, LANES), F32), pltpu.VMEM((rows, LANES), F32)]


def _const_spec(a):
    return pl.BlockSpec(a.shape, lambda *_: (0,) * a.ndim)


def _filter_response(h_pq, ss, skip, tables):
    g1, _, f2t, _ = tables
    _, p_n, q_n, _ = h_pq.shape
    kq_n = g1.shape[0] // 2
    nb = HY_W // LANES
    taps = lambda d: pl.BlockSpec((1, p_n, q_n, LANES), lambda o, j: (0, 0, 0, (2 * o + d) * nb + j))
    sumsq = lambda d: pl.BlockSpec((1, LANES), lambda o, j: (0, (2 * o + d) * nb + j))
    return pl.pallas_call(
        functools.partial(_filter_response_kernel, n=2 * p_n * q_n),
        grid=(2, nb),
        in_specs=[taps(0), taps(1), sumsq(0), sumsq(1), pl.BlockSpec((1, 1, LANES), lambda o, j: (o, 0, j)),
                  _const_spec(g1), _const_spec(f2t)],
        out_specs=pl.BlockSpec((1, q_n + 1, 2 * p_n, LANES), lambda o, j: (o, 0, 0, j)),
        out_shape=jax.ShapeDtypeStruct((2, q_n + 1, 2 * p_n, HY_W), BF16),
        scratch_shapes=_fft_scratch(kq_n, p_n) + [pltpu.VMEM((q_n + 1, 2 * p_n, LANES), F32)],
        compiler_params=_params("parallel", "parallel"),
        name="hyena_filter_response",
    )(h_pq, h_pq, ss, ss, skip.reshape(2, 1, HY_W), g1, f2t)


def _long_conv_gated(u, u_block0, conv_x, p_u, gate_block0, conv_w, resp, order, tables):
    g1, g1i, f2t, f2it = tables
    b, p_n, q_n, _ = u.shape
    kq_n = g1.shape[0] // 2
    blk = lambda off: pl.BlockSpec((1, p_n, q_n, LANES), lambda j, bi: (bi, 0, 0, off + j))
    taps = lambda off: pl.BlockSpec((3, LANES), lambda j, bi: (0, off + j))
    return pl.pallas_call(
        functools.partial(_fftconv_kernel, conv_x=conv_x),
        grid=(HY_W // LANES, b),
        in_specs=[blk(u_block0), blk(gate_block0), taps(u_block0 if conv_x else 0), taps(gate_block0),
                  pl.BlockSpec((1, q_n + 1, 2 * p_n, LANES), lambda j, bi: (order, 0, 0, j)),
                  _const_spec(g1), _const_spec(g1i), _const_spec(f2t), _const_spec(f2it)],
        out_specs=blk(0),
        out_shape=jax.ShapeDtypeStruct((b, p_n, q_n, HY_W), BF16),
        scratch_shapes=_fft_scratch(kq_n, p_n),
        compiler_params=_params("parallel", "parallel"),
        name="hyena_fftconv",
    )(u, p_u, conv_w, conv_w, resp, g1, g1i, f2t, f2it)


def _out1_kernel(x_ref, y_ref, z_ref, gate_ref, w_ref, fg_ref, o_ref):
    r = x_ref[0] + gate_ref[0] * _dot(y_ref[0].astype(F32) * _silu(z_ref[0].astype(F32)), w_ref[...])
    ms = jnp.mean(r * r, axis=-1, keepdims=True)
    o_ref[0] = r * lax.rsqrt(ms + EPS) * fg_ref[...]


def _out1(x, y, z, gate, w_out, final_g, tm):
    b, l, d = x.shape
    tile = pl.BlockSpec((1, tm, d), lambda bi, i: (bi, i, 0))
    return pl.pallas_call(
        _out1_kernel,
        grid=(b, l // tm),
        in_specs=[tile, tile, tile, pl.BlockSpec((1, 1, d), lambda bi, i: (bi, 0, 0)),
                  pl.BlockSpec(w_out.shape, lambda bi, i: (0, 0)), pl.BlockSpec((1, d), lambda bi, i: (0, 0))],
        out_specs=tile,
        out_shape=jax.ShapeDtypeStruct((b, l, d), F32),
        compiler_params=_params("parallel", "parallel"),
        name="out_proj1",
    )(x, y, z, gate, w_out.astype(BF16), final_g.reshape(1, d))


def _row_tile(l, want):
    return want if l % want == 0 else l


def _even_layer(x, ctx, c, c_ctx, norm_g, mod_w, mod_b, w_in, rpb, dn_conv, a_log, dt_bias, dn_norm_g, w_out):
    b, l, d = x.shape
    lc = ctx.shape[1]
    off_dn = 3 * NA_W
    off_ab = off_dn + DN_CONV_W
    off_z = off_ab + 4 * DN_HEADS
    rows = b + 1
    pad = (-rows) % 8
    cvecs = jnp.concatenate([c, c_ctx[None, :], jnp.zeros((pad, d), F32)], axis=0)
    m = _modulation(cvecs, mod_w, mod_b)
    shift, scale, gate = (m[:, i * d:(i + 1) * d] for i in range(3))
    lat = lambda a: a[:b, None, :]
    cx = lambda a: jnp.broadcast_to(a[b:b + 1, None, :], (b, 1, d))

    w_na = jnp.concatenate([w_in[:, :NA_W] * (NA_DH ** -0.5 * LOG2E), w_in[:, NA_W:off_dn]], axis=1)
    w_dn = w_in[:, off_dn:off_ab]
    w_ab = jnp.pad(w_in[:, off_ab:off_z], ((0, 0), (0, LANES - 4 * DN_HEADS)))
    w_z = w_in[:, off_z:]
    qkv_x, dn_x, gb_x, z_x = _norm_proj0(x, norm_g, lat(shift), lat(scale), w_na, w_dn, w_ab, w_z,
                                         dn_conv, a_log, dt_bias, _row_tile(l, 512))
    qkv_c, dn_c, gb_c, _ = _norm_proj0(ctx, norm_g, cx(shift), cx(scale), w_na, w_dn, w_ab, w_z,
                                       dn_conv, a_log, dt_bias, _row_tile(lc, 256))

    s0 = jnp.zeros((b, 2 * DN_HEADS, DN_DK, DN_DK), F32)
    ((_, _, s_ctx),) = _run_parts("dn_scan_ctx", _dn_scan(dn_c, gb_c, s0))
    (o_f, o_b, _), (na_x,) = _run_parts("dn_scan_and_attention", _dn_scan(dn_x, gb_x, s_ctx),
                                        _neighbourhood_attention(qkv_x, qkv_c, rpb))

    return _out0(x, na_x, o_f, o_b, z_x, lat(gate), dn_norm_g, w_out, _row_tile(l, 512))


def _hyena_layer(x, c, norm_g, mod_w, mod_b, w_in, conv_w, fw1, fb1, ff1, fw2, fb2, ff2, fw3, skip, w_out,
                 final_g):
    b, l, d = x.shape
    pad = (-b) % 8
    cvecs = jnp.concatenate([c, jnp.zeros((pad, d), F32)], axis=0)
    m = _modulation(cvecs, mod_w, mod_b)
    shift, scale, gate = (m[:b, None, i * d:(i + 1) * d] for i in range(3))
    q_n = l // HY_P
    xt = _time_major_to_pq(x).reshape(b, l, d)
    p_u, gz = _norm_proj(xt, norm_g, shift, scale,
                         [(w_in[:, :3 * HY_W], "bf16", BF16), (w_in[:, 3 * HY_W:], "bf16", BF16)],
                         _row_tile(l, 1024))
    p_u = p_u.reshape(b, HY_P, q_n, 3 * HY_W)

    h_raw, ss = _hyena_filters_raw(l, fw1, fb1, ff1, fw2, fb2, ff2, fw3, _row_tile(l, 256))
    tables = _fft_tables(l)
    resp = _filter_response(h_raw.reshape(1, HY_P, q_n, h_raw.shape[1]), ss, skip, tables)

    nb = HY_W // LANES
    z = _long_conv_gated(p_u, 0, True, p_u, nb, conv_w, resp, 0, tables)
    y = _long_conv_gated(z, 0, False, p_u, 2 * nb, conv_w, resp, 1, tables)
    out = _out1(xt, y.reshape(b, l, HY_W), gz, gate, w_out, final_g, _row_tile(l, 512))
    return _pq_to_time_major(out.reshape(b, HY_P, q_n, d))


def kernel(x, c, ctx, c_ctx, e_norm_g, e_mod_w, e_mod_b, e_w_in, e_na_rpb, e_dn_conv, e_dn_a_log, e_dn_dt_bias, e_dn_norm_g, e_w_out, o_norm_g, o_mod_w, o_mod_b, o_w_in, o_hy_conv, o_ffn_w1, o_ffn_b1, o_ffn_f1, o_ffn_w2, o_ffn_b2, o_ffn_f2, o_ffn_w3, o_hy_skip, o_w_out, final_norm_g):
    x = _even_layer(x, ctx, c, c_ctx, e_norm_g[0], e_mod_w[0], e_mod_b[0], e_w_in[0], e_na_rpb[0],
                    e_dn_conv[0], e_dn_a_log[0], e_dn_dt_bias[0], e_dn_norm_g[0], e_w_out[0])
    return _hyena_layer(x, c, o_norm_g[0], o_mod_w[0], o_mod_b[0], o_w_in[0], o_hy_conv[0],
                        o_ffn_w1[0], o_ffn_b1[0], o_ffn_f1[0], o_ffn_w2[0], o_ffn_b2[0], o_ffn_f2[0],
                        o_ffn_w3[0], o_hy_skip[0], o_w_out[0], final_norm_g)
```

```python
import functools
import math

import numpy as np
import jax
import jax.numpy as jnp
from jax import lax
from jax.experimental import pallas as pl
from jax.experimental.pallas import tpu as pltpu

F32 = jnp.float32
BF16 = jnp.bfloat16
HI = lax.Precision.HIGHEST
EPS = 1e-6
NEG = -1e30
LOG2E = math.log2(math.e)

LANES = 128
VMEM_LIMIT_BYTES = 56 * 1024 * 1024

GRID_W = 64
NA_HEADS = 8
NA_DH = 64
NA_W = NA_HEADS * NA_DH
NA_WIN_R = 8
NA_WIN_C = 16
NA_QROWS = 4
NA_KROWS = NA_QROWS + NA_WIN_R

DN_HEADS = 4
DN_DK = 128
DN_W = DN_HEADS * DN_DK
DN_CONV_W = 3 * DN_W
DN_CONV = 5
DN_CHUNK = 128

HY_W = 1024
HY_EMB = 33
HY_BANDS = (HY_EMB - 1) // 2
HY_DECAY_MIN = math.log(1e-2) / 1.5
HY_DECAY_MAX = math.log(1e-2) / 0.3


def _params(*sem):
    return pltpu.CompilerParams(dimension_semantics=sem, vmem_limit_bytes=VMEM_LIMIT_BYTES)


def _silu(v):
    return v * jax.nn.sigmoid(v)


def _dot(a, b):
    return jnp.dot(a.astype(BF16), b.astype(BF16), preferred_element_type=F32)


def _dot_nt(a, b):
    return lax.dot_general(a.astype(BF16), b.astype(BF16), (((1,), (1,)), ((), ())),
                           preferred_element_type=F32)


def _dot_tn(a, b):
    return lax.dot_general(a.astype(BF16), b.astype(BF16), (((0,), (0,)), ((), ())),
                           preferred_element_type=F32)


def _mod_kernel(c_ref, w_ref, b_ref, o_ref):
    o_ref[...] = jnp.dot(_silu(c_ref[...]), w_ref[...], precision=HI,
                         preferred_element_type=F32) + b_ref[...]


def _modulation(cvecs, w, b):
    r, d = cvecs.shape
    n = w.shape[1]
    tn = 512
    return pl.pallas_call(
        _mod_kernel,
        grid=(n // tn,),
        in_specs=[pl.BlockSpec((r, d), lambda j: (0, 0)),
                  pl.BlockSpec((d, tn), lambda j: (0, j)),
                  pl.BlockSpec((1, tn), lambda j: (0, j))],
        out_specs=pl.BlockSpec((r, tn), lambda j: (0, j)),
        out_shape=jax.ShapeDtypeStruct((r, n), F32),
        compiler_params=_params("parallel"),
        name="adaln_mod",
    )(cvecs, w, b.reshape(1, n))


def _proj_kernel(x_ref, g_ref, sh_ref, sc_ref, *refs, precs):
    n = len(precs)
    w_refs, o_refs = refs[:n], refs[n:]
    x = x_ref[0]
    ms = jnp.mean(x * x, axis=-1, keepdims=True)
    h = x * lax.rsqrt(ms + EPS) * g_ref[...] * (1.0 + sc_ref[0]) + sh_ref[0]
    hb = h.astype(BF16)
    for w_ref, o_ref, prec in zip(w_refs, o_refs, precs):
        if prec == "bf16":
            r = jnp.dot(hb, w_ref[...], preferred_element_type=F32)
        else:
            r = jnp.dot(h, w_ref[...], precision=HI, preferred_element_type=F32)
        o_ref[0] = r.astype(o_ref.dtype)


def _norm_proj(x, norm_g, shift, scale, sections, tm):
    b, l, d = x.shape
    ws, precs, out_shapes, out_specs, w_specs = [], [], [], [], []
    for w, prec, odt in sections:
        ws.append(w.astype(BF16) if prec == "bf16" else w)
        precs.append(prec)
        n = w.shape[1]
        w_specs.append(pl.BlockSpec((d, n), lambda bi, i: (0, 0)))
        out_specs.append(pl.BlockSpec((1, tm, n), lambda bi, i: (bi, i, 0)))
        out_shapes.append(jax.ShapeDtypeStruct((b, l, n), odt))
    vec = pl.BlockSpec((1, 1, d), lambda bi, i: (bi, 0, 0))
    return pl.pallas_call(
        functools.partial(_proj_kernel, precs=tuple(precs)),
        grid=(b, l // tm),
        in_specs=[pl.BlockSpec((1, tm, d), lambda bi, i: (bi, i, 0)),
                  pl.BlockSpec((1, d), lambda bi, i: (0, 0)), vec, vec] + w_specs,
        out_specs=out_specs,
        out_shape=out_shapes,
        compiler_params=_params("parallel", "parallel"),
        name="norm_proj",
    )(x, norm_g.reshape(1, d), shift, scale, *ws)


def _na_block_geometry(rows):
    return ((0, 0), (NA_QROWS, 0), (rows - NA_QROWS, rows - NA_KROWS))


def _na_bias_kernel(rc_ref, o_ref, *, rows):
    wr = min(NA_WIN_R, rows)
    qc = lax.broadcasted_iota(jnp.int32, (GRID_W, LANES), 0)
    lane = lax.broadcasted_iota(jnp.int32, (GRID_W, LANES), 1)
    kc = lane & (GRID_W - 1)
    c0 = jnp.clip(qc - NA_WIN_C // 2, 0, GRID_W - NA_WIN_C)
    col_ok = (kc >= c0) & (kc < c0 + NA_WIN_C)
    neg = jnp.full((GRID_W, LANES), NEG, F32)
    tiles = []
    for dr in range(2 * NA_WIN_R - 1):
        base = jnp.broadcast_to(rc_ref[0, dr:dr + 1, :], (GRID_W, LANES))
        tiles.append(jnp.where(col_ok, pltpu.roll(base, 0, 1, stride=1, stride_axis=0) * LOG2E, neg))
    for g, (r_first, k_first) in enumerate(_na_block_geometry(rows)):
        for i in range(NA_QROWS):
            qr = r_first + i
            r0 = min(max(qr - wr // 2, 0), rows - wr)
            for jp in range(NA_KROWS // 2):
                halves = []
                for j in (2 * jp, 2 * jp + 1):
                    kr = k_first + j
                    halves.append(tiles[kr - qr + NA_WIN_R - 1] if r0 <= kr < r0 + wr else neg)
                o_ref[g, 0, i * GRID_W:(i + 1) * GRID_W, jp * LANES:(jp + 1) * LANES] = jnp.where(
                    lane < GRID_W, halves[0], halves[1])


def _na_bias_table(rpb, rows):
    wc = NA_WIN_C
    fill = jnp.full(rpb.shape[:2] + (GRID_W - (2 * wc - 1),), NEG, F32)
    ring = jnp.concatenate([rpb[..., wc - 1:], fill, rpb[..., :wc - 1]], axis=-1)
    ring = jnp.concatenate([ring, ring], axis=-1)
    nq, nk = NA_QROWS * GRID_W, NA_KROWS * GRID_W
    return pl.pallas_call(
        functools.partial(_na_bias_kernel, rows=rows),
        grid=(NA_HEADS,),
        in_specs=[pl.BlockSpec((1, 2 * NA_WIN_R - 1, LANES), lambda h: (h, 0, 0))],
        out_specs=pl.BlockSpec((3, 1, nq, nk), lambda h: (0, h, 0, 0)),
        out_shape=jax.ShapeDtypeStruct((3, NA_HEADS, nq, nk), F32),
        compiler_params=_params("parallel"),
        name="na_bias_table",
    )(ring)


def _na_kernel(q_ref, k_ref, v_ref, kc_ref, vc_ref, bias_ref, o_ref, *, rows):
    blk = pl.program_id(1)
    nq = NA_QROWS * GRID_W
    nk = NA_KROWS * GRID_W
    k_first = jnp.clip(blk * NA_QROWS - NA_WIN_R // 2, 0, rows - NA_KROWS)
    start = pl.multiple_of(k_first * GRID_W, GRID_W)
    lane = lax.broadcasted_iota(jnp.int32, (nq, LANES), 1)
    low = lane < NA_DH
    for p in range(NA_W // LANES):
        cs = slice(p * LANES, (p + 1) * LANES)
        q2 = q_ref[0, :, cs]
        k2 = k_ref[0, pl.ds(start, nk), cs]
        v2 = v_ref[0, pl.ds(start, nk), cs]
        kc2 = kc_ref[0, :, cs]
        vc2 = vc_ref[0, :, cs]
        zero = jnp.zeros_like(q2)
        qq = jnp.concatenate([jnp.where(low, q2, zero), jnp.where(low, zero, q2)], axis=0)
        bias = jnp.concatenate([bias_ref[0, 2 * p], bias_ref[0, 2 * p + 1]], axis=0)
        s_win = _dot_nt(qq, k2) + bias
        s_ctx = _dot_nt(qq, kc2)
        yield
        m = jnp.maximum(jnp.max(s_win, axis=-1, keepdims=True), jnp.max(s_ctx, axis=-1, keepdims=True))
        p_win = jnp.exp2(s_win - m)
        p_ctx = jnp.exp2(s_ctx - m)
        den = jnp.sum(p_win, axis=-1, keepdims=True) + jnp.sum(p_ctx, axis=-1, keepdims=True)
        yield
        o = (_dot(p_win, v2) + _dot(p_ctx, vc2)) / den
        o_ref[0, :, cs] = jnp.where(low, o[:nq], o[nq:]).astype(o_ref.dtype)
        yield


def _neighbourhood_attention(qkv, qkv_c, rpb):
    b, l, _ = qkv.shape
    lc = qkv_c.shape[1]
    rows = l // GRID_W
    assert rows % NA_QROWS == 0 and rows >= NA_KROWS + 1
    nq = NA_QROWS * GRID_W
    nblk = rows // NA_QROWS
    bias = _na_bias_table(rpb, rows)

    def cfg(bi, i):
        return (jnp.where(i == 0, 0, jnp.where(i == nblk - 1, 2, 1)), 0, 0, 0)

    return dict(
        kernel=functools.partial(_na_kernel, rows=rows),
        grid=(b, nblk),
        in_specs=[pl.BlockSpec((1, nq, NA_W), lambda bi, i: (bi, i, 0)),
                  pl.BlockSpec((1, l, NA_W), lambda bi, i: (bi, 0, 1)),
                  pl.BlockSpec((1, l, NA_W), lambda bi, i: (bi, 0, 2)),
                  pl.BlockSpec((1, lc, NA_W), lambda bi, i: (bi, 0, 1)),
                  pl.BlockSpec((1, lc, NA_W), lambda bi, i: (bi, 0, 2)),
                  pl.BlockSpec((1, NA_HEADS, nq, NA_KROWS * GRID_W), cfg)],
        out_specs=[pl.BlockSpec((1, nq, NA_W), lambda bi, i: (bi, i, 0))],
        out_shape=[jax.ShapeDtypeStruct((b, l, NA_W), BF16)],
        scratch_shapes=[],
        args=(qkv, qkv, qkv, qkv_c, qkv_c, bias))


HALO_ROWS = 16


def _halo_specs(tl, width, l):
    nbh = tl // HALO_ROWS
    last = l // HALO_ROWS - 1
    return [pl.BlockSpec((1, tl, width), lambda bi, i: (bi, i, 0)),
            pl.BlockSpec((1, HALO_ROWS, width), lambda bi, i: (bi, jnp.maximum(i * nbh - 1, 0), 0)),
            pl.BlockSpec((1, HALO_ROWS, width), lambda bi, i: (bi, jnp.minimum((i + 1) * nbh, last), 0))]


def _norm_modulate(x, g_ref, sh_ref, sc_ref):
    ms = jnp.mean(x * x, axis=-1, keepdims=True)
    return x * lax.rsqrt(ms + EPS) * g_ref[...] * (1.0 + sc_ref[0]) + sh_ref[0]


def _proj0_kernel(x_ref, xp_ref, xn_ref, g_ref, sh_ref, sc_ref, wna_ref, wdn_ref, wab_ref, wz_ref,
                  cw_ref, al_ref, dtb_ref, qkv_ref, dn_ref, gb_ref, z_ref):
    i = pl.program_id(1)
    tm = x_ref.shape[1]
    h = _norm_modulate(x_ref[0], g_ref, sh_ref, sc_ref)
    hb = h.astype(BF16)
    qkv_ref[0] = jnp.dot(hb, wna_ref[...], preferred_element_type=F32).astype(qkv_ref.dtype)
    z_ref[0] = jnp.dot(hb, wz_ref[...], preferred_element_type=F32).astype(z_ref.dtype)
    h_lo = (h - hb.astype(F32)).astype(BF16)
    ab2 = (jnp.dot(hb, wab_ref[...], preferred_element_type=F32)
           + jnp.dot(h_lo, wab_ref[...], preferred_element_type=F32))
    ab = ab2[:, :LANES] + ab2[:, LANES:]

    h_prev = jnp.where(i > 0, _norm_modulate(xp_ref[0], g_ref, sh_ref, sc_ref), 0.0)
    h_next = jnp.where(i < pl.num_programs(1) - 1, _norm_modulate(xn_ref[0], g_ref, sh_ref, sc_ref), 0.0)
    h_ext = jnp.concatenate([h_prev.astype(BF16), hb, h_next.astype(BF16)], axis=0)
    xe = jnp.dot(h_ext, wdn_ref[...], preferred_element_type=F32)
    half = DN_CONV // 2
    acc = jnp.zeros((tm, xe.shape[1]), F32)
    for j in range(DN_CONV):
        off = HALO_ROWS - half + j
        acc = acc + xe[off:off + tm] * cw_ref[j:j + 1, :]
    t = _silu(acc)
    segs = []
    for hh in range(3 * DN_HEADS):
        seg = t[:, hh * DN_DK:(hh + 1) * DN_DK]
        if hh < 2 * DN_HEADS:
            inv_norm = lax.rsqrt(jnp.sum(seg * seg, axis=-1, keepdims=True) + EPS)
            seg = seg * (inv_norm * (DN_DK ** -0.5) if hh < DN_HEADS else inv_norm)
        segs.append(seg)
    dn_ref[0] = jnp.concatenate(segs, axis=1).astype(dn_ref.dtype)
    zg = ab + dtb_ref[...]
    softplus = jnp.maximum(zg, 0.0) + jnp.log(1.0 + jnp.exp(-jnp.abs(zg)))
    g = -jnp.exp(al_ref[...]) * softplus
    lane = lax.broadcasted_iota(jnp.int32, ab.shape, 1)
    gb_ref[0] = jnp.where(lane < 2 * DN_HEADS, g, jax.nn.sigmoid(ab))


def _norm_proj0(x, norm_g, shift, scale, w_na, w_dn, w_ab, w_z, conv_w, a_log, dt_bias, tm):
    b, l, d = x.shape
    pad = LANES - 2 * DN_HEADS
    al = jnp.pad(a_log.reshape(1, 2 * DN_HEADS), ((0, 0), (0, pad)))
    dtb = jnp.pad(dt_bias.reshape(1, 2 * DN_HEADS), ((0, 0), (0, pad)))
    const = lambda a: pl.BlockSpec(a.shape, lambda bi, i: (0,) * a.ndim)
    vec = pl.BlockSpec((1, 1, d), lambda bi, i: (bi, 0, 0))
    tile = lambda n: pl.BlockSpec((1, tm, n), lambda bi, i: (bi, i, 0))
    consts = [norm_g.reshape(1, d)]
    w_ab_hi = w_ab.astype(BF16)
    w_ab2 = jnp.concatenate([w_ab_hi, (w_ab - w_ab_hi.astype(F32)).astype(BF16)], axis=1)
    weights = [w_na.astype(BF16), w_dn.astype(BF16), w_ab2, w_z.astype(BF16), conv_w, al, dtb]
    widths = [w_na.shape[1], w_dn.shape[1], LANES, w_z.shape[1]]
    dtypes = [BF16, BF16, F32, BF16]
    return pl.pallas_call(
        _proj0_kernel,
        grid=(b, l // tm),
        in_specs=_halo_specs(tm, d, l) + [const(consts[0]), vec, vec] + [const(w) for w in weights],
        out_specs=[tile(n) for n in widths],
        out_shape=[jax.ShapeDtypeStruct((b, l, n), dt) for n, dt in zip(widths, dtypes)],
        compiler_params=_params("parallel", "parallel"),
        name="norm_proj0",
    )(x, x, x, consts[0], shift, scale, *weights)


DN_INV_BASE = 16


def _unit_triangular_inverse(nil, eye, row, col):
    c = nil.shape[-1]
    sh = int(math.log2(DN_INV_BASE))
    diag = jnp.where((row >> sh) == (col >> sh), nil, 0.0)
    inv = eye - diag
    pw = diag
    for _ in range(sh - 1):
        pw = _bdot(pw, pw)
        yield
        inv = inv + _bdot(inv, pw)
        yield
    while (1 << sh) < c:
        off = jnp.where(((row >> (sh + 1)) == (col >> (sh + 1))) & ((row >> sh) != (col >> sh)), nil, 0.0)
        t = _bdot(off, inv)
        yield
        inv = inv - _bdot(inv, t)
        yield
        sh += 1
    return inv


def _bdot(a, b):
    return lax.dot_general(a.astype(BF16), b.astype(BF16), (((2,), (1,)), ((0,), (0,))),
                           preferred_element_type=F32)


def _bdot_nt(a, b):
    return lax.dot_general(a.astype(BF16), b.astype(BF16), (((2,), (2,)), ((0,), (0,))),
                           preferred_element_type=F32)


def _bdot_tn(a, b):
    return lax.dot_general(a.astype(BF16), b.astype(BF16), (((1,), (1,)), ((0,), (0,))),
                           preferred_element_type=F32)


DN_STEP_CHUNKS = 2


def _dn_scan_kernel(xf_ref, xb_ref, gf_ref, gbk_ref, s0_ref, of_ref, ob_ref, sfin_ref, s_ref, *, nsub):
    t = pl.program_id(1)
    c = DN_CHUNK
    nh = DN_HEADS
    nb = 2 * nh

    @pl.when(t == 0)
    def _():
        s_ref[...] = s0_ref[0]

    row = lax.broadcasted_iota(jnp.int32, (c, c), 0)
    col = lax.broadcasted_iota(jnp.int32, (c, c), 1)
    eye = (row == col).astype(F32)
    incl = jnp.stack([row >= col, row <= col])[None, :, None]
    strict = jnp.stack([row > col, row < col])[None, :, None]

    def masked(mask, a):
        return jnp.where(mask, a.reshape(nsub, 2, nh, c, c), 0.0).reshape(a.shape)

    rowg = lax.broadcasted_iota(jnp.int32, (c, LANES), 0)

    def chunk_rows(d, sq):
        first = (sq if d == 0 else nsub - 1 - sq) * c
        return slice(first, first + c)

    qs, ks, vs, gcs, betas = [], [], [], [], []
    for sq in range(nsub):
        for d, (x_ref, g_ref) in enumerate(((xf_ref, gf_ref), (xb_ref, gbk_ref))):
            rs = chunk_rows(d, sq)
            gb = g_ref[0, rs, :]
            cum = gb
            step = 1
            while step < c:
                if d == 0:
                    cum = cum + jnp.where(rowg >= step, pltpu.roll(cum, step, 0), 0.0)
                else:
                    cum = cum + jnp.where(rowg < c - step, pltpu.roll(cum, c - step, 0), 0.0)
                step *= 2
            for h in range(nh):
                qs.append(x_ref[0, rs, h * DN_DK:(h + 1) * DN_DK].astype(F32))
                ks.append(x_ref[0, rs, DN_W + h * DN_DK:DN_W + (h + 1) * DN_DK].astype(F32))
                vs.append(x_ref[0, rs, 2 * DN_W + h * DN_DK:2 * DN_W + (h + 1) * DN_DK].astype(F32))
                ci = d * nh + h
                gcs.append(jnp.broadcast_to(cum[:, ci:ci + 1], (c, c)))
                betas.append(gb[:, 2 * nh + ci:2 * nh + ci + 1])
    q, k, v, gc_rows, beta = (jnp.stack(a) for a in (qs, ks, vs, gcs, betas))
    gc_cols = jnp.swapaxes(gc_rows, 1, 2)
    decay = masked(incl, jnp.exp(masked(incl, gc_rows - gc_cols)))
    kb = k * beta
    vb = v * beta
    yield
    kq_k = _bdot_nt(jnp.concatenate([kb, q], axis=1), k)
    yield
    nil = masked(strict, kq_k[:, :c] * decay)
    aqk = kq_k[:, c:] * decay
    egc = jnp.exp(gc_rows)
    inv = yield from _unit_triangular_inverse(nil, eye, row, col)
    sol = _bdot(inv, jnp.concatenate([vb, kb * egc], axis=2))
    yield
    u, w = sol[:, :, :DN_DK], sol[:, :, DN_DK:]
    wq = jnp.concatenate([w, q * egc], axis=1)
    s = s_ref[...]
    for sq in range(nsub):
        e = slice(sq * nb, (sq + 1) * nb)
        gc = gc_rows[e]
        wq_s = _bdot(wq[e], s)
        yield
        v_new = u[e] - wq_s[:, :c]
        o = wq_s[:, c:] + _bdot(aqk[e], v_new)
        g_last = jnp.concatenate([gc[:nh, c - 1:c, :], gc[nh:, 0:1, :]], axis=0)
        s = s * jnp.exp(g_last) + _bdot_tn(k[e] * jnp.exp(g_last - gc), v_new)
        yield
        for h in range(nh):
            of_ref[0, chunk_rows(0, sq), h * DN_DK:(h + 1) * DN_DK] = o[h].astype(of_ref.dtype)
            ob_ref[0, chunk_rows(1, sq), h * DN_DK:(h + 1) * DN_DK] = o[nh + h].astype(ob_ref.dtype)
    s_ref[...] = s
    sfin_ref[0] = s


def _dn_scan(qkv, gb, s0):
    b, l, w = qkv.shape
    nsub = DN_STEP_CHUNKS if l % (DN_STEP_CHUNKS * DN_CHUNK) == 0 else 1
    rows = nsub * DN_CHUNK
    n = l // rows
    fwd = lambda bi, t: (bi, t, 0)
    bwd = lambda bi, t: (bi, n - 1 - t, 0)
    state = pl.BlockSpec((1, 2 * DN_HEADS, DN_DK, DN_DK), lambda bi, t: (bi, 0, 0, 0))
    return dict(
        kernel=functools.partial(_dn_scan_kernel, nsub=nsub),
        grid=(b, n),
        in_specs=[pl.BlockSpec((1, rows, w), fwd), pl.BlockSpec((1, rows, w), bwd),
                  pl.BlockSpec((1, rows, LANES), fwd), pl.BlockSpec((1, rows, LANES), bwd),
                  state],
        out_specs=[pl.BlockSpec((1, rows, DN_W), fwd), pl.BlockSpec((1, rows, DN_W), bwd),
                   state],
        out_shape=[jax.ShapeDtypeStruct((b, l, DN_W), BF16), jax.ShapeDtypeStruct((b, l, DN_W), BF16),
                   jax.ShapeDtypeStruct((b, 2 * DN_HEADS, DN_DK, DN_DK), F32)],
        scratch_shapes=[pltpu.VMEM((2 * DN_HEADS, DN_DK, DN_DK), F32)],
        args=(qkv, qkv, gb, gb, s0))


def _run_parts(name, *parts):
    grid = parts[0]["grid"]
    assert all(p["grid"] == grid for p in parts)
    n_in = [len(p["in_specs"]) for p in parts]
    n_out = [len(p["out_specs"]) for p in parts]
    n_scr = [len(p["scratch_shapes"]) for p in parts]

    def body(*refs):
        ins, outs, scr = refs[:sum(n_in)], refs[sum(n_in):sum(n_in) + sum(n_out)], refs[sum(n_in) + sum(n_out):]
        live = []
        for p, ni, no, ns in zip(parts, n_in, n_out, n_scr):
            live.append(p["kernel"](*ins[:ni], *outs[:no], *scr[:ns]))
            ins, outs, scr = ins[ni:], outs[no:], scr[ns:]
        while live:
            for g in list(live):
                if next(g, live) is live:
                    live.remove(g)

    res = pl.pallas_call(
        body,
        grid=grid,
        in_specs=[s for p in parts for s in p["in_specs"]],
        out_specs=[s for p in parts for s in p["out_specs"]],
        out_shape=[s for p in parts for s in p["out_shape"]],
        scratch_shapes=[s for p in parts for s in p["scratch_shapes"]],
        compiler_params=_params("parallel", "arbitrary"),
        name=name,
    )(*[a for p in parts for a in p["args"]])
    out, k = [], 0
    for no in n_out:
        out.append(res[k:k + no])
        k += no
    return out


def _out0_kernel(x_ref, na_ref, of_ref, ob_ref, z_ref, gate_ref, ng_ref, w_ref, o_ref):
    o = of_ref[0].astype(F32) + ob_ref[0].astype(F32)
    segs = [na_ref[0].astype(F32)]
    for h in range(DN_HEADS):
        seg = o[:, h * DN_DK:(h + 1) * DN_DK]
        ms = jnp.mean(seg * seg, axis=-1, keepdims=True)
        segs.append(seg * lax.rsqrt(ms + EPS) * ng_ref[...])
    mix = jnp.concatenate(segs, axis=1) * _silu(z_ref[0].astype(F32))
    o_ref[0] = x_ref[0] + gate_ref[0] * _dot(mix, w_ref[...])


def _out0(x, na, o_f, o_b, z, gate, norm_g, w_out, tm):
    b, l, d = x.shape
    tile = lambda n: pl.BlockSpec((1, tm, n), lambda bi, i: (bi, i, 0))
    return pl.pallas_call(
        _out0_kernel,
        grid=(b, l // tm),
        in_specs=[tile(d), tile(NA_W), tile(DN_W), tile(DN_W), tile(d),
                  pl.BlockSpec((1, 1, d), lambda bi, i: (bi, 0, 0)),
                  pl.BlockSpec((1, DN_DK), lambda bi, i: (0, 0)),
                  pl.BlockSpec(w_out.shape, lambda bi, i: (0, 0))],
        out_specs=tile(d),
        out_shape=jax.ShapeDtypeStruct((b, l, d), F32),
        compiler_params=_params("parallel", "parallel"),
        name="out_proj0",
    )(x, na, o_f, o_b, z, gate, norm_g.reshape(1, DN_DK), w_out.astype(BF16))


def _filter_kernel(feat_ref, decay_ref, w1_ref, b1_ref, f1_ref, w2_ref, b2_ref, f2_ref, w3a_ref, w3lo_ref,
                   h_ref, ss_ref):
    hid = jnp.sin(f1_ref[...] * (jnp.dot(feat_ref[...], w1_ref[...], precision=HI,
                                         preferred_element_type=F32) + b1_ref[...]))
    hid = jnp.sin(f2_ref[...] * (jnp.dot(hid, w2_ref[...], precision=HI,
                                         preferred_element_type=F32) + b2_ref[...]))
    hid_hi = hid.astype(BF16)
    hid_lo = (hid - hid_hi.astype(F32)).astype(BF16)
    h = (jnp.dot(jnp.concatenate([hid_hi, hid_lo], axis=1), w3a_ref[...], preferred_element_type=F32)
         + jnp.dot(hid_hi, w3lo_ref[...], preferred_element_type=F32))
    env = jnp.exp(-feat_ref[:, 0:1] * decay_ref[...])
    h = h * jnp.concatenate([env] * 4, axis=1)
    h_ref[...] = h.astype(h_ref.dtype)

    @pl.when(pl.program_id(0) == 0)
    def _():
        ss_ref[...] = jnp.zeros_like(ss_ref)

    ss_ref[...] += jnp.sum(h * h, axis=0, keepdims=True)


def _hyena_filters_raw(length, w1, b1, f1, w2, b2, f2, w3, tl):
    t = jnp.linspace(0.0, 1.0, length, dtype=F32)[:, None]
    wv = 2.0 * math.pi * jnp.arange(length, dtype=F32)[:, None] / length
    f = jnp.linspace(1e-4, HY_BANDS - 1, HY_BANDS, dtype=F32)[None, :]
    feats = jnp.concatenate([t, jnp.cos(f * wv), -jnp.sin(f * wv)], axis=-1)
    decay = jnp.abs(jnp.linspace(HY_DECAY_MIN, HY_DECAY_MAX, HY_W, dtype=F32))
    feats = _time_major_to_pq(feats[None])[0].reshape(length, -1)
    ffn = w1.shape[1]
    pe, pf = LANES - HY_EMB, LANES - ffn
    feats = jnp.pad(feats, ((0, 0), (0, pe)))
    w1p = jnp.pad(w1, ((0, pe), (0, pf)))
    w2p = jnp.pad(w2, ((0, pf), (0, pf)))
    w3p = jnp.pad(w3, ((0, pf), (0, 0)))
    w3_hi = w3p.astype(BF16)
    w3_lo = (w3p - w3_hi.astype(F32)).astype(BF16)
    w3a = jnp.concatenate([w3_hi, w3_hi], axis=0)
    vec = lambda a: jnp.pad(a.reshape(1, ffn), ((0, 0), (0, pf)))
    n_out = w3.shape[1]
    const = lambda shape: pl.BlockSpec(shape, lambda i: (0, 0))
    return pl.pallas_call(
        _filter_kernel,
        grid=(length // tl,),
        in_specs=[pl.BlockSpec((tl, LANES), lambda i: (i, 0)), const((1, HY_W)),
                  const((LANES, LANES)), const((1, LANES)), const((1, LANES)),
                  const((LANES, LANES)), const((1, LANES)), const((1, LANES)),
                  const((2 * LANES, n_out)), const((LANES, n_out))],
        out_specs=[pl.BlockSpec((tl, n_out), lambda i: (i, 0)), const((1, n_out))],
        out_shape=[jax.ShapeDtypeStruct((length, n_out), BF16), jax.ShapeDtypeStruct((1, n_out), F32)],
        compiler_params=_params("arbitrary"),
        name="hyena_filter_ffn",
    )(feats, decay.reshape(1, HY_W), w1p, vec(b1), vec(f1), w2p, vec(b2), vec(f2), w3a, w3_lo)


HY_P = 64
HY_CB = 2 * LANES
HY_KQ_PAD = 8
HY_ROW_PAD = 8


def _time_major_to_pq(a):
    b, l, c = a.shape
    return a.reshape(b, l // HY_P, HY_P, c).transpose(0, 2, 1, 3)


def _pq_to_time_major(a):
    b, p, q, c = a.shape
    return a.transpose(0, 2, 1, 3).reshape(b, p * q, c)


def _fft_tables(length):
    p_n = HY_P
    q_n = length // p_n
    qn2 = 2 * q_n
    n = 2 * length
    kq_n = q_n + HY_KQ_PAD
    kq = np.arange(kq_n)
    live = (kq <= q_n)[:, None]
    a1 = 2.0 * np.pi * ((kq[:, None] * np.arange(q_n)[None, :]) % qn2) / qn2
    c1, s1 = np.cos(a1) * live, np.sin(a1) * live
    g1 = np.concatenate([c1, -s1], axis=0)
    g1i = np.concatenate([c1.T, -s1.T], axis=1)
    bf = lambda m: jnp.asarray(m, F32).astype(BF16)
    pp = jnp.arange(p_n, dtype=jnp.int32)
    k = jnp.arange(kq_n, dtype=jnp.int32)[:, None, None] + qn2 * pp[None, :, None]
    ang = ((k * pp[None, None, :]) % n).astype(F32) * (2.0 * math.pi / n)
    c2, s2 = jnp.cos(ang), jnp.sin(ang)
    f2t = jnp.concatenate([jnp.concatenate([c2, s2], axis=2), jnp.concatenate([-s2, c2], axis=2)], axis=1)
    c2t, s2t = jnp.swapaxes(c2, 1, 2), jnp.swapaxes(s2, 1, 2)
    f2it = jnp.concatenate([jnp.concatenate([c2t, -s2t], axis=2), jnp.concatenate([s2t, c2t], axis=2)], axis=1)
    return bf(g1), bf(g1i), f2t.astype(BF16), f2it.astype(BF16)


def _conv3_slabs(ref, w_ref):
    p_n, q_n = ref.shape[1], ref.shape[2]
    w = w_ref[...]
    zero = jnp.zeros((1, ref.shape[3]), F32)

    def raw(p):
        return ref[0, p].astype(F32)

    def slab(p):
        prev = raw(p - 1) if p > 0 else jnp.concatenate([zero, raw(p_n - 1)[:q_n - 1]], axis=0)
        nxt = raw(p + 1) if p < p_n - 1 else jnp.concatenate([raw(0)[1:], zero], axis=0)
        return prev * w[0:1, :] + raw(p) * w[1:2, :] + nxt * w[2:3, :]

    return slab


def _lane_slabs(n):
    return [slice(i * LANES, (i + 1) * LANES) for i in range(n)]


def _fft_forward(slab, p_n, g1_ref, sre_ref, sim_ref):
    kq_n = g1_ref.shape[0] // 2
    pitch = p_n + HY_ROW_PAD
    for p in range(p_n):
        a = jnp.dot(g1_ref[...], slab(p).astype(BF16), preferred_element_type=F32)
        for i, lanes in enumerate(_lane_slabs(sre_ref.shape[0])):
            sre_ref[i, pl.ds(p, kq_n, stride=pitch), :] = a[:kq_n, lanes]
            sim_ref[i, pl.ds(p, kq_n, stride=pitch), :] = a[kq_n:, lanes]


def _fft_slab_spectrum(kq, f2t_ref, sre_ref, sim_ref, p_n):
    base = kq * (p_n + HY_ROW_PAD)
    ns = sre_ref.shape[0]
    slab = jnp.concatenate(
        [jnp.concatenate([ref[i, base:base + p_n, :] for i in range(ns)], axis=1) for ref in (sre_ref, sim_ref)],
        axis=0)
    return base, jnp.dot(f2t_ref[kq], slab.astype(BF16), preferred_element_type=F32)


def _fftconv_kernel(x_ref, gate_ref, xw_ref, gw_ref, h_ref, g1_ref, g1i_ref, f2t_ref, f2it_ref, o_ref,
                    sre_ref, sim_ref, *, conv_x):
    p_n, q_n = x_ref.shape[1], x_ref.shape[2]
    kq_n = g1_ref.shape[0] // 2
    pitch = p_n + HY_ROW_PAD
    ns = sre_ref.shape[0]
    x_slab = _conv3_slabs(x_ref, xw_ref) if conv_x else (lambda p: x_ref[0, p])
    gate_slab = _conv3_slabs(gate_ref, gw_ref)
    _fft_forward(x_slab, p_n, g1_ref, sre_ref, sim_ref)

    for kq in range(q_n + 1):
        base, x = _fft_slab_spectrum(kq, f2t_ref, sre_ref, sim_ref, p_n)
        xr, xi = x[:p_n], x[p_n:]
        h = h_ref[0, kq].astype(F32)
        hr, hi = h[:p_n], h[p_n:]
        y = jnp.concatenate([xr * hr - xi * hi, xr * hi + xi * hr], axis=0)
        z = jnp.dot(f2it_ref[kq], y.astype(BF16), preferred_element_type=F32)
        for i, lanes in enumerate(_lane_slabs(ns)):
            sre_ref[i, base:base + p_n, :] = z[:p_n, lanes]
            sim_ref[i, base:base + p_n, :] = z[p_n:, lanes]

    for p in range(p_n):
        z = jnp.concatenate(
            [jnp.concatenate([ref[i, pl.ds(p, kq_n, stride=pitch), :] for i in range(ns)], axis=1)
             for ref in (sre_ref, sim_ref)], axis=0)
        y = jnp.dot(g1i_ref[...], z.astype(BF16), preferred_element_type=F32)
        o_ref[0, p] = (gate_slab(p) * y).astype(o_ref.dtype)


def _filter_response_kernel(hf_ref, hb_ref, ssf_ref, ssb_ref, skip_ref, g1_ref, f2t_ref, o_ref,
                            sre_ref, sim_ref, spec_ref, *, n):
    p_n, q_n = hf_ref.shape[1], hf_ref.shape[2]
    nf = lax.rsqrt(ssf_ref[...] + EPS)
    nb = lax.rsqrt(ssb_ref[...] + EPS)
    _fft_forward(lambda p: hf_ref[0, p], p_n, g1_ref, sre_ref, sim_ref)
    for kq in range(q_n + 1):
        spec_ref[kq] = _fft_slab_spectrum(kq, f2t_ref, sre_ref, sim_ref, p_n)[1]
    _fft_forward(lambda p: hb_ref[0, p], p_n, g1_ref, sre_ref, sim_ref)
    for kq in range(q_n + 1):
        xb = _fft_slab_spectrum(kq, f2t_ref, sre_ref, sim_ref, p_n)[1]
        xf = spec_ref[kq]
        coef = (1.0 if kq in (0, q_n) else 2.0) / n
        hr = (xf[:p_n] * nf + xb[:p_n] * nb + skip_ref[0]) * coef
        hi = (xf[p_n:] * nf - xb[p_n:] * nb) * coef
        o_ref[0, kq] = jnp.concatenate([hr, hi], axis=0).astype(o_ref.dtype)


def _fft_scratch(kq_n, p_n):
    rows = kq_n * (p_n + HY_ROW_PAD)
    ns = HY_CB // LANES
    return [pltpu.VMEM((ns, rows, LANES), F32), pltpu.VMEM((ns, rows, LANES), F32)]


def _const_spec(a):
    return pl.BlockSpec(a.shape, lambda *_: (0,) * a.ndim)


def _filter_response(h_pq, ss, skip, tables):
    g1, _, f2t, _ = tables
    _, p_n, q_n, _ = h_pq.shape
    kq_n = g1.shape[0] // 2
    nb = HY_W // HY_CB
    taps = lambda d: pl.BlockSpec((1, p_n, q_n, HY_CB), lambda o, j: (0, 0, 0, (2 * o + d) * nb + j))
    sumsq = lambda d: pl.BlockSpec((1, HY_CB), lambda o, j: (0, (2 * o + d) * nb + j))
    return pl.pallas_call(
        functools.partial(_filter_response_kernel, n=2 * p_n * q_n),
        grid=(2, nb),
        in_specs=[taps(0), taps(1), sumsq(0), sumsq(1), pl.BlockSpec((1, 1, HY_CB), lambda o, j: (o, 0, j)),
                  _const_spec(g1), _const_spec(f2t)],
        out_specs=pl.BlockSpec((1, q_n + 1, 2 * p_n, HY_CB), lambda o, j: (o, 0, 0, j)),
        out_shape=jax.ShapeDtypeStruct((2, q_n + 1, 2 * p_n, HY_W), BF16),
        scratch_shapes=_fft_scratch(kq_n, p_n) + [pltpu.VMEM((q_n + 1, 2 * p_n, HY_CB), F32)],
        compiler_params=_params("parallel", "parallel"),
        name="hyena_filter_response",
    )(h_pq, h_pq, ss, ss, skip.reshape(2, 1, HY_W), g1, f2t)


def _long_conv_gated(u, u_block0, conv_x, p_u, gate_block0, conv_w, resp, order, tables):
    g1, g1i, f2t, f2it = tables
    b, p_n, q_n, _ = u.shape
    kq_n = g1.shape[0] // 2
    blk = lambda off: pl.BlockSpec((1, p_n, q_n, HY_CB), lambda j, bi: (bi, 0, 0, off + j))
    taps = lambda off: pl.BlockSpec((3, HY_CB), lambda j, bi: (0, off + j))
    return pl.pallas_call(
        functools.partial(_fftconv_kernel, conv_x=conv_x),
        grid=(HY_W // HY_CB, b),
        in_specs=[blk(u_block0), blk(gate_block0), taps(u_block0 if conv_x else 0), taps(gate_block0),
                  pl.BlockSpec((1, q_n + 1, 2 * p_n, HY_CB), lambda j, bi: (order, 0, 0, j)),
                  _const_spec(g1), _const_spec(g1i), _const_spec(f2t), _const_spec(f2it)],
        out_specs=blk(0),
        out_shape=jax.ShapeDtypeStruct((b, p_n, q_n, HY_W), BF16),
        scratch_shapes=_fft_scratch(kq_n, p_n),
        compiler_params=_params("parallel", "parallel"),
        name="hyena_fftconv",
    )(u, p_u, conv_w, conv_w, resp, g1, g1i, f2t, f2it)


def _out1_kernel(x_ref, y_ref, z_ref, gate_ref, w_ref, fg_ref, o_ref):
    r = x_ref[0] + gate_ref[0] * _dot(y_ref[0].astype(F32) * _silu(z_ref[0].astype(F32)), w_ref[...])
    ms = jnp.mean(r * r, axis=-1, keepdims=True)
    o_ref[0] = r * lax.rsqrt(ms + EPS) * fg_ref[...]


def _out1(x, y, z, gate, w_out, final_g, tm):
    b, l, d = x.shape
    tile = pl.BlockSpec((1, tm, d), lambda bi, i: (bi, i, 0))
    return pl.pallas_call(
        _out1_kernel,
        grid=(b, l // tm),
        in_specs=[tile, tile, tile, pl.BlockSpec((1, 1, d), lambda bi, i: (bi, 0, 0)),
                  pl.BlockSpec(w_out.shape, lambda bi, i: (0, 0)), pl.BlockSpec((1, d), lambda bi, i: (0, 0))],
        out_specs=tile,
        out_shape=jax.ShapeDtypeStruct((b, l, d), F32),
        compiler_params=_params("parallel", "parallel"),
        name="out_proj1",
    )(x, y, z, gate, w_out.astype(BF16), final_g.reshape(1, d))


def _row_tile(l, want):
    return want if l % want == 0 else l


def _even_layer(x, ctx, c, c_ctx, norm_g, mod_w, mod_b, w_in, rpb, dn_conv, a_log, dt_bias, dn_norm_g, w_out):
    b, l, d = x.shape
    lc = ctx.shape[1]
    off_dn = 3 * NA_W
    off_ab = off_dn + DN_CONV_W
    off_z = off_ab + 4 * DN_HEADS
    rows = b + 1
    pad = (-rows) % 8
    cvecs = jnp.concatenate([c, c_ctx[None, :], jnp.zeros((pad, d), F32)], axis=0)
    m = _modulation(cvecs, mod_w, mod_b)
    shift, scale, gate = (m[:, i * d:(i + 1) * d] for i in range(3))
    lat = lambda a: a[:b, None, :]
    cx = lambda a: jnp.broadcast_to(a[b:b + 1, None, :], (b, 1, d))

    w_na = jnp.concatenate([w_in[:, :NA_W] * (NA_DH ** -0.5 * LOG2E), w_in[:, NA_W:off_dn]], axis=1)
    w_dn = w_in[:, off_dn:off_ab]
    w_ab = jnp.pad(w_in[:, off_ab:off_z], ((0, 0), (0, LANES - 4 * DN_HEADS)))
    w_z = w_in[:, off_z:]
    qkv_x, dn_x, gb_x, z_x = _norm_proj0(x, norm_g, lat(shift), lat(scale), w_na, w_dn, w_ab, w_z,
                                         dn_conv, a_log, dt_bias, _row_tile(l, 512))
    qkv_c, dn_c, gb_c, _ = _norm_proj0(ctx, norm_g, cx(shift), cx(scale), w_na, w_dn, w_ab, w_z,
                                       dn_conv, a_log, dt_bias, _row_tile(lc, 256))

    s0 = jnp.zeros((b, 2 * DN_HEADS, DN_DK, DN_DK), F32)
    ((_, _, s_ctx),) = _run_parts("dn_scan_ctx", _dn_scan(dn_c, gb_c, s0))
    (o_f, o_b, _), (na_x,) = _run_parts("dn_scan_and_attention", _dn_scan(dn_x, gb_x, s_ctx),
                                        _neighbourhood_attention(qkv_x, qkv_c, rpb))

    return _out0(x, na_x, o_f, o_b, z_x, lat(gate), dn_norm_g, w_out, _row_tile(l, 512))


def _hyena_layer(x, c, norm_g, mod_w, mod_b, w_in, conv_w, fw1, fb1, ff1, fw2, fb2, ff2, fw3, skip, w_out,
                 final_g):
    b, l, d = x.shape
    pad = (-b) % 8
    cvecs = jnp.concatenate([c, jnp.zeros((pad, d), F32)], axis=0)
    m = _modulation(cvecs, mod_w, mod_b)
    shift, scale, gate = (m[:b, None, i * d:(i + 1) * d] for i in range(3))
    q_n = l // HY_P
    xt = _time_major_to_pq(x).reshape(b, l, d)
    p_u, gz = _norm_proj(xt, norm_g, shift, scale,
                         [(w_in[:, :3 * HY_W], "bf16", BF16), (w_in[:, 3 * HY_W:], "bf16", BF16)],
                         _row_tile(l, 1024))
    p_u = p_u.reshape(b, HY_P, q_n, 3 * HY_W)

    h_raw, ss = _hyena_filters_raw(l, fw1, fb1, ff1, fw2, fb2, ff2, fw3, _row_tile(l, 256))
    tables = _fft_tables(l)
    resp = _filter_response(h_raw.reshape(1, HY_P, q_n, h_raw.shape[1]), ss, skip, tables)

    nb = HY_W // HY_CB
    z = _long_conv_gated(p_u, 0, True, p_u, nb, conv_w, resp, 0, tables)
    y = _long_conv_gated(z, 0, False, p_u, 2 * nb, conv_w, resp, 1, tables)
    out = _out1(xt, y.reshape(b, l, HY_W), gz, gate, w_out, final_g, _row_tile(l, 512))
    return _pq_to_time_major(out.reshape(b, HY_P, q_n, d))


def kernel(x, c, ctx, c_ctx, e_norm_g, e_mod_w, e_mod_b, e_w_in, e_na_rpb, e_dn_conv, e_dn_a_log, e_dn_dt_bias, e_dn_norm_g, e_w_out, o_norm_g, o_mod_w, o_mod_b, o_w_in, o_hy_conv, o_ffn_w1, o_ffn_b1, o_ffn_f1, o_ffn_w2, o_ffn_b2, o_ffn_f2, o_ffn_w3, o_hy_skip, o_w_out, final_norm_g):
    x = _even_layer(x, ctx, c, c_ctx, e_norm_g[0], e_mod_w[0], e_mod_b[0], e_w_in[0], e_na_rpb[0],
                    e_dn_conv[0], e_dn_a_log[0], e_dn_dt_bias[0], e_dn_norm_g[0], e_w_out[0])
    return _hyena_layer(x, c, o_norm_g[0], o_mod_w[0], o_mod_b[0], o_w_in[0], o_hy_conv[0],
                        o_ffn_w1[0], o_ffn_b1[0], o_ffn_f1[0], o_ffn_w2[0], o_ffn_b2[0], o_ffn_f2[0],
                        o_ffn_w3[0], o_hy_skip[0], o_w_out[0], final_norm_g)
```

```python
import functools
import math

import numpy as np
import jax
import jax.numpy as jnp
from jax import lax
from jax.experimental import pallas as pl
from jax.experimental.pallas import tpu as pltpu

F32 = jnp.float32
BF16 = jnp.bfloat16
HI = lax.Precision.HIGHEST
EPS = 1e-6
NEG = -1e30
LOG2E = math.log2(math.e)

LANES = 128
VMEM_LIMIT_BYTES = 56 * 1024 * 1024

GRID_W = 64
NA_HEADS = 8
NA_DH = 64
NA_W = NA_HEADS * NA_DH
NA_WIN_R = 8
NA_WIN_C = 16
NA_QROWS = 4
NA_KROWS = NA_QROWS + NA_WIN_R

DN_HEADS = 4
DN_DK = 128
DN_W = DN_HEADS * DN_DK
DN_CONV_W = 3 * DN_W
DN_CONV = 5
DN_CHUNK = 128

HY_W = 1024
HY_EMB = 33
HY_BANDS = (HY_EMB - 1) // 2
HY_DECAY_MIN = math.log(1e-2) / 1.5
HY_DECAY_MAX = math.log(1e-2) / 0.3


def _params(*sem):
    return pltpu.CompilerParams(dimension_semantics=sem, vmem_limit_bytes=VMEM_LIMIT_BYTES)


def _silu(v):
    return v * jax.nn.sigmoid(v)


def _dot(a, b):
    return jnp.dot(a.astype(BF16), b.astype(BF16), preferred_element_type=F32)


def _dot_nt(a, b):
    return lax.dot_general(a.astype(BF16), b.astype(BF16), (((1,), (1,)), ((), ())),
                           preferred_element_type=F32)


def _mod_kernel(c_ref, w_ref, b_ref, o_ref):
    o_ref[...] = jnp.dot(_silu(c_ref[...]), w_ref[...], precision=HI,
                         preferred_element_type=F32) + b_ref[...]


def _modulation(cvecs, w, b):
    r, d = cvecs.shape
    n = w.shape[1]
    tn = 512
    return pl.pallas_call(
        _mod_kernel,
        grid=(n // tn,),
        in_specs=[pl.BlockSpec((r, d), lambda j: (0, 0)),
                  pl.BlockSpec((d, tn), lambda j: (0, j)),
                  pl.BlockSpec((1, tn), lambda j: (0, j))],
        out_specs=pl.BlockSpec((r, tn), lambda j: (0, j)),
        out_shape=jax.ShapeDtypeStruct((r, n), F32),
        compiler_params=_params("parallel"),
        name="adaln_mod",
    )(cvecs, w, b.reshape(1, n))


def _norm_modulate(x, g_ref, sh_ref, sc_ref):
    ms = jnp.mean(x * x, axis=-1, keepdims=True)
    return x * lax.rsqrt(ms + EPS) * g_ref[...] * (1.0 + sc_ref[0]) + sh_ref[0]


def _proj_kernel(x_ref, g_ref, sh_ref, sc_ref, *refs):
    n = len(refs) // 2
    hb = _norm_modulate(x_ref[0], g_ref, sh_ref, sc_ref).astype(BF16)
    for w_ref, o_ref in zip(refs[:n], refs[n:]):
        o_ref[0] = jnp.dot(hb, w_ref[...], preferred_element_type=F32).astype(o_ref.dtype)


def _norm_proj(x, norm_g, shift, scale, sections, tm):
    b, l, d = x.shape
    ws, out_shapes, out_specs, w_specs = [], [], [], []
    for w, odt in sections:
        ws.append(w.astype(BF16))
        n = w.shape[1]
        w_specs.append(pl.BlockSpec((d, n), lambda bi, i: (0, 0)))
        out_specs.append(pl.BlockSpec((1, tm, n), lambda bi, i: (bi, i, 0)))
        out_shapes.append(jax.ShapeDtypeStruct((b, l, n), odt))
    vec = pl.BlockSpec((1, 1, d), lambda bi, i: (bi, 0, 0))
    return pl.pallas_call(
        _proj_kernel,
        grid=(b, l // tm),
        in_specs=[pl.BlockSpec((1, tm, d), lambda bi, i: (bi, i, 0)),
                  pl.BlockSpec((1, d), lambda bi, i: (0, 0)), vec, vec] + w_specs,
        out_specs=out_specs,
        out_shape=out_shapes,
        compiler_params=_params("parallel", "parallel"),
        name="norm_proj",
    )(x, norm_g.reshape(1, d), shift, scale, *ws)


def _na_block_geometry(rows):
    return ((0, 0), (NA_QROWS, 0), (rows - NA_QROWS, rows - NA_KROWS))


def _na_bias_kernel(rc_ref, o_ref, *, rows):
    wr = min(NA_WIN_R, rows)
    qc = lax.broadcasted_iota(jnp.int32, (GRID_W, LANES), 0)
    lane = lax.broadcasted_iota(jnp.int32, (GRID_W, LANES), 1)
    kc = lane & (GRID_W - 1)
    c0 = jnp.clip(qc - NA_WIN_C // 2, 0, GRID_W - NA_WIN_C)
    col_ok = (kc >= c0) & (kc < c0 + NA_WIN_C)
    neg = jnp.full((GRID_W, LANES), NEG, F32)
    tiles = []
    for dr in range(2 * NA_WIN_R - 1):
        base = jnp.broadcast_to(rc_ref[0, dr:dr + 1, :], (GRID_W, LANES))
        tiles.append(jnp.where(col_ok, pltpu.roll(base, 0, 1, stride=1, stride_axis=0) * LOG2E, neg))
    for g, (r_first, k_first) in enumerate(_na_block_geometry(rows)):
        for i in range(NA_QROWS):
            qr = r_first + i
            r0 = min(max(qr - wr // 2, 0), rows - wr)
            for jp in range(NA_KROWS // 2):
                halves = []
                for j in (2 * jp, 2 * jp + 1):
                    kr = k_first + j
                    halves.append(tiles[kr - qr + NA_WIN_R - 1] if r0 <= kr < r0 + wr else neg)
                o_ref[g, 0, i * GRID_W:(i + 1) * GRID_W, jp * LANES:(jp + 1) * LANES] = jnp.where(
                    lane < GRID_W, halves[0], halves[1])


def _na_bias_table(rpb, rows):
    wc = NA_WIN_C
    fill = jnp.full(rpb.shape[:2] + (GRID_W - (2 * wc - 1),), NEG, F32)
    ring = jnp.concatenate([rpb[..., wc - 1:], fill, rpb[..., :wc - 1]], axis=-1)
    ring = jnp.concatenate([ring, ring], axis=-1)
    nq, nk = NA_QROWS * GRID_W, NA_KROWS * GRID_W
    return pl.pallas_call(
        functools.partial(_na_bias_kernel, rows=rows),
        grid=(NA_HEADS,),
        in_specs=[pl.BlockSpec((1, 2 * NA_WIN_R - 1, LANES), lambda h: (h, 0, 0))],
        out_specs=pl.BlockSpec((3, 1, nq, nk), lambda h: (0, h, 0, 0)),
        out_shape=jax.ShapeDtypeStruct((3, NA_HEADS, nq, nk), F32),
        compiler_params=_params("parallel"),
        name="na_bias_table",
    )(ring)


def _na_kernel(q_ref, k_ref, v_ref, kc_ref, vc_ref, bias_ref, o_ref, *, rows):
    blk = pl.program_id(1)
    nq = NA_QROWS * GRID_W
    nk = NA_KROWS * GRID_W
    k_first = jnp.clip(blk * NA_QROWS - NA_WIN_R // 2, 0, rows - NA_KROWS)
    start = pl.multiple_of(k_first * GRID_W, GRID_W)
    lane = lax.broadcasted_iota(jnp.int32, (nq, LANES), 1)
    low = lane < NA_DH
    for p in range(NA_W // LANES):
        cs = slice(p * LANES, (p + 1) * LANES)
        q2 = q_ref[0, :, cs]
        k2 = k_ref[0, pl.ds(start, nk), cs]
        v2 = v_ref[0, pl.ds(start, nk), cs]
        kc2 = kc_ref[0, :, cs]
        vc2 = vc_ref[0, :, cs]
        zero = jnp.zeros_like(q2)
        qq = jnp.concatenate([jnp.where(low, q2, zero), jnp.where(low, zero, q2)], axis=0)
        bias = jnp.concatenate([bias_ref[0, 2 * p], bias_ref[0, 2 * p + 1]], axis=0)
        s_win = _dot_nt(qq, k2) + bias
        s_ctx = _dot_nt(qq, kc2)
        yield
        m = jnp.maximum(jnp.max(s_win, axis=-1, keepdims=True), jnp.max(s_ctx, axis=-1, keepdims=True))
        p_win = jnp.exp2(s_win - m)
        p_ctx = jnp.exp2(s_ctx - m)
        den = jnp.sum(p_win, axis=-1, keepdims=True) + jnp.sum(p_ctx, axis=-1, keepdims=True)
        yield
        o = (_dot(p_win, v2) + _dot(p_ctx, vc2)) / den
        o_ref[0, :, cs] = jnp.where(low, o[:nq], o[nq:]).astype(o_ref.dtype)
        yield


def _neighbourhood_attention(qkv, qkv_c, rpb):
    b, l, _ = qkv.shape
    lc = qkv_c.shape[1]
    rows = l // GRID_W
    assert rows % NA_QROWS == 0 and rows >= NA_KROWS + 1
    nq = NA_QROWS * GRID_W
    nblk = rows // NA_QROWS
    bias = _na_bias_table(rpb, rows)

    def cfg(bi, i):
        return (jnp.where(i == 0, 0, jnp.where(i == nblk - 1, 2, 1)), 0, 0, 0)

    return dict(
        kernel=functools.partial(_na_kernel, rows=rows),
        grid=(b, nblk),
        in_specs=[pl.BlockSpec((1, nq, NA_W), lambda bi, i: (bi, i, 0)),
                  pl.BlockSpec((1, l, NA_W), lambda bi, i: (bi, 0, 1)),
                  pl.BlockSpec((1, l, NA_W), lambda bi, i: (bi, 0, 2)),
                  pl.BlockSpec((1, lc, NA_W), lambda bi, i: (bi, 0, 1)),
                  pl.BlockSpec((1, lc, NA_W), lambda bi, i: (bi, 0, 2)),
                  pl.BlockSpec((1, NA_HEADS, nq, NA_KROWS * GRID_W), cfg)],
        out_specs=[pl.BlockSpec((1, nq, NA_W), lambda bi, i: (bi, i, 0))],
        out_shape=[jax.ShapeDtypeStruct((b, l, NA_W), BF16)],
        scratch_shapes=[],
        args=(qkv, qkv, qkv, qkv_c, qkv_c, bias))


HALO_ROWS = 16


def _halo_specs(tl, width, l):
    nbh = tl // HALO_ROWS
    last = l // HALO_ROWS - 1
    return [pl.BlockSpec((1, tl, width), lambda bi, i: (bi, i, 0)),
            pl.BlockSpec((1, HALO_ROWS, width), lambda bi, i: (bi, jnp.maximum(i * nbh - 1, 0), 0)),
            pl.BlockSpec((1, HALO_ROWS, width), lambda bi, i: (bi, jnp.minimum((i + 1) * nbh, last), 0))]


def _proj0_kernel(x_ref, xp_ref, xn_ref, g_ref, sh_ref, sc_ref, wna_ref, wdn_ref, wab_ref, wz_ref,
                  cw_ref, al_ref, dtb_ref, qkv_ref, dn_ref, gb_ref, z_ref):
    i = pl.program_id(1)
    tm = x_ref.shape[1]
    h = _norm_modulate(x_ref[0], g_ref, sh_ref, sc_ref)
    hb = h.astype(BF16)
    qkv_ref[0] = jnp.dot(hb, wna_ref[...], preferred_element_type=F32).astype(qkv_ref.dtype)
    z_ref[0] = jnp.dot(hb, wz_ref[...], preferred_element_type=F32).astype(z_ref.dtype)
    h_lo = (h - hb.astype(F32)).astype(BF16)
    ab2 = (jnp.dot(hb, wab_ref[...], preferred_element_type=F32)
           + jnp.dot(h_lo, wab_ref[...], preferred_element_type=F32))
    ab = ab2[:, :LANES] + ab2[:, LANES:]

    h_prev = jnp.where(i > 0, _norm_modulate(xp_ref[0], g_ref, sh_ref, sc_ref), 0.0)
    h_next = jnp.where(i < pl.num_programs(1) - 1, _norm_modulate(xn_ref[0], g_ref, sh_ref, sc_ref), 0.0)
    h_ext = jnp.concatenate([h_prev.astype(BF16), hb, h_next.astype(BF16)], axis=0)
    xe = jnp.dot(h_ext, wdn_ref[...], preferred_element_type=F32)
    half = DN_CONV // 2
    acc = jnp.zeros((tm, xe.shape[1]), F32)
    for j in range(DN_CONV):
        off = HALO_ROWS - half + j
        acc = acc + xe[off:off + tm] * cw_ref[j:j + 1, :]
    t = _silu(acc)
    segs = []
    for hh in range(3 * DN_HEADS):
        seg = t[:, hh * DN_DK:(hh + 1) * DN_DK]
        if hh < 2 * DN_HEADS:
            inv_norm = lax.rsqrt(jnp.sum(seg * seg, axis=-1, keepdims=True) + EPS)
            seg = seg * (inv_norm * (DN_DK ** -0.5) if hh < DN_HEADS else inv_norm)
        segs.append(seg)
    dn_ref[0] = jnp.concatenate(segs, axis=1).astype(dn_ref.dtype)
    zg = ab + dtb_ref[...]
    softplus = jnp.maximum(zg, 0.0) + jnp.log(1.0 + jnp.exp(-jnp.abs(zg)))
    g = -jnp.exp(al_ref[...]) * softplus
    lane = lax.broadcasted_iota(jnp.int32, ab.shape, 1)
    gb_ref[0] = jnp.where(lane < 2 * DN_HEADS, g, jax.nn.sigmoid(ab))


def _norm_proj0(x, norm_g, shift, scale, w_na, w_dn, w_ab, w_z, conv_w, a_log, dt_bias, tm):
    b, l, d = x.shape
    pad = LANES - 2 * DN_HEADS
    al = jnp.pad(a_log.reshape(1, 2 * DN_HEADS), ((0, 0), (0, pad)))
    dtb = jnp.pad(dt_bias.reshape(1, 2 * DN_HEADS), ((0, 0), (0, pad)))
    const = lambda a: pl.BlockSpec(a.shape, lambda bi, i: (0,) * a.ndim)
    vec = pl.BlockSpec((1, 1, d), lambda bi, i: (bi, 0, 0))
    tile = lambda n: pl.BlockSpec((1, tm, n), lambda bi, i: (bi, i, 0))
    consts = [norm_g.reshape(1, d)]
    w_ab_hi = w_ab.astype(BF16)
    w_ab2 = jnp.concatenate([w_ab_hi, (w_ab - w_ab_hi.astype(F32)).astype(BF16)], axis=1)
    weights = [w_na.astype(BF16), w_dn.astype(BF16), w_ab2, w_z.astype(BF16), conv_w, al, dtb]
    widths = [w_na.shape[1], w_dn.shape[1], LANES, w_z.shape[1]]
    dtypes = [BF16, BF16, F32, BF16]
    return pl.pallas_call(
        _proj0_kernel,
        grid=(b, l // tm),
        in_specs=_halo_specs(tm, d, l) + [const(consts[0]), vec, vec] + [const(w) for w in weights],
        out_specs=[tile(n) for n in widths],
        out_shape=[jax.ShapeDtypeStruct((b, l, n), dt) for n, dt in zip(widths, dtypes)],
        compiler_params=_params("parallel", "parallel"),
        name="norm_proj0",
    )(x, x, x, consts[0], shift, scale, *weights)


DN_INV_BASE = 4


def _unit_triangular_inverse(nil, eye, row, col):
    c = nil.shape[-1]
    sh = int(math.log2(DN_INV_BASE))
    diag = jnp.where((row >> sh) == (col >> sh), nil, 0.0)
    inv = eye - diag
    pw = diag
    for _ in range(sh - 1):
        pw = _bdot(pw, pw)
        yield
        inv = inv + _bdot(inv, pw)
        yield
    while (1 << sh) < c:
        off = jnp.where(((row >> (sh + 1)) == (col >> (sh + 1))) & ((row >> sh) != (col >> sh)), nil, 0.0)
        t = _bdot(off, inv)
        yield
        inv = inv - _bdot(inv, t)
        yield
        sh += 1
    return inv


def _bdot(a, b):
    return lax.dot_general(a.astype(BF16), b.astype(BF16), (((2,), (1,)), ((0,), (0,))),
                           preferred_element_type=F32)


def _bdot_nt(a, b):
    return lax.dot_general(a.astype(BF16), b.astype(BF16), (((2,), (2,)), ((0,), (0,))),
                           preferred_element_type=F32)


def _bdot_tn(a, b):
    return lax.dot_general(a.astype(BF16), b.astype(BF16), (((1,), (1,)), ((0,), (0,))),
                           preferred_element_type=F32)


DN_STEP_CHUNKS = 2


def _dn_scan_kernel(xf_ref, xb_ref, gf_ref, gbk_ref, s0_ref, of_ref, ob_ref, sfin_ref, s_ref, *, nsub):
    t = pl.program_id(1)
    c = DN_CHUNK
    nh = DN_HEADS
    nb = 2 * nh

    @pl.when(t == 0)
    def _():
        s_ref[...] = s0_ref[0]

    row = lax.broadcasted_iota(jnp.int32, (c, c), 0)
    col = lax.broadcasted_iota(jnp.int32, (c, c), 1)
    eye = (row == col).astype(F32)
    incl = jnp.stack([row >= col, row <= col])[None, :, None]
    strict = jnp.stack([row > col, row < col])[None, :, None]

    def masked(mask, a):
        return jnp.where(mask, a.reshape(nsub, 2, nh, c, c), 0.0).reshape(a.shape)

    rowg = lax.broadcasted_iota(jnp.int32, (c, LANES), 0)

    def chunk_rows(d, sq):
        first = (sq if d == 0 else nsub - 1 - sq) * c
        return slice(first, first + c)

    qs, ks, vs, gcs, betas = [], [], [], [], []
    for sq in range(nsub):
        for d, (x_ref, g_ref) in enumerate(((xf_ref, gf_ref), (xb_ref, gbk_ref))):
            rs = chunk_rows(d, sq)
            gb = g_ref[0, rs, :]
            cum = gb
            step = 1
            while step < c:
                if d == 0:
                    cum = cum + jnp.where(rowg >= step, pltpu.roll(cum, step, 0), 0.0)
                else:
                    cum = cum + jnp.where(rowg < c - step, pltpu.roll(cum, c - step, 0), 0.0)
                step *= 2
            for h in range(nh):
                qs.append(x_ref[0, rs, h * DN_DK:(h + 1) * DN_DK].astype(F32))
                ks.append(x_ref[0, rs, DN_W + h * DN_DK:DN_W + (h + 1) * DN_DK].astype(F32))
                vs.append(x_ref[0, rs, 2 * DN_W + h * DN_DK:2 * DN_W + (h + 1) * DN_DK].astype(F32))
                ci = d * nh + h
                gcs.append(jnp.broadcast_to(cum[:, ci:ci + 1], (c, c)))
                betas.append(gb[:, 2 * nh + ci:2 * nh + ci + 1])
    q, k, v, gc_rows, beta = (jnp.stack(a) for a in (qs, ks, vs, gcs, betas))
    gc_cols = jnp.swapaxes(gc_rows, 1, 2)
    decay = masked(incl, jnp.exp(masked(incl, gc_rows - gc_cols)))
    kb = k * beta
    vb = v * beta
    yield
    kq_k = _bdot_nt(jnp.concatenate([kb, q], axis=1), k)
    yield
    nil = masked(strict, kq_k[:, :c] * decay)
    aqk = kq_k[:, c:] * decay
    egc = jnp.exp(gc_rows)
    inv = yield from _unit_triangular_inverse(nil, eye, row, col)
    sol = _bdot(inv, jnp.concatenate([vb, kb * egc], axis=2))
    yield
    u, w = sol[:, :, :DN_DK], sol[:, :, DN_DK:]
    wq = jnp.concatenate([w, q * egc], axis=1)
    s = s_ref[...]
    for sq in range(nsub):
        e = slice(sq * nb, (sq + 1) * nb)
        gc = gc_rows[e]
        wq_s = _bdot(wq[e], s)
        yield
        v_new = u[e] - wq_s[:, :c]
        o = wq_s[:, c:] + _bdot(aqk[e], v_new)
        g_last = jnp.concatenate([gc[:nh, c - 1:c, :], gc[nh:, 0:1, :]], axis=0)
        s = s * jnp.exp(g_last) + _bdot_tn(k[e] * jnp.exp(g_last - gc), v_new)
        yield
        for h in range(nh):
            of_ref[0, chunk_rows(0, sq), h * DN_DK:(h + 1) * DN_DK] = o[h].astype(of_ref.dtype)
            ob_ref[0, chunk_rows(1, sq), h * DN_DK:(h + 1) * DN_DK] = o[nh + h].astype(ob_ref.dtype)
    s_ref[...] = s
    sfin_ref[0] = s


def _dn_scan(qkv, gb, s0):
    b, l, w = qkv.shape
    nsub = DN_STEP_CHUNKS if l % (DN_STEP_CHUNKS * DN_CHUNK) == 0 else 1
    rows = nsub * DN_CHUNK
    n = l // rows
    fwd = lambda bi, t: (bi, t, 0)
    bwd = lambda bi, t: (bi, n - 1 - t, 0)
    state = pl.BlockSpec((1, 2 * DN_HEADS, DN_DK, DN_DK), lambda bi, t: (bi, 0, 0, 0))
    return dict(
        kernel=functools.partial(_dn_scan_kernel, nsub=nsub),
        grid=(b, n),
        in_specs=[pl.BlockSpec((1, rows, w), fwd), pl.BlockSpec((1, rows, w), bwd),
                  pl.BlockSpec((1, rows, LANES), fwd), pl.BlockSpec((1, rows, LANES), bwd),
                  state],
        out_specs=[pl.BlockSpec((1, rows, DN_W), fwd), pl.BlockSpec((1, rows, DN_W), bwd),
                   state],
        out_shape=[jax.ShapeDtypeStruct((b, l, DN_W), BF16), jax.ShapeDtypeStruct((b, l, DN_W), BF16),
                   jax.ShapeDtypeStruct((b, 2 * DN_HEADS, DN_DK, DN_DK), F32)],
        scratch_shapes=[pltpu.VMEM((2 * DN_HEADS, DN_DK, DN_DK), F32)],
        args=(qkv, qkv, gb, gb, s0))


def _run_parts(name, *parts):
    grid = parts[0]["grid"]
    assert all(p["grid"] == grid for p in parts)
    n_in = [len(p["in_specs"]) for p in parts]
    n_out = [len(p["out_specs"]) for p in parts]
    n_scr = [len(p["scratch_shapes"]) for p in parts]

    def body(*refs):
        ins, outs, scr = refs[:sum(n_in)], refs[sum(n_in):sum(n_in) + sum(n_out)], refs[sum(n_in) + sum(n_out):]
        live = []
        for p, ni, no, ns in zip(parts, n_in, n_out, n_scr):
            live.append(p["kernel"](*ins[:ni], *outs[:no], *scr[:ns]))
            ins, outs, scr = ins[ni:], outs[no:], scr[ns:]
        while live:
            for g in list(live):
                if next(g, live) is live:
                    live.remove(g)

    res = pl.pallas_call(
        body,
        grid=grid,
        in_specs=[s for p in parts for s in p["in_specs"]],
        out_specs=[s for p in parts for s in p["out_specs"]],
        out_shape=[s for p in parts for s in p["out_shape"]],
        scratch_shapes=[s for p in parts for s in p["scratch_shapes"]],
        compiler_params=_params("parallel", "arbitrary"),
        name=name,
    )(*[a for p in parts for a in p["args"]])
    out, k = [], 0
    for no in n_out:
        out.append(res[k:k + no])
        k += no
    return out


def _out0_kernel(x_ref, na_ref, of_ref, ob_ref, z_ref, gate_ref, ng_ref, w_ref, o_ref):
    o = of_ref[0].astype(F32) + ob_ref[0].astype(F32)
    segs = [na_ref[0].astype(F32)]
    for h in range(DN_HEADS):
        seg = o[:, h * DN_DK:(h + 1) * DN_DK]
        ms = jnp.mean(seg * seg, axis=-1, keepdims=True)
        segs.append(seg * lax.rsqrt(ms + EPS) * ng_ref[...])
    mix = jnp.concatenate(segs, axis=1) * _silu(z_ref[0].astype(F32))
    o_ref[0] = x_ref[0] + gate_ref[0] * _dot(mix, w_ref[...])


def _out0(x, na, o_f, o_b, z, gate, norm_g, w_out, tm):
    b, l, d = x.shape
    tile = lambda n: pl.BlockSpec((1, tm, n), lambda bi, i: (bi, i, 0))
    return pl.pallas_call(
        _out0_kernel,
        grid=(b, l // tm),
        in_specs=[tile(d), tile(NA_W), tile(DN_W), tile(DN_W), tile(d),
                  pl.BlockSpec((1, 1, d), lambda bi, i: (bi, 0, 0)),
                  pl.BlockSpec((1, DN_DK), lambda bi, i: (0, 0)),
                  pl.BlockSpec(w_out.shape, lambda bi, i: (0, 0))],
        out_specs=tile(d),
        out_shape=jax.ShapeDtypeStruct((b, l, d), F32),
        compiler_params=_params("parallel", "parallel"),
        name="out_proj0",
    )(x, na, o_f, o_b, z, gate, norm_g.reshape(1, DN_DK), w_out.astype(BF16))


def _filter_kernel(feat_ref, decay_ref, w1_ref, b1_ref, f1_ref, w2_ref, b2_ref, f2_ref, w3a_ref, w3lo_ref,
                   h_ref, ss_ref):
    hid = jnp.sin(f1_ref[...] * (jnp.dot(feat_ref[...], w1_ref[...], precision=HI,
                                         preferred_element_type=F32) + b1_ref[...]))
    hid = jnp.sin(f2_ref[...] * (jnp.dot(hid, w2_ref[...], precision=HI,
                                         preferred_element_type=F32) + b2_ref[...]))
    hid_hi = hid.astype(BF16)
    hid_lo = (hid - hid_hi.astype(F32)).astype(BF16)
    h = (jnp.dot(jnp.concatenate([hid_hi, hid_lo], axis=1), w3a_ref[...], preferred_element_type=F32)
         + jnp.dot(hid_hi, w3lo_ref[...], preferred_element_type=F32))
    env = jnp.exp(-feat_ref[:, 0:1] * decay_ref[...])
    h = h * jnp.concatenate([env] * 4, axis=1)
    h_ref[...] = h.astype(h_ref.dtype)

    @pl.when(pl.program_id(0) == 0)
    def _():
        ss_ref[...] = jnp.zeros_like(ss_ref)

    ss_ref[...] += jnp.sum(h * h, axis=0, keepdims=True)


def _hyena_filters_raw(length, w1, b1, f1, w2, b2, f2, w3, tl):
    t = jnp.linspace(0.0, 1.0, length, dtype=F32)[:, None]
    wv = 2.0 * math.pi * jnp.arange(length, dtype=F32)[:, None] / length
    f = jnp.linspace(1e-4, HY_BANDS - 1, HY_BANDS, dtype=F32)[None, :]
    feats = jnp.concatenate([t, jnp.cos(f * wv), -jnp.sin(f * wv)], axis=-1)
    decay = jnp.abs(jnp.linspace(HY_DECAY_MIN, HY_DECAY_MAX, HY_W, dtype=F32))
    feats = _time_major_to_pq(feats[None])[0].reshape(length, -1)
    ffn = w1.shape[1]
    pe, pf = LANES - HY_EMB, LANES - ffn
    feats = jnp.pad(feats, ((0, 0), (0, pe)))
    w1p = jnp.pad(w1, ((0, pe), (0, pf)))
    w2p = jnp.pad(w2, ((0, pf), (0, pf)))
    w3p = jnp.pad(w3, ((0, pf), (0, 0)))
    w3_hi = w3p.astype(BF16)
    w3_lo = (w3p - w3_hi.astype(F32)).astype(BF16)
    w3a = jnp.concatenate([w3_hi, w3_hi], axis=0)
    vec = lambda a: jnp.pad(a.reshape(1, ffn), ((0, 0), (0, pf)))
    n_out = w3.shape[1]
    const = lambda shape: pl.BlockSpec(shape, lambda i: (0, 0))
    return pl.pallas_call(
        _filter_kernel,
        grid=(length // tl,),
        in_specs=[pl.BlockSpec((tl, LANES), lambda i: (i, 0)), const((1, HY_W)),
                  const((LANES, LANES)), const((1, LANES)), const((1, LANES)),
                  const((LANES, LANES)), const((1, LANES)), const((1, LANES)),
                  const((2 * LANES, n_out)), const((LANES, n_out))],
        out_specs=[pl.BlockSpec((tl, n_out), lambda i: (i, 0)), const((1, n_out))],
        out_shape=[jax.ShapeDtypeStruct((length, n_out), BF16), jax.ShapeDtypeStruct((1, n_out), F32)],
        compiler_params=_params("arbitrary"),
        name="hyena_filter_ffn",
    )(feats, decay.reshape(1, HY_W), w1p, vec(b1), vec(f1), w2p, vec(b2), vec(f2), w3a, w3_lo)


HY_P = 64
HY_CB = 2 * LANES
HY_KQ_PAD = 8
HY_ROW_PAD = 8


def _time_major_to_pq(a):
    b, l, c = a.shape
    return a.reshape(b, l // HY_P, HY_P, c).transpose(0, 2, 1, 3)


def _pq_to_time_major(a):
    b, p, q, c = a.shape
    return a.transpose(0, 2, 1, 3).reshape(b, p * q, c)


def _fft_tables(length):
    p_n = HY_P
    q_n = length // p_n
    qn2 = 2 * q_n
    n = 2 * length
    kq_n = q_n + HY_KQ_PAD
    kq = np.arange(kq_n)
    live = (kq <= q_n)[:, None]
    a1 = 2.0 * np.pi * ((kq[:, None] * np.arange(q_n)[None, :]) % qn2) / qn2
    c1, s1 = np.cos(a1) * live, np.sin(a1) * live
    g1 = np.concatenate([c1, -s1], axis=0)
    g1i = np.concatenate([c1.T, -s1.T], axis=1)
    bf = lambda m: jnp.asarray(m, F32).astype(BF16)
    pp = jnp.arange(p_n, dtype=jnp.int32)
    k = jnp.arange(kq_n, dtype=jnp.int32)[:, None, None] + qn2 * pp[None, :, None]
    ang = ((k * pp[None, None, :]) % n).astype(F32) * (2.0 * math.pi / n)
    c2, s2 = jnp.cos(ang), jnp.sin(ang)
    f2t = jnp.concatenate([jnp.concatenate([c2, s2], axis=2), jnp.concatenate([-s2, c2], axis=2)], axis=1)
    c2t, s2t = jnp.swapaxes(c2, 1, 2), jnp.swapaxes(s2, 1, 2)
    f2it = jnp.concatenate([jnp.concatenate([c2t, -s2t], axis=2), jnp.concatenate([s2t, c2t], axis=2)], axis=1)
    return bf(g1), bf(g1i), f2t.astype(BF16), f2it.astype(BF16)


def _conv3_slabs(ref, w_ref):
    p_n, q_n = ref.shape[1], ref.shape[2]
    w = w_ref[...]
    zero = jnp.zeros((1, ref.shape[3]), F32)

    def raw(p):
        return ref[0, p].astype(F32)

    def slab(p):
        prev = raw(p - 1) if p > 0 else jnp.concatenate([zero, raw(p_n - 1)[:q_n - 1]], axis=0)
        nxt = raw(p + 1) if p < p_n - 1 else jnp.concatenate([raw(0)[1:], zero], axis=0)
        return prev * w[0:1, :] + raw(p) * w[1:2, :] + nxt * w[2:3, :]

    return slab


def _lane_slabs(n):
    return [slice(i * LANES, (i + 1) * LANES) for i in range(n)]


def _fft_forward(slab, p_n, g1_ref, sre_ref, sim_ref):
    kq_n = g1_ref.shape[0] // 2
    pitch = p_n + HY_ROW_PAD
    for p in range(p_n):
        a = jnp.dot(g1_ref[...], slab(p).astype(BF16), preferred_element_type=F32)
        for i, lanes in enumerate(_lane_slabs(sre_ref.shape[0])):
            sre_ref[i, pl.ds(p, kq_n, stride=pitch), :] = a[:kq_n, lanes]
            sim_ref[i, pl.ds(p, kq_n, stride=pitch), :] = a[kq_n:, lanes]


def _fft_slab_spectrum(kq, f2t_ref, sre_ref, sim_ref, p_n):
    base = kq * (p_n + HY_ROW_PAD)
    ns = sre_ref.shape[0]
    slab = jnp.concatenate(
        [jnp.concatenate([ref[i, base:base + p_n, :] for i in range(ns)], axis=1) for ref in (sre_ref, sim_ref)],
        axis=0)
    return base, jnp.dot(f2t_ref[kq], slab.astype(BF16), preferred_element_type=F32)


def _fftconv_kernel(x_ref, gate_ref, xw_ref, gw_ref, h_ref, g1_ref, g1i_ref, f2t_ref, f2it_ref, o_ref,
                    sre_ref, sim_ref, *, conv_x):
    p_n, q_n = x_ref.shape[1], x_ref.shape[2]
    kq_n = g1_ref.shape[0] // 2
    pitch = p_n + HY_ROW_PAD
    ns = sre_ref.shape[0]
    x_slab = _conv3_slabs(x_ref, xw_ref) if conv_x else (lambda p: x_ref[0, p])
    gate_slab = _conv3_slabs(gate_ref, gw_ref)
    _fft_forward(x_slab, p_n, g1_ref, sre_ref, sim_ref)

    for kq in range(q_n + 1):
        base, x = _fft_slab_spectrum(kq, f2t_ref, sre_ref, sim_ref, p_n)
        xr, xi = x[:p_n], x[p_n:]
        h = h_ref[0, kq].astype(F32)
        hr, hi = h[:p_n], h[p_n:]
        y = jnp.concatenate([xr * hr - xi * hi, xr * hi + xi * hr], axis=0)
        z = jnp.dot(f2it_ref[kq], y.astype(BF16), preferred_element_type=F32)
        for i, lanes in enumerate(_lane_slabs(ns)):
            sre_ref[i, base:base + p_n, :] = z[:p_n, lanes]
            sim_ref[i, base:base + p_n, :] = z[p_n:, lanes]

    for p in range(p_n):
        z = jnp.concatenate(
            [jnp.concatenate([ref[i, pl.ds(p, kq_n, stride=pitch), :] for i in range(ns)], axis=1)
             for ref in (sre_ref, sim_ref)], axis=0)
        y = jnp.dot(g1i_ref[...], z.astype(BF16), preferred_element_type=F32)
        o_ref[0, p] = (gate_slab(p) * y).astype(o_ref.dtype)


def _filter_response_kernel(hf_ref, hb_ref, ssf_ref, ssb_ref, skip_ref, g1_ref, f2t_ref, o_ref,
                            sre_ref, sim_ref, spec_ref, *, n):
    p_n, q_n = hf_ref.shape[1], hf_ref.shape[2]
    nf = lax.rsqrt(ssf_ref[...] + EPS)
    nb = lax.rsqrt(ssb_ref[...] + EPS)
    _fft_forward(lambda p: hf_ref[0, p], p_n, g1_ref, sre_ref, sim_ref)
    for kq in range(q_n + 1):
        spec_ref[kq] = _fft_slab_spectrum(kq, f2t_ref, sre_ref, sim_ref, p_n)[1]
    _fft_forward(lambda p: hb_ref[0, p], p_n, g1_ref, sre_ref, sim_ref)
    for kq in range(q_n + 1):
        xb = _fft_slab_spectrum(kq, f2t_ref, sre_ref, sim_ref, p_n)[1]
        xf = spec_ref[kq]
        coef = (1.0 if kq in (0, q_n) else 2.0) / n
        hr = (xf[:p_n] * nf + xb[:p_n] * nb + skip_ref[0]) * coef
        hi = (xf[p_n:] * nf - xb[p_n:] * nb) * coef
        o_ref[0, kq] = jnp.concatenate([hr, hi], axis=0).astype(o_ref.dtype)


def _fft_scratch(kq_n, p_n):
    rows = kq_n * (p_n + HY_ROW_PAD)
    ns = HY_CB // LANES
    return [pltpu.VMEM((ns, rows, LANES), F32), pltpu.VMEM((ns, rows, LANES), F32)]


def _const_spec(a):
    return pl.BlockSpec(a.shape, lambda *_: (0,) * a.ndim)


def _filter_response(h_pq, ss, skip, tables):
    g1, _, f2t, _ = tables
    _, p_n, q_n, _ = h_pq.shape
    kq_n = g1.shape[0] // 2
    nb = HY_W // HY_CB
    taps = lambda d: pl.BlockSpec((1, p_n, q_n, HY_CB), lambda o, j: (0, 0, 0, (2 * o + d) * nb + j))
    sumsq = lambda d: pl.BlockSpec((1, HY_CB), lambda o, j: (0, (2 * o + d) * nb + j))
    return pl.pallas_call(
        functools.partial(_filter_response_kernel, n=2 * p_n * q_n),
        grid=(2, nb),
        in_specs=[taps(0), taps(1), sumsq(0), sumsq(1), pl.BlockSpec((1, 1, HY_CB), lambda o, j: (o, 0, j)),
                  _const_spec(g1), _const_spec(f2t)],
        out_specs=pl.BlockSpec((1, q_n + 1, 2 * p_n, HY_CB), lambda o, j: (o, 0, 0, j)),
        out_shape=jax.ShapeDtypeStruct((2, q_n + 1, 2 * p_n, HY_W), BF16),
        scratch_shapes=_fft_scratch(kq_n, p_n) + [pltpu.VMEM((q_n + 1, 2 * p_n, HY_CB), F32)],
        compiler_params=_params("parallel", "parallel"),
        name="hyena_filter_response",
    )(h_pq, h_pq, ss, ss, skip.reshape(2, 1, HY_W), g1, f2t)


def _long_conv_gated(u, u_block0, conv_x, p_u, gate_block0, conv_w, resp, order, tables):
    g1, g1i, f2t, f2it = tables
    b, p_n, q_n, _ = u.shape
    kq_n = g1.shape[0] // 2
    blk = lambda off: pl.BlockSpec((1, p_n, q_n, HY_CB), lambda j, bi: (bi, 0, 0, off + j))
    taps = lambda off: pl.BlockSpec((3, HY_CB), lambda j, bi: (0, off + j))
    return pl.pallas_call(
        functools.partial(_fftconv_kernel, conv_x=conv_x),
        grid=(HY_W // HY_CB, b),
        in_specs=[blk(u_block0), blk(gate_block0), taps(u_block0 if conv_x else 0), taps(gate_block0),
                  pl.BlockSpec((1, q_n + 1, 2 * p_n, HY_CB), lambda j, bi: (order, 0, 0, j)),
                  _const_spec(g1), _const_spec(g1i), _const_spec(f2t), _const_spec(f2it)],
        out_specs=blk(0),
        out_shape=jax.ShapeDtypeStruct((b, p_n, q_n, HY_W), BF16),
        scratch_shapes=_fft_scratch(kq_n, p_n),
        compiler_params=_params("parallel", "parallel"),
        name="hyena_fftconv",
    )(u, p_u, conv_w, conv_w, resp, g1, g1i, f2t, f2it)


def _out1_kernel(x_ref, y_ref, z_ref, gate_ref, w_ref, fg_ref, o_ref):
    r = x_ref[0] + gate_ref[0] * _dot(y_ref[0].astype(F32) * _silu(z_ref[0].astype(F32)), w_ref[...])
    ms = jnp.mean(r * r, axis=-1, keepdims=True)
    o_ref[0] = r * lax.rsqrt(ms + EPS) * fg_ref[...]


def _out1(x, y, z, gate, w_out, final_g, tm):
    b, l, d = x.shape
    tile = pl.BlockSpec((1, tm, d), lambda bi, i: (bi, i, 0))
    return pl.pallas_call(
        _out1_kernel,
        grid=(b, l // tm),
        in_specs=[tile, tile, tile, pl.BlockSpec((1, 1, d), lambda bi, i: (bi, 0, 0)),
                  pl.BlockSpec(w_out.shape, lambda bi, i: (0, 0)), pl.BlockSpec((1, d), lambda bi, i: (0, 0))],
        out_specs=tile,
        out_shape=jax.ShapeDtypeStruct((b, l, d), F32),
        compiler_params=_params("parallel", "parallel"),
        name="out_proj1",
    )(x, y, z, gate, w_out.astype(BF16), final_g.reshape(1, d))


def _row_tile(l, want):
    return want if l % want == 0 else l


def _even_layer(x, ctx, c, c_ctx, norm_g, mod_w, mod_b, w_in, rpb, dn_conv, a_log, dt_bias, dn_norm_g, w_out):
    b, l, d = x.shape
    lc = ctx.shape[1]
    off_dn = 3 * NA_W
    off_ab = off_dn + DN_CONV_W
    off_z = off_ab + 4 * DN_HEADS
    rows = b + 1
    pad = (-rows) % 8
    cvecs = jnp.concatenate([c, c_ctx[None, :], jnp.zeros((pad, d), F32)], axis=0)
    m = _modulation(cvecs, mod_w, mod_b)
    shift, scale, gate = (m[:, i * d:(i + 1) * d] for i in range(3))
    lat = lambda a: a[:b, None, :]
    cx = lambda a: jnp.broadcast_to(a[b:b + 1, None, :], (b, 1, d))

    w_na = jnp.concatenate([w_in[:, :NA_W] * (NA_DH ** -0.5 * LOG2E), w_in[:, NA_W:off_dn]], axis=1)
    w_dn = w_in[:, off_dn:off_ab]
    w_ab = jnp.pad(w_in[:, off_ab:off_z], ((0, 0), (0, LANES - 4 * DN_HEADS)))
    w_z = w_in[:, off_z:]
    qkv_x, dn_x, gb_x, z_x = _norm_proj0(x, norm_g, lat(shift), lat(scale), w_na, w_dn, w_ab, w_z,
                                         dn_conv, a_log, dt_bias, _row_tile(l, 512))
    qkv_c, dn_c, gb_c, _ = _norm_proj0(ctx, norm_g, cx(shift), cx(scale), w_na, w_dn, w_ab, w_z,
                                       dn_conv, a_log, dt_bias, _row_tile(lc, 256))

    s0 = jnp.zeros((b, 2 * DN_HEADS, DN_DK, DN_DK), F32)
    ((_, _, s_ctx),) = _run_parts("dn_scan_ctx", _dn_scan(dn_c, gb_c, s0))
    (o_f, o_b, _), (na_x,) = _run_parts("dn_scan_and_attention", _dn_scan(dn_x, gb_x, s_ctx),
                                        _neighbourhood_attention(qkv_x, qkv_c, rpb))

    return _out0(x, na_x, o_f, o_b, z_x, lat(gate), dn_norm_g, w_out, _row_tile(l, 512))


def _hyena_layer(x, c, norm_g, mod_w, mod_b, w_in, conv_w, fw1, fb1, ff1, fw2, fb2, ff2, fw3, skip, w_out,
                 final_g):
    b, l, d = x.shape
    pad = (-b) % 8
    cvecs = jnp.concatenate([c, jnp.zeros((pad, d), F32)], axis=0)
    m = _modulation(cvecs, mod_w, mod_b)
    shift, scale, gate = (m[:b, None, i * d:(i + 1) * d] for i in range(3))
    q_n = l // HY_P
    xt = _time_major_to_pq(x).reshape(b, l, d)
    p_u, gz = _norm_proj(xt, norm_g, shift, scale,
                         [(w_in[:, :3 * HY_W], BF16), (w_in[:, 3 * HY_W:], BF16)],
                         _row_tile(l, 1024))
    p_u = p_u.reshape(b, HY_P, q_n, 3 * HY_W)

    h_raw, ss = _hyena_filters_raw(l, fw1, fb1, ff1, fw2, fb2, ff2, fw3, _row_tile(l, 256))
    tables = _fft_tables(l)
    resp = _filter_response(h_raw.reshape(1, HY_P, q_n, h_raw.shape[1]), ss, skip, tables)

    nb = HY_W // HY_CB
    z = _long_conv_gated(p_u, 0, True, p_u, nb, conv_w, resp, 0, tables)
    y = _long_conv_gated(z, 0, False, p_u, 2 * nb, conv_w, resp, 1, tables)
    out = _out1(xt, y.reshape(b, l, HY_W), gz, gate, w_out, final_g, _row_tile(l, 512))
    return _pq_to_time_major(out.reshape(b, HY_P, q_n, d))


def kernel(x, c, ctx, c_ctx, e_norm_g, e_mod_w, e_mod_b, e_w_in, e_na_rpb, e_dn_conv, e_dn_a_log, e_dn_dt_bias, e_dn_norm_g, e_w_out, o_norm_g, o_mod_w, o_mod_b, o_w_in, o_hy_conv, o_ffn_w1, o_ffn_b1, o_ffn_f1, o_ffn_w2, o_ffn_b2, o_ffn_f2, o_ffn_w3, o_hy_skip, o_w_out, final_norm_g):
    x = _even_layer(x, ctx, c, c_ctx, e_norm_g[0], e_mod_w[0], e_mod_b[0], e_w_in[0], e_na_rpb[0],
                    e_dn_conv[0], e_dn_a_log[0], e_dn_dt_bias[0], e_dn_norm_g[0], e_w_out[0])
    return _hyena_layer(x, c, o_norm_g[0], o_mod_w[0], o_mod_b[0], o_w_in[0], o_hy_conv[0],
                        o_ffn_w1[0], o_ffn_b1[0], o_ffn_f1[0], o_ffn_w2[0], o_ffn_b2[0], o_ffn_f2[0],
                        o_ffn_w3[0], o_hy_skip[0], o_w_out[0], final_norm_g)
```
